```python
import math
import jax, jax.numpy as jnp
from jax import lax
import numpy as np

D_MODEL = 2048
BATCH = 8
SEQ = 2048
DEPTH = 2
DEC_BATCH = 128
DEC_SEQ = 1
PAST_LEN = 8192
PAGE_SIZE = 128

N_HEADS = 16
N_KV_HEADS = 4
HEAD_DIM = 64
Q_GROUP = N_HEADS // N_KV_HEADS
WINDOW = 128
ATTN_WIDTH = N_HEADS * HEAD_DIM
KV_WIDTH = N_KV_HEADS * HEAD_DIM
REL_BUCKETS = 32
REL_MAX_DIST = 128
POOL_WINDOWS = (2, 4, 8, 16)
N_POOL_GROUPS = 4
POOL_CH = D_MODEL // 4
POOL_GROUP_CH = POOL_CH // N_POOL_GROUPS
POOL_BUF = max(POOL_WINDOWS) - 1
N_MEM = 256
MEM_HEADS = 4
MEM_HEAD_DIM = 128
MEM_WIDTH = MEM_HEADS * MEM_HEAD_DIM
N_BRANCHES = 3
Q_OFF = 0
K_OFF = Q_OFF + ATTN_WIDTH
V_OFF = K_OFF + KV_WIDTH
P_OFF = V_OFF + KV_WIDTH
M_OFF = P_OFF + POOL_CH
G_OFF = M_OFF + MEM_WIDTH
IN_COLS = G_OFF + N_BRANCHES * D_MODEL
DFF_DENSE = 5632
N_EXPERTS = 8
TOP_K = 2
DFF_EXPERT = 7168
N_DENSE_LAYERS = (DEPTH + 1) // 2
N_MOE_LAYERS = DEPTH // 2
RMS_EPS = 1e-5
NEG_INF = -1e30

kernel_name = 'hybrid_swa_pool_memory_decoder_step'


def _rmsnorm(x, g):
    x32 = x.astype(jnp.float32)
    y = x32 * lax.rsqrt(jnp.mean(x32 * x32, axis=-1, keepdims=True) + RMS_EPS)
    return (y * g.astype(jnp.float32)).astype(x.dtype)


def _rel_bias_by_distance(rel_bias_table):
    d = jnp.arange(WINDOW, dtype=jnp.int32)
    max_exact = REL_BUCKETS // 2
    d_f = jnp.maximum(d, 1).astype(jnp.float32)
    large = max_exact + (jnp.log(d_f / max_exact) / math.log(REL_MAX_DIST / max_exact)
                         * (REL_BUCKETS - max_exact)).astype(jnp.int32)
    large = jnp.minimum(large, REL_BUCKETS - 1)
    bucket = jnp.where(d < max_exact, d, large)
    return rel_bias_table[bucket].T


def _swa_attend(q, k, v, dist, valid, bias_hw, sink):
    tq, tk = dist.shape
    logits = jnp.einsum('bnqhgd,bnshd->bnhgqs', q, k).astype(jnp.float32) * (HEAD_DIM ** -0.5)
    bias = bias_hw.astype(jnp.float32)[:, jnp.clip(dist, 0, WINDOW - 1)].reshape(N_KV_HEADS, Q_GROUP, tq, tk)
    logits = jnp.where(valid[None, :, None, None], logits + bias[None, None], NEG_INF)
    s = sink.astype(jnp.float32).reshape(N_KV_HEADS, Q_GROUP)[None, None, :, :, None, None]
    m = jnp.maximum(jnp.max(logits, axis=-1, keepdims=True), s)
    p = jnp.exp(logits - m)
    denom = jnp.sum(p, axis=-1, keepdims=True) + jnp.exp(s - m)
    probs = (p / denom).astype(v.dtype)
    return jnp.einsum('bnhgqs,bnshd->bnqhgd', probs, v)


def _swa_prompt(q, k, v, bias_hw, sink):
    b, t = q.shape[:2]
    nb = t // WINDOW
    qb = q.reshape(b, nb, WINDOW, N_KV_HEADS, Q_GROUP, HEAD_DIM)
    pad = ((0, 0), (WINDOW, 0), (0, 0), (0, 0))
    kb = jnp.pad(k, pad).reshape(b, nb + 1, WINDOW, N_KV_HEADS, HEAD_DIM)
    vb = jnp.pad(v, pad).reshape(b, nb + 1, WINDOW, N_KV_HEADS, HEAD_DIM)
    kw = jnp.concatenate([kb[:, :-1], kb[:, 1:]], axis=2)
    vw = jnp.concatenate([vb[:, :-1], vb[:, 1:]], axis=2)
    i = jnp.arange(WINDOW)[:, None]
    j = jnp.arange(2 * WINDOW)[None, :]
    dist = i + WINDOW - j
    in_band = (dist >= 0) & (dist < WINDOW)
    blk = jnp.arange(nb)[:, None, None]
    valid = in_band[None] & ((blk > 0) | (j[None] >= WINDOW))
    out = _swa_attend(qb, kw, vw, dist, valid, bias_hw, sink)
    return out.reshape(b, t, ATTN_WIDTH)


def _swa_sample(q, k, v, k_buf, v_buf, bias_hw, sink):
    b, t = q.shape[:2]
    kc = jnp.concatenate([k_buf, k], axis=1)
    vc = jnp.concatenate([v_buf, v], axis=1)
    i = jnp.arange(t)[:, None]
    j = jnp.arange(WINDOW + t)[None, :]
    dist = i + WINDOW - j
    valid = ((dist >= 0) & (dist < WINDOW))[None]
    qb = q.reshape(b, 1, t, N_KV_HEADS, Q_GROUP, HEAD_DIM)
    out = _swa_attend(qb, kc[:, None], vc[:, None], dist, valid, bias_hw, sink)
    return out.reshape(b, t, ATTN_WIDTH), kc[:, -WINDOW:], vc[:, -WINDOW:]


def _pool_mixer(p, prefix, t0, pool_maps, pool_scale):
    b, t, _ = p.shape
    ext = jnp.concatenate([prefix, p], axis=1).astype(jnp.float32)
    csum = jnp.pad(jnp.cumsum(ext, axis=1), ((0, 0), (1, 0), (0, 0)))
    pos = t0 + jnp.arange(t)
    outs = []
    for g, w in enumerate(POOL_WINDOWS):
        c0 = g * POOL_GROUP_CH
        c1 = c0 + POOL_GROUP_CH
        hi = csum[:, POOL_BUF + 1:POOL_BUF + 1 + t, c0:c1]
        lo = csum[:, POOL_BUF + 1 - w:POOL_BUF + 1 - w + t, c0:c1]
        cnt = jnp.minimum(pos + 1, w).astype(jnp.float32)[None, :, None]
        outs.append((hi - lo) / cnt)
    pooled = jnp.stack(outs, axis=2)
    diff = pooled - p.astype(jnp.float32).reshape(b, t, N_POOL_GROUPS, POOL_GROUP_CH)
    mixed = jnp.einsum('btgc,gcd->btgd', diff.astype(p.dtype), pool_maps)
    return mixed.reshape(b, t, POOL_CH) * pool_scale


def _memory_kv(mem, g_mem, w_mem_kv):
    b, n = mem.shape[:2]
    kv = _rmsnorm(mem, g_mem) @ w_mem_kv
    mk = kv[..., :MEM_WIDTH].reshape(b, n, MEM_HEADS, MEM_HEAD_DIM)
    mv = kv[..., MEM_WIDTH:].reshape(b, n, MEM_HEADS, MEM_HEAD_DIM)
    return mk, mv


def _mem_attend(qm, mk, mv):
    b, t = qm.shape[:2]
    logits = jnp.einsum('bqhd,bshd->bhqs', qm, mk).astype(jnp.float32) * (MEM_HEAD_DIM ** -0.5)
    probs = jax.nn.softmax(logits, axis=-1).astype(mv.dtype)
    return jnp.einsum('bhqs,bshd->bqhd', probs, mv).reshape(b, t, MEM_WIDTH)


def _mixer_sublayer(x, swa_buf_k, swa_buf_v, pool_prefix, t0, mem_k, mem_v, bias_hw,
                    g_attn, w_in, sink, pool_maps, pool_scale, w_up_attn, w_up_pool, w_up_mem, w_o):
    b, t, _ = x.shape
    xn = _rmsnorm(x, g_attn)
    z = xn @ w_in
    q = z[..., Q_OFF:K_OFF].reshape(b, t, N_HEADS, HEAD_DIM)
    k = z[..., K_OFF:V_OFF].reshape(b, t, N_KV_HEADS, HEAD_DIM)
    v = z[..., V_OFF:P_OFF].reshape(b, t, N_KV_HEADS, HEAD_DIM)
    p = z[..., P_OFF:M_OFF]
    qm = z[..., M_OFF:G_OFF].reshape(b, t, MEM_HEADS, MEM_HEAD_DIM)
    gates = jax.nn.sigmoid(z[..., G_OFF:].astype(jnp.float32)).astype(x.dtype).reshape(b, t, N_BRANCHES, D_MODEL)
    if swa_buf_k is None:
        ya = _swa_prompt(q, k, v, bias_hw, sink)
        new_k, new_v = k[:, -WINDOW:], v[:, -WINDOW:]
    else:
        ya, new_k, new_v = _swa_sample(q, k, v, swa_buf_k, swa_buf_v, bias_hw, sink)
    yp = _pool_mixer(p, pool_prefix, t0, pool_maps, pool_scale)
    new_pool = jnp.concatenate([pool_prefix, p], axis=1)[:, -POOL_BUF:]
    ym = _mem_attend(qm, mem_k, mem_v)
    merged = (gates[:, :, 0] * (ya @ w_up_attn)
              + gates[:, :, 1] * (yp @ w_up_pool)
              + gates[:, :, 2] * (ym @ w_up_mem))
    return x + merged @ w_o, new_k, new_v, new_pool


def _swiglu(x, w1, w3, w2):
    return (jax.nn.silu(x @ w1) * (x @ w3)) @ w2


def _moe(xf, router, w1, w3, w2):
    logits = (xf @ router).astype(jnp.float32)
    top_vals, top_idx = lax.top_k(logits, TOP_K)
    top_w = jax.nn.softmax(top_vals, axis=-1)
    gates = jnp.sum(jax.nn.one_hot(top_idx, N_EXPERTS, dtype=jnp.float32) * top_w[..., None], axis=-2).astype(xf.dtype)
    y = jnp.zeros_like(xf)
    for e in range(N_EXPERTS):
        y = y + gates[:, e:e + 1] * _swiglu(xf, w1[e], w3[e], w2[e])
    return y


def _ffn_sublayer(x, l, g_ffn, ffn_w1, ffn_w3, ffn_w2, moe_router, moe_w1, moe_w3, moe_w2):
    b, t, d = x.shape
    xn = _rmsnorm(x, g_ffn[l])
    i = l // 2
    if l % 2 == 0:
        y = _swiglu(xn, ffn_w1[i], ffn_w3[i], ffn_w2[i])
    else:
        y = _moe(xn.reshape(b * t, d), moe_router[i], moe_w1[i], moe_w3[i], moe_w2[i]).reshape(b, t, d)
    return x + y


def setup_inputs(seed: int = 0) -> dict:
    key = jax.random.key(seed)
    ks = jax.random.split(key, 28)

    def nrm(k, shape, scale):
        return jax.random.normal(k, shape, jnp.float32) * scale

    return {
        'x_prompt': nrm(ks[0], (BATCH, SEQ, D_MODEL), 1.0),
        'x_sample': nrm(ks[1], (DEC_BATCH, DEC_SEQ, D_MODEL), 1.0),
        'mem_prompt': nrm(ks[2], (BATCH, N_MEM, D_MODEL), 1.0),
        'cache_swa_k': nrm(ks[3], (DEPTH, DEC_BATCH, WINDOW, N_KV_HEADS, HEAD_DIM), 1.0),
        'cache_swa_v': nrm(ks[4], (DEPTH, DEC_BATCH, WINDOW, N_KV_HEADS, HEAD_DIM), 1.0),
        'state_pool': nrm(ks[5], (DEPTH, DEC_BATCH, POOL_BUF, POOL_CH), 1.0),
        'cache_mem_k': nrm(ks[6], (DEPTH, DEC_BATCH, N_MEM, MEM_HEADS, MEM_HEAD_DIM), 1.0),
        'cache_mem_v': nrm(ks[7], (DEPTH, DEC_BATCH, N_MEM, MEM_HEADS, MEM_HEAD_DIM), 1.0),
        'rel_bias_table': nrm(ks[8], (REL_BUCKETS, N_HEADS), 0.5),
        'g_attn': 1.0 + nrm(ks[9], (DEPTH, D_MODEL), 0.02),
        'w_in': nrm(ks[10], (DEPTH, D_MODEL, IN_COLS), D_MODEL ** -0.5),
        'attn_sinks': nrm(ks[11], (DEPTH, N_HEADS), 1.0),
        'pool_maps': nrm(ks[12], (DEPTH, N_POOL_GROUPS, POOL_GROUP_CH, POOL_GROUP_CH), POOL_GROUP_CH ** -0.5),
        'pool_scale': 1.0 + nrm(ks[13], (DEPTH, POOL_CH), 0.02),
        'g_mem': 1.0 + nrm(ks[14], (DEPTH, D_MODEL), 0.02),
        'w_mem_kv': nrm(ks[15], (DEPTH, D_MODEL, 2 * MEM_WIDTH), D_MODEL ** -0.5),
        'w_up_attn': nrm(ks[16], (DEPTH, ATTN_WIDTH, D_MODEL), ATTN_WIDTH ** -0.5),
        'w_up_pool': nrm(ks[17], (DEPTH, POOL_CH, D_MODEL), POOL_CH ** -0.5),
        'w_up_mem': nrm(ks[18], (DEPTH, MEM_WIDTH, D_MODEL), MEM_WIDTH ** -0.5),
        'w_o': nrm(ks[19], (DEPTH, D_MODEL, D_MODEL), D_MODEL ** -0.5),
        'g_ffn': 1.0 + nrm(ks[20], (DEPTH, D_MODEL), 0.02),
        'ffn_w1': nrm(ks[21], (N_DENSE_LAYERS, D_MODEL, DFF_DENSE), D_MODEL ** -0.5),
        'ffn_w3': nrm(ks[22], (N_DENSE_LAYERS, D_MODEL, DFF_DENSE), D_MODEL ** -0.5),
        'ffn_w2': nrm(ks[23], (N_DENSE_LAYERS, DFF_DENSE, D_MODEL), DFF_DENSE ** -0.5),
        'moe_router': nrm(ks[24], (N_MOE_LAYERS, D_MODEL, N_EXPERTS), D_MODEL ** -0.5),
        'moe_w1': nrm(ks[25], (N_MOE_LAYERS, N_EXPERTS, D_MODEL, DFF_EXPERT), D_MODEL ** -0.5),
        'moe_w3': nrm(ks[26], (N_MOE_LAYERS, N_EXPERTS, D_MODEL, DFF_EXPERT), D_MODEL ** -0.5),
        'moe_w2': nrm(ks[27], (N_MOE_LAYERS, N_EXPERTS, DFF_EXPERT, D_MODEL), DFF_EXPERT ** -0.5),
        'g_final': 1.0 + nrm(jax.random.fold_in(key, 99), (D_MODEL,), 0.02),
    }


def reference(x_prompt, x_sample, mem_prompt, cache_swa_k, cache_swa_v, state_pool, cache_mem_k, cache_mem_v,
              rel_bias_table, g_attn, w_in, attn_sinks, pool_maps, pool_scale, g_mem, w_mem_kv,
              w_up_attn, w_up_pool, w_up_mem, w_o, g_ffn, ffn_w1, ffn_w3, ffn_w2,
              moe_router, moe_w1, moe_w3, moe_w2, g_final):
    bias_hw = _rel_bias_by_distance(rel_bias_table)
    xp, xs = x_prompt, x_sample
    p_k, p_v, p_pool, p_mk, p_mv = [], [], [], [], []
    s_k, s_v, s_pool = [], [], []
    pool_zero_prefix = jnp.zeros((xp.shape[0], POOL_BUF, POOL_CH), xp.dtype)
    for l in range(DEPTH):
        lw = (g_attn[l], w_in[l], attn_sinks[l], pool_maps[l], pool_scale[l],
              w_up_attn[l], w_up_pool[l], w_up_mem[l], w_o[l])
        mk, mv = _memory_kv(mem_prompt, g_mem[l], w_mem_kv[l])
        xp, nk, nv, npool = _mixer_sublayer(xp, None, None, pool_zero_prefix, 0, mk, mv, bias_hw, *lw)
        p_k.append(nk)
        p_v.append(nv)
        p_pool.append(npool)
        p_mk.append(mk)
        p_mv.append(mv)
        xs, nk, nv, npool = _mixer_sublayer(xs, cache_swa_k[l], cache_swa_v[l], state_pool[l], PAST_LEN,
                                            cache_mem_k[l], cache_mem_v[l], bias_hw, *lw)
        s_k.append(nk)
        s_v.append(nv)
        s_pool.append(npool)
        xp = _ffn_sublayer(xp, l, g_ffn, ffn_w1, ffn_w3, ffn_w2, moe_router, moe_w1, moe_w3, moe_w2)
        xs = _ffn_sublayer(xs, l, g_ffn, ffn_w1, ffn_w3, ffn_w2, moe_router, moe_w1, moe_w3, moe_w2)
    y_prompt = _rmsnorm(xp, g_final)
    y_sample = _rmsnorm(xs, g_final)
    return (y_prompt, y_sample,
            jnp.stack(p_k), jnp.stack(p_v), jnp.stack(p_pool), jnp.stack(p_mk), jnp.stack(p_mv),
            jnp.stack(s_k), jnp.stack(s_v), jnp.stack(s_pool))
```

```python
import functools
import math

import jax
import jax.numpy as jnp
from jax import lax
from jax.experimental import pallas as pl
from jax.experimental.pallas import tpu as pltpu

F32 = jnp.float32
BF16 = jnp.bfloat16

D_MODEL = 2048
N_HEADS = 16
N_KV_HEADS = 4
HEAD_DIM = 64
Q_GROUP = N_HEADS // N_KV_HEADS
WINDOW = 128
ATTN_WIDTH = N_HEADS * HEAD_DIM
KV_WIDTH = N_KV_HEADS * HEAD_DIM
REL_BUCKETS = 32
REL_MAX_DIST = 128
POOL_WINDOWS = (2, 4, 8, 16)
N_POOL_GROUPS = 4
POOL_CH = D_MODEL // 4
POOL_GROUP_CH = POOL_CH // N_POOL_GROUPS
POOL_BUF = max(POOL_WINDOWS) - 1
N_MEM = 256
MEM_HEADS = 4
MEM_HEAD_DIM = 128
MEM_WIDTH = MEM_HEADS * MEM_HEAD_DIM
N_BRANCHES = 3
Q_OFF = 0
K_OFF = Q_OFF + ATTN_WIDTH
V_OFF = K_OFF + KV_WIDTH
P_OFF = V_OFF + KV_WIDTH
M_OFF = P_OFF + POOL_CH
G_OFF = M_OFF + MEM_WIDTH
IN_COLS = G_OFF + N_BRANCHES * D_MODEL
N_EXPERTS = 8
TOP_K = 2
RMS_EPS = 1e-5
NEG_INF = -1e30
PAST_LEN = 8192

VMEM_LIMIT_BYTES = 56 * 1024 * 1024
LANES = 128
POOL_HALO = 16
ROUTER_PAD = LANES
MOE_TM = 512
GATHER_ROWS = 256
COMBINE_ROWS = 128


def _cparams(n_axes):
    return pltpu.CompilerParams(dimension_semantics=("arbitrary",) * n_axes,
                                vmem_limit_bytes=VMEM_LIMIT_BYTES)


def _rms(x, g):
    return (x * lax.rsqrt(jnp.mean(x * x, axis=-1, keepdims=True) + RMS_EPS)) * g


_NN = (((1,), (0,)), ((), ()))
_NT = (((1,), (1,)), ((), ()))


def _dot(a, b, dims=_NN):
    return lax.dot_general(a.astype(BF16), b.astype(BF16), dims, preferred_element_type=F32)


def _norm_mm_kernel(x_ref, g_ref, w_ref, o_ref, xn_ref, *, gate_from):
    j = pl.program_id(1)

    @pl.when(j == 0)
    def _():
        xn_ref[...] = _rms(x_ref[...], g_ref[...]).astype(BF16)

    acc = _dot(xn_ref[...], w_ref[...])
    if gate_from is None:
        o_ref[...] = acc
    else:
        @pl.when(j < gate_from)
        def _():
            o_ref[...] = acc

        @pl.when(j >= gate_from)
        def _():
            o_ref[...] = jax.nn.sigmoid(acc)


def norm_matmul(x, g, w, *, tm, tn, gate_from_col=None):
    m, d = x.shape
    n = w.shape[1]
    gate_from = None if gate_from_col is None else gate_from_col // tn
    return pl.pallas_call(
        functools.partial(_norm_mm_kernel, gate_from=gate_from),
        grid=(m // tm, n // tn),
        in_specs=[pl.BlockSpec((tm, d), lambda i, j: (i, 0)),
                  pl.BlockSpec((1, d), lambda i, j: (0, 0)),
                  pl.BlockSpec((d, tn), lambda i, j: (0, j))],
        out_specs=pl.BlockSpec((tm, tn), lambda i, j: (i, j)),
        out_shape=jax.ShapeDtypeStruct((m, n), F32),
        scratch_shapes=[pltpu.VMEM((tm, d), BF16)],
        compiler_params=_cparams(2),
        name="norm_matmul",
    )(x, g, w)


def _norm_swiglu_kernel(x_ref, g_ref, w1_ref, w3_ref, o_ref, xn_ref):
    @pl.when(pl.program_id(1) == 0)
    def _():
        xn_ref[...] = _rms(x_ref[...], g_ref[...]).astype(BF16)

    xn = xn_ref[...]
    a = _dot(xn, w1_ref[...])
    b = _dot(xn, w3_ref[...])
    o_ref[...] = (jax.nn.silu(a) * b).astype(BF16)


def norm_swiglu(x, g, w1, w3, *, tm, tn):
    m, d = x.shape
    n = w1.shape[1]
    return pl.pallas_call(
        functools.partial(_norm_swiglu_kernel),
        grid=(m // tm, n // tn),
        in_specs=[pl.BlockSpec((tm, d), lambda i, j: (i, 0)),
                  pl.BlockSpec((1, d), lambda i, j: (0, 0)),
                  pl.BlockSpec((d, tn), lambda i, j: (0, j)),
                  pl.BlockSpec((d, tn), lambda i, j: (0, j))],
        out_specs=pl.BlockSpec((tm, tn), lambda i, j: (i, j)),
        out_shape=jax.ShapeDtypeStruct((m, n), BF16),
        scratch_shapes=[pltpu.VMEM((tm, d), BF16)],
        compiler_params=_cparams(2),
        name="norm_swiglu",
    )(x, g, w1, w3)


def _mm_res_kernel(a_ref, w_ref, r_ref, o_ref):
    o_ref[...] = r_ref[...] + _dot(a_ref[...], w_ref[...])


def matmul_residual(a, w, res, *, tm, tn):
    m, k = a.shape
    n = w.shape[1]
    return pl.pallas_call(
        functools.partial(_mm_res_kernel),
        grid=(m // tm, n // tn),
        in_specs=[pl.BlockSpec((tm, k), lambda i, j: (i, 0)),
                  pl.BlockSpec((k, tn), lambda i, j: (0, j)),
                  pl.BlockSpec((tm, tn), lambda i, j: (i, j))],
        out_specs=pl.BlockSpec((tm, tn), lambda i, j: (i, j)),
        out_shape=jax.ShapeDtypeStruct((m, n), F32),
        compiler_params=_cparams(2),
        name="matmul_residual",
    )(a, w, res)


def _merge_kernel(ya_ref, yp_ref, ym_ref, wa_ref, wp_ref, wm_ref, g0_ref, g1_ref, g2_ref, o_ref):
    ua = _dot(ya_ref[...], wa_ref[...])
    up = _dot(yp_ref[...], wp_ref[...])
    um = _dot(ym_ref[...], wm_ref[...])
    o_ref[...] = (g0_ref[...] * ua + g1_ref[...] * up + g2_ref[...] * um).astype(BF16)


def merge_branches(ya, yp, ym, wa, wp, wm, z, *, tm, tn):
    m = ya.shape[0]
    nj = D_MODEL // tn
    g_blk = G_OFF // tn

    def gate_spec(b):
        return pl.BlockSpec((tm, tn), lambda i, j: (i, g_blk + b * nj + j))

    return pl.pallas_call(
        functools.partial(_merge_kernel),
        grid=(m // tm, nj),
        in_specs=[pl.BlockSpec((tm, ATTN_WIDTH), lambda i, j: (i, 0)),
                  pl.BlockSpec((tm, POOL_CH), lambda i, j: (i, 0)),
                  pl.BlockSpec((tm, MEM_WIDTH), lambda i, j: (i, 0)),
                  pl.BlockSpec((ATTN_WIDTH, tn), lambda i, j: (0, j)),
                  pl.BlockSpec((POOL_CH, tn), lambda i, j: (0, j)),
                  pl.BlockSpec((MEM_WIDTH, tn), lambda i, j: (0, j)),
                  gate_spec(0), gate_spec(1), gate_spec(2)],
        out_specs=pl.BlockSpec((tm, tn), lambda i, j: (i, j)),
        out_shape=jax.ShapeDtypeStruct((m, D_MODEL), BF16),
        compiler_params=_cparams(2),
        name="merge_branches",
    )(ya, yp, ym, wa, wp, wm, z, z, z)


def _swa_prompt_kernel(q_ref, kp_ref, kc_ref, vp_ref, vc_ref, bias_ref, sink_ref, o_ref):
    i = pl.program_id(1)
    q = q_ref[...].astype(BF16)
    k2 = jnp.concatenate([kp_ref[...], kc_ref[...]], axis=0).astype(BF16)
    v2 = jnp.concatenate([vp_ref[...], vc_ref[...]], axis=0).astype(BF16)
    row = lax.broadcasted_iota(jnp.int32, (WINDOW, 2 * WINDOW), 0)
    col = lax.broadcasted_iota(jnp.int32, (WINDOW, 2 * WINDOW), 1)
    dist = row + WINDOW - col
    valid = (dist >= 0) & (dist < WINDOW) & ((i > 0) | (col >= WINDOW))
    for h in range(N_HEADS):
        kv = h // Q_GROUP
        qh = q[:, h * HEAD_DIM:(h + 1) * HEAD_DIM]
        kh = k2[:, kv * HEAD_DIM:(kv + 1) * HEAD_DIM]
        vh = v2[:, kv * HEAD_DIM:(kv + 1) * HEAD_DIM]
        logits = lax.dot_general(qh, kh, (((1,), (1,)), ((), ())),
                                 preferred_element_type=F32) * (HEAD_DIM ** -0.5)
        logits = jnp.where(valid, logits + bias_ref[h], NEG_INF)
        s = sink_ref[h]
        m = jnp.maximum(jnp.max(logits, axis=-1, keepdims=True), s)
        p = jnp.exp(logits - m)
        denom = jnp.sum(p, axis=-1, keepdims=True) + jnp.exp(s - m)
        probs = (p * (1.0 / denom)).astype(BF16)
        o = jnp.dot(probs, vh, preferred_element_type=F32)
        o_ref[:, h * HEAD_DIM:(h + 1) * HEAD_DIM] = o.astype(BF16)


def swa_prompt(z, bias2, sinks, *, batch, seq):
    nb = seq // WINDOW
    kblk = K_OFF // KV_WIDTH
    vblk = V_OFF // KV_WIDTH

    def cur(c):
        return pl.BlockSpec((WINDOW, KV_WIDTH), lambda b, i: (b * nb + i, c))

    def prev(c):
        return pl.BlockSpec((WINDOW, KV_WIDTH), lambda b, i: (b * nb + jnp.maximum(i - 1, 0), c))

    return pl.pallas_call(
        _swa_prompt_kernel,
        grid=(batch, nb),
        in_specs=[pl.BlockSpec((WINDOW, ATTN_WIDTH), lambda b, i: (b * nb + i, 0)),
                  prev(kblk), cur(kblk), prev(vblk), cur(vblk),
                  pl.BlockSpec((N_HEADS, WINDOW, 2 * WINDOW), lambda b, i: (0, 0, 0)),
                  pl.BlockSpec(memory_space=pltpu.SMEM)],
        out_specs=pl.BlockSpec((WINDOW, ATTN_WIDTH), lambda b, i: (b * nb + i, 0)),
        out_shape=jax.ShapeDtypeStruct((batch * seq, ATTN_WIDTH), BF16),
        compiler_params=_cparams(2),
        name="swa_prompt",
    )(z, z, z, z, z, bias2, sinks)


def _swa_sample_kernel(q_ref, kn_ref, vn_ref, ck_ref, cv_ref, bias_ref, sink_ref,
                       o_ref, ok_ref, ov_ref, *, bb):
    row = lax.broadcasted_iota(jnp.int32, (WINDOW, KV_WIDTH), 0)
    hrow = lax.broadcasted_iota(jnp.int32, (N_HEADS, KV_WIDTH), 0)
    hcol = lax.broadcasted_iota(jnp.int32, (N_HEADS, KV_WIDTH), 1)
    own = (hcol // HEAD_DIM) == (hrow // Q_GROUP)
    s = sink_ref[...]
    for b in range(bb):
        newk = jnp.where(row == WINDOW - 1, kn_ref[b:b + 1, :], pltpu.roll(ck_ref[b], WINDOW - 1, 0))
        newv = jnp.where(row == WINDOW - 1, vn_ref[b:b + 1, :], pltpu.roll(cv_ref[b], WINDOW - 1, 0))
        ok_ref[b] = newk
        ov_ref[b] = newv
        qb = q_ref[b]
        qbd = jnp.where(own, jnp.concatenate([qb] * N_KV_HEADS, axis=1), 0.0)
        logits = _dot(qbd, newk, _NT) * (HEAD_DIM ** -0.5)
        logits = logits + bias_ref[...]
        m = jnp.maximum(jnp.max(logits, axis=-1, keepdims=True), s)
        p = jnp.exp(logits - m)
        denom = jnp.sum(p, axis=-1, keepdims=True) + jnp.exp(s - m)
        o = jnp.where(own, _dot(p * (1.0 / denom), newv), 0.0)
        acc = o[:, 0:HEAD_DIM]
        for c in range(1, N_KV_HEADS):
            acc = acc + o[:, c * HEAD_DIM:(c + 1) * HEAD_DIM]
        o_ref[b] = acc.astype(BF16)


def swa_sample(q3, zs, cache_k, cache_v, bias_s, sinks_col, *, bb):
    nbatch = q3.shape[0]
    kblk = K_OFF // KV_WIDTH
    vblk = V_OFF // KV_WIDTH
    cache_spec = pl.BlockSpec((bb, WINDOW, KV_WIDTH), lambda i: (i, 0, 0))
    return pl.pallas_call(
        functools.partial(_swa_sample_kernel, bb=bb),
        grid=(nbatch // bb,),
        in_specs=[pl.BlockSpec((bb, N_HEADS, HEAD_DIM), lambda i: (i, 0, 0)),
                  pl.BlockSpec((bb, KV_WIDTH), lambda i: (i, kblk)),
                  pl.BlockSpec((bb, KV_WIDTH), lambda i: (i, vblk)),
                  cache_spec, cache_spec,
                  pl.BlockSpec((N_HEADS, WINDOW), lambda i: (0, 0)),
                  pl.BlockSpec((N_HEADS, 1), lambda i: (0, 0))],
        out_specs=[pl.BlockSpec((bb, N_HEADS, HEAD_DIM), lambda i: (i, 0, 0)), cache_spec, cache_spec],
        out_shape=[jax.ShapeDtypeStruct((nbatch, N_HEADS, HEAD_DIM), BF16),
                   jax.ShapeDtypeStruct((nbatch, WINDOW, KV_WIDTH), F32),
                   jax.ShapeDtypeStruct((nbatch, WINDOW, KV_WIDTH), F32)],
        compiler_params=_cparams(1),
        name="swa_sample",
    )(q3, zs, zs, cache_k, cache_v, bias_s, sinks_col)


def _pool_mix(sums, p, cnts, maps_ref, scale_ref, o_ref):
    for g in range(N_POOL_GROUPS):
        c0 = g * POOL_GROUP_CH
        c1 = c0 + POOL_GROUP_CH
        diff = sums[g] / cnts[g] - p[:, c0:c1]
        mixed = _dot(diff, maps_ref[g])
        o_ref[:, c0:c1] = (mixed * scale_ref[:, c0:c1]).astype(BF16)


def _pool_prompt_kernel(p_ref, halo_ref, maps_ref, scale_ref, o_ref, *, tt):
    i = pl.program_id(1)
    p = p_ref[...]
    halo = jnp.where(i == 0, 0.0, halo_ref[...])
    ext = jnp.concatenate([halo, p], axis=0)
    s = ext
    sums = []
    shift = 1
    for g in range(N_POOL_GROUPS):
        s = s[:, (POOL_GROUP_CH if g > 0 else 0):]
        s = s + pltpu.roll(s, shift, 0)
        shift *= 2
        sums.append(s[POOL_HALO:, 0:POOL_GROUP_CH])
    pos = i * tt + lax.broadcasted_iota(jnp.int32, (tt, 1), 0)
    cnts = [jnp.minimum(pos + 1, w).astype(F32) for w in POOL_WINDOWS]
    _pool_mix(sums, p, cnts, maps_ref, scale_ref, o_ref)


def pool_prompt(z, maps, scale, *, batch, seq, tt):
    nt = seq // tt
    pblk = P_OFF // POOL_CH
    hb = tt // POOL_HALO
    return pl.pallas_call(
        functools.partial(_pool_prompt_kernel, tt=tt),
        grid=(batch, nt),
        in_specs=[pl.BlockSpec((tt, POOL_CH), lambda b, i: (b * nt + i, pblk)),
                  pl.BlockSpec((POOL_HALO, POOL_CH),
                               lambda b, i: (jnp.maximum((b * nt + i) * hb - 1, 0), pblk)),
                  pl.BlockSpec((N_POOL_GROUPS, POOL_GROUP_CH, POOL_GROUP_CH), lambda b, i: (0, 0, 0)),
                  pl.BlockSpec((1, POOL_CH), lambda b, i: (0, 0))],
        out_specs=pl.BlockSpec((tt, POOL_CH), lambda b, i: (b * nt + i, 0)),
        out_shape=jax.ShapeDtypeStruct((batch * seq, POOL_CH), BF16),
        compiler_params=_cparams(2),
        name="pool_prompt",
    )(z, z, maps, scale)


def _pool_sample_kernel(p_ref, st_ref, maps_ref, scale_ref, o_ref, ns_ref):
    p = p_ref[...]
    rows = [st_ref[:, r, :] for r in range(POOL_BUF)]
    sums = []
    for g, w in enumerate(POOL_WINDOWS):
        c0 = g * POOL_GROUP_CH
        c1 = c0 + POOL_GROUP_CH
        s = p[:, c0:c1]
        for r in range(POOL_BUF - (w - 1), POOL_BUF):
            s = s + rows[r][:, c0:c1]
        sums.append(s)
    cnts = [float(min(PAST_LEN + 1, w)) for w in POOL_WINDOWS]
    _pool_mix(sums, p, cnts, maps_ref, scale_ref, o_ref)
    for r in range(POOL_BUF - 1):
        ns_ref[:, r, :] = rows[r + 1]
    ns_ref[:, POOL_BUF - 1, :] = p


def pool_sample(zs, state, maps, scale):
    nbatch = zs.shape[0]
    pblk = P_OFF // POOL_CH
    st_spec = pl.BlockSpec((nbatch, POOL_BUF, POOL_CH), lambda i: (0, 0, 0))
    return pl.pallas_call(
        _pool_sample_kernel,
        grid=(1,),
        in_specs=[pl.BlockSpec((nbatch, POOL_CH), lambda i: (0, pblk)),
                  st_spec,
                  pl.BlockSpec((N_POOL_GROUPS, POOL_GROUP_CH, POOL_GROUP_CH), lambda i: (0, 0, 0)),
                  pl.BlockSpec((1, POOL_CH), lambda i: (0, 0))],
        out_specs=[pl.BlockSpec((nbatch, POOL_CH), lambda i: (0, 0)), st_spec],
        out_shape=[jax.ShapeDtypeStruct((nbatch, POOL_CH), BF16),
                   jax.ShapeDtypeStruct((nbatch, POOL_BUF, POOL_CH), F32)],
        compiler_params=_cparams(1),
        name="pool_sample",
    )(zs, state, maps, scale)


def _softmax_rows(logits):
    m = jnp.max(logits, axis=-1, keepdims=True)
    p = jnp.exp(logits - m)
    return p * (1.0 / jnp.sum(p, axis=-1, keepdims=True))


def _mem_prompt_kernel(q_ref, k_ref, v_ref, o_ref):
    q = q_ref[...].astype(BF16)
    k = k_ref[...].astype(BF16)
    v = v_ref[...].astype(BF16)
    for h in range(MEM_HEADS):
        c0 = h * MEM_HEAD_DIM
        c1 = c0 + MEM_HEAD_DIM
        logits = lax.dot_general(q[:, c0:c1], k[:, c0:c1], (((1,), (1,)), ((), ())),
                                 preferred_element_type=F32) * (MEM_HEAD_DIM ** -0.5)
        probs = _softmax_rows(logits).astype(BF16)
        o_ref[:, c0:c1] = jnp.dot(probs, v[:, c0:c1], preferred_element_type=F32).astype(BF16)


def mem_prompt(z, kv, *, batch, seq, tq):
    nt = seq // tq
    mblk = M_OFF // MEM_WIDTH
    return pl.pallas_call(
        _mem_prompt_kernel,
        grid=(batch, nt),
        in_specs=[pl.BlockSpec((tq, MEM_WIDTH), lambda b, i: (b * nt + i, mblk)),
                  pl.BlockSpec((N_MEM, MEM_WIDTH), lambda b, i: (b, 0)),
                  pl.BlockSpec((N_MEM, MEM_WIDTH), lambda b, i: (b, 1))],
        out_specs=pl.BlockSpec((tq, MEM_WIDTH), lambda b, i: (b * nt + i, 0)),
        out_shape=jax.ShapeDtypeStruct((batch * seq, MEM_WIDTH), BF16),
        compiler_params=_cparams(2),
        name="mem_prompt",
    )(z, kv, kv)


MEM_Q_ROWS = 16


def _mem_sample_kernel(q_ref, k_ref, v_ref, o_ref, *, bb):
    hrow = lax.broadcasted_iota(jnp.int32, (MEM_Q_ROWS, MEM_WIDTH), 0)
    hcol = lax.broadcasted_iota(jnp.int32, (MEM_Q_ROWS, MEM_WIDTH), 1)
    own = (hcol // MEM_HEAD_DIM) == hrow
    for b in range(bb):
        qb = q_ref[b]
        qbd = jnp.where(own, jnp.concatenate([qb] * MEM_HEADS, axis=1), 0.0)
        logits = _dot(qbd, k_ref[b], _NT) * (MEM_HEAD_DIM ** -0.5)
        probs = _softmax_rows(logits)
        o = jnp.where(own, _dot(probs, v_ref[b]), 0.0)
        acc = o[:, 0:MEM_HEAD_DIM]
        for c in range(1, MEM_HEADS):
            acc = acc + o[:, c * MEM_HEAD_DIM:(c + 1) * MEM_HEAD_DIM]
        o_ref[b] = acc.astype(BF16)


def mem_sample(q3, mem_k, mem_v, *, bb):
    nbatch = q3.shape[0]
    kv_spec = pl.BlockSpec((bb, N_MEM, MEM_WIDTH), lambda i: (i, 0, 0))
    q_spec = pl.BlockSpec((bb, MEM_Q_ROWS, MEM_HEAD_DIM), lambda i: (i, 0, 0))
    return pl.pallas_call(
        functools.partial(_mem_sample_kernel, bb=bb),
        grid=(nbatch // bb,),
        in_specs=[q_spec, kv_spec, kv_spec],
        out_specs=q_spec,
        out_shape=jax.ShapeDtypeStruct((nbatch, MEM_Q_ROWS, MEM_HEAD_DIM), BF16),
        compiler_params=_cparams(1),
        name="mem_sample",
    )(q3, mem_k, mem_v)


def _norm_router_kernel(x_ref, g_ref, r_ref, xn_ref, idx_ref, wgt_ref):
    xn = _rms(x_ref[...], g_ref[...])
    xn_ref[...] = xn
    logits = _dot(xn, r_ref[...])
    lane = lax.broadcasted_iota(jnp.int32, logits.shape, 1)
    logits = jnp.where(lane < N_EXPERTS, logits, -jnp.inf)
    v1 = jnp.max(logits, axis=-1, keepdims=True)
    i1 = jnp.min(jnp.where(logits == v1, lane, ROUTER_PAD), axis=-1, keepdims=True)
    rest = jnp.where(lane == i1, -jnp.inf, logits)
    v2 = jnp.max(rest, axis=-1, keepdims=True)
    i2 = jnp.min(jnp.where(rest == v2, lane, ROUTER_PAD), axis=-1, keepdims=True)
    e2 = jnp.exp(v2 - v1)
    den = 1.0 + e2
    idx_ref[...] = jnp.where(lane == 0, i1, jnp.where(lane == 1, i2, 0))
    wgt_ref[...] = jnp.where(lane == 0, 1.0 / den, jnp.where(lane == 1, e2 / den, 0.0))


def norm_router(x, g, router_pad, *, tm):
    m, d = x.shape
    return pl.pallas_call(
        _norm_router_kernel,
        grid=(m // tm,),
        in_specs=[pl.BlockSpec((tm, d), lambda i: (i, 0)),
                  pl.BlockSpec((1, d), lambda i: (0, 0)),
                  pl.BlockSpec((d, ROUTER_PAD), lambda i: (0, 0))],
        out_specs=[pl.BlockSpec((tm, d), lambda i: (i, 0)),
                   pl.BlockSpec((tm, ROUTER_PAD), lambda i: (i, 0)),
                   pl.BlockSpec((tm, ROUTER_PAD), lambda i: (i, 0))],
        out_shape=[jax.ShapeDtypeStruct((m, d), F32),
                   jax.ShapeDtypeStruct((m, ROUTER_PAD), jnp.int32),
                   jax.ShapeDtypeStruct((m, ROUTER_PAD), F32)],
        compiler_params=_cparams(1),
        name="norm_router",
    )(x, g, router_pad)


def _row_copy(src_hbm, row, buf, slot, sem):
    return pltpu.make_async_copy(src_hbm.at[pl.ds(row, 1)], buf.at[pl.ds(slot, 1)], sem)


def _gather_kernel(idx_ref, src_hbm, o_ref, buf, sem, *, rows):
    def start(r, c):
        _row_copy(src_hbm, idx_ref[0, 0, r], buf, r, sem).start()
        return c

    def wait(r, c):
        _row_copy(src_hbm, idx_ref[0, 0, r], buf, r, sem).wait()
        return c

    lax.fori_loop(0, rows, start, 0)
    lax.fori_loop(0, rows, wait, 0)
    o_ref[...] = buf[...].astype(BF16)


def gather_rows(src, idx, *, rows):
    n = idx.shape[0]
    d = src.shape[1]
    nchunk = n // rows
    return pl.pallas_call(
        functools.partial(_gather_kernel, rows=rows),
        grid=(nchunk,),
        in_specs=[pl.BlockSpec((1, 1, rows), lambda i: (i, 0, 0), memory_space=pltpu.SMEM),
                  pl.BlockSpec(memory_space=pl.ANY)],
        out_specs=pl.BlockSpec((rows, d), lambda i: (i, 0)),
        out_shape=jax.ShapeDtypeStruct((n, d), BF16),
        scratch_shapes=[pltpu.VMEM((rows, d), F32), pltpu.SemaphoreType.DMA(())],
        compiler_params=_cparams(1),
        name="gather_rows",
    )(idx.reshape(nchunk, 1, rows), src)


def _moe_up_kernel(te_ref, tv_ref, x_ref, w1_ref, w3_ref, o_ref):
    t = pl.program_id(0)

    @pl.when(tv_ref[t] > 0)
    def _():
        x = x_ref[...]
        a = jnp.dot(x, w1_ref[0], preferred_element_type=F32)
        b = jnp.dot(x, w3_ref[0], preferred_element_type=F32)
        o_ref[...] = (jax.nn.silu(a) * b).astype(BF16)

    @pl.when(tv_ref[t] == 0)
    def _():
        o_ref[...] = jnp.zeros_like(o_ref)


def moe_up(xs, w1, w3, tile_expert, tile_valid, *, tm, tn):
    n, d = xs.shape
    f = w1.shape[2]
    nj = f // tn

    def w_map(t, j, te, tv):
        return (te[t], 0, jnp.where(tv[t] > 0, j, nj - 1))

    return pl.pallas_call(
        _moe_up_kernel,
        grid_spec=pltpu.PrefetchScalarGridSpec(
            num_scalar_prefetch=2,
            grid=(n // tm, nj),
            in_specs=[pl.BlockSpec((tm, d), lambda t, j, te, tv: (t, 0)),
                      pl.BlockSpec((1, d, tn), w_map),
                      pl.BlockSpec((1, d, tn), w_map)],
            out_specs=pl.BlockSpec((tm, tn), lambda t, j, te, tv: (t, j))),
        out_shape=jax.ShapeDtypeStruct((n, f), BF16),
        compiler_params=_cparams(2),
        name="moe_up",
    )(tile_expert, tile_valid, xs, w1, w3)


def _moe_down_kernel(te_ref, tv_ref, h_ref, w2_ref, gw_ref, o_ref):
    t = pl.program_id(0)

    @pl.when(tv_ref[t] > 0)
    def _():
        o_ref[...] = gw_ref[...] * jnp.dot(h_ref[...], w2_ref[0], preferred_element_type=F32)

    @pl.when(tv_ref[t] == 0)
    def _():
        o_ref[...] = jnp.zeros_like(o_ref)


def moe_down(h, w2, gw, tile_expert, tile_valid, *, tm, tn):
    n, f = h.shape
    d = w2.shape[2]
    nj = d // tn

    def w_map(t, j, te, tv):
        return (te[t], 0, jnp.where(tv[t] > 0, j, nj - 1))

    return pl.pallas_call(
        _moe_down_kernel,
        grid_spec=pltpu.PrefetchScalarGridSpec(
            num_scalar_prefetch=2,
            grid=(n // tm, nj),
            in_specs=[pl.BlockSpec((tm, f), lambda t, j, te, tv: (t, 0)),
                      pl.BlockSpec((1, f, tn), w_map),
                      pl.BlockSpec((tm, 1), lambda t, j, te, tv: (t, 0))],
            out_specs=pl.BlockSpec((tm, tn), lambda t, j, te, tv: (t, j))),
        out_shape=jax.ShapeDtypeStruct((n, d), F32),
        compiler_params=_cparams(2),
        name="moe_down",
    )(tile_expert, tile_valid, h, w2, gw)


def _combine_kernel(pos_ref, x_ref, g_ref, ys_hbm, yp_ref, ysmp_ref, buf, sem, *, rows, n_prompt_chunks):
    i = pl.program_id(0)

    def start(r, c):
        _row_copy(ys_hbm, pos_ref[0, 0, r], buf, r, sem).start()
        return c

    def wait(r, c):
        _row_copy(ys_hbm, pos_ref[0, 0, r], buf, r, sem).wait()
        return c

    lax.fori_loop(0, 2 * rows, start, 0)
    lax.fori_loop(0, 2 * rows, wait, 0)
    y = x_ref[...] + (buf[0:rows, :] + buf[rows:2 * rows, :])
    y = _rms(y, g_ref[...])

    @pl.when(i < n_prompt_chunks)
    def _():
        yp_ref[...] = y

    @pl.when(i >= n_prompt_chunks)
    def _():
        ysmp_ref[...] = y


def moe_combine(x_all, g_final, ys, pos, *, rows, m_prompt):
    m, d = x_all.shape
    nchunk = m // rows
    npc = m_prompt // rows
    return pl.pallas_call(
        functools.partial(_combine_kernel, rows=rows, n_prompt_chunks=npc),
        grid=(nchunk,),
        in_specs=[pl.BlockSpec((1, 1, 2 * rows), lambda i: (i, 0, 0), memory_space=pltpu.SMEM),
                  pl.BlockSpec((rows, d), lambda i: (i, 0)),
                  pl.BlockSpec((1, d), lambda i: (0, 0)),
                  pl.BlockSpec(memory_space=pl.ANY)],
        out_specs=[pl.BlockSpec((rows, d), lambda i: (jnp.minimum(i, npc - 1), 0)),
                   pl.BlockSpec((rows, d), lambda i: (jnp.maximum(i - npc, 0), 0))],
        out_shape=[jax.ShapeDtypeStruct((m_prompt, d), F32),
                   jax.ShapeDtypeStruct((m - m_prompt, d), F32)],
        scratch_shapes=[pltpu.VMEM((2 * rows, d), F32), pltpu.SemaphoreType.DMA(())],
        compiler_params=_cparams(1),
        name="moe_combine",
    )(pos, x_all, g_final, ys)


def _routing_tables(idx2, wgt2, *, tm, n_rows):
    m = idx2.shape[0]
    e_flat = idx2.reshape(-1)
    onehot = (e_flat[:, None] == jnp.arange(N_EXPERTS, dtype=jnp.int32)[None, :]).astype(jnp.int32)
    csum = jnp.cumsum(onehot, axis=0)
    counts = csum[-1]
    rank = jnp.sum(csum * onehot, axis=1) - 1
    padded = ((counts + tm - 1) // tm) * tm
    ends = jnp.cumsum(padded)
    starts = ends - padded
    dest = starts[e_flat] + rank
    token = jnp.arange(2 * m, dtype=jnp.int32) // 2
    src = jnp.zeros((n_rows,), jnp.int32).at[dest].set(token)
    gw = jnp.zeros((n_rows,), F32).at[dest].set(wgt2.reshape(-1))
    tile_start = jnp.arange(n_rows // tm, dtype=jnp.int32) * tm
    tile_valid = (tile_start < ends[-1]).astype(jnp.int32)
    tile_expert = jnp.minimum(jnp.searchsorted(ends, tile_start, side="right"), N_EXPERTS - 1)
    last_expert = tile_expert[jnp.maximum(ends[-1] // tm - 1, 0)]
    tile_expert = jnp.where(tile_valid > 0, tile_expert, last_expert).astype(jnp.int32)
    return src, gw, dest.reshape(m, 2), tile_expert, tile_valid


def _rel_bias_by_distance(rel_bias_table):
    d = jnp.arange(WINDOW, dtype=jnp.int32)
    max_exact = REL_BUCKETS // 2
    d_f = jnp.maximum(d, 1).astype(F32)
    large = max_exact + (jnp.log(d_f / max_exact) / math.log(REL_MAX_DIST / max_exact)
                         * (REL_BUCKETS - max_exact)).astype(jnp.int32)
    large = jnp.minimum(large, REL_BUCKETS - 1)
    bucket = jnp.where(d < max_exact, d, large)
    return rel_bias_table[bucket].T


def _mixer_prompt(x, mem2d, lw, bias2, *, batch, seq):
    kv = norm_matmul(mem2d, lw["g_mem"], lw["w_mem_kv"], tm=1024, tn=512)
    z = norm_matmul(x, lw["g_attn"], lw["w_in"], tm=1024, tn=512, gate_from_col=G_OFF)
    ya = swa_prompt(z, bias2, lw["sinks"], batch=batch, seq=seq)
    yp = pool_prompt(z, lw["pool_maps"], lw["pool_scale"], batch=batch, seq=seq, tt=512)
    ym = mem_prompt(z, kv, batch=batch, seq=seq, tq=512)
    merged = merge_branches(ya, yp, ym, lw["w_up_attn"], lw["w_up_pool"], lw["w_up_mem"], z, tm=1024, tn=512)
    x = matmul_residual(merged, lw["w_o"], x, tm=1024, tn=512)
    z3 = z.reshape(batch, seq, IN_COLS)
    new_k = z3[:, seq - WINDOW:, K_OFF:V_OFF].reshape(batch, WINDOW, N_KV_HEADS, HEAD_DIM)
    new_v = z3[:, seq - WINDOW:, V_OFF:P_OFF].reshape(batch, WINDOW, N_KV_HEADS, HEAD_DIM)
    new_pool = z3[:, seq - POOL_BUF:, P_OFF:M_OFF]
    kv3 = kv.reshape(batch, N_MEM, 2 * MEM_WIDTH)
    mk = kv3[..., :MEM_WIDTH].reshape(batch, N_MEM, MEM_HEADS, MEM_HEAD_DIM)
    mv = kv3[..., MEM_WIDTH:].reshape(batch, N_MEM, MEM_HEADS, MEM_HEAD_DIM)
    return x, new_k, new_v, new_pool, mk, mv


def _mixer_sample(x, cache_k, cache_v, state, mem_k, mem_v, lw, bias_s):
    nb = x.shape[0]
    z = norm_matmul(x, lw["g_attn"], lw["w_in"], tm=nb, tn=512, gate_from_col=G_OFF)
    q3 = z[:, Q_OFF:K_OFF].reshape(nb, N_HEADS, HEAD_DIM)
    ya3, new_k, new_v = swa_sample(q3, z, cache_k.reshape(nb, WINDOW, KV_WIDTH),
                                   cache_v.reshape(nb, WINDOW, KV_WIDTH), bias_s, lw["sinks"][:, None], bb=8)
    yp, new_pool = pool_sample(z, state, lw["pool_maps"], lw["pool_scale"])
    qm3 = z[:, M_OFF:G_OFF].reshape(nb, MEM_HEADS, MEM_HEAD_DIM)
    qm3 = jnp.pad(qm3, ((0, 0), (0, MEM_Q_ROWS - MEM_HEADS), (0, 0)))
    ym3 = mem_sample(qm3, mem_k.reshape(nb, N_MEM, MEM_WIDTH), mem_v.reshape(nb, N_MEM, MEM_WIDTH), bb=8)
    ya = ya3.reshape(nb, ATTN_WIDTH)
    ym = ym3[:, :MEM_HEADS].reshape(nb, MEM_WIDTH)
    merged = merge_branches(ya, yp, ym, lw["w_up_attn"], lw["w_up_pool"], lw["w_up_mem"], z,
                            tm=nb, tn=512)
    x = matmul_residual(merged, lw["w_o"], x, tm=nb, tn=512)
    return (x, new_k.reshape(nb, WINDOW, N_KV_HEADS, HEAD_DIM),
            new_v.reshape(nb, WINDOW, N_KV_HEADS, HEAD_DIM), new_pool)


def _dense_ffn(x, g, w1, w3, w2, *, tm):
    h = norm_swiglu(x, g, w1, w3, tm=tm, tn=512)
    return matmul_residual(h, w2, x, tm=tm, tn=512)


def _moe_ffn_final(xp, xs, g, router, w1, w3, w2, g_final):
    m_prompt = xp.shape[0]
    x_all = jnp.concatenate([xp, xs], axis=0)
    m = x_all.shape[0]
    router_pad = jnp.pad(router, ((0, 0), (0, ROUTER_PAD - N_EXPERTS)))
    xn, idx_pad, wgt_pad = norm_router(x_all, g, router_pad, tm=COMBINE_ROWS * 3)
    assert MOE_TM % GATHER_ROWS == 0
    n_rows = pl.cdiv(TOP_K * m, MOE_TM) * MOE_TM + N_EXPERTS * MOE_TM
    src, gw, dest, tile_expert, tile_valid = _routing_tables(
        idx_pad[:, :TOP_K], wgt_pad[:, :TOP_K], tm=MOE_TM, n_rows=n_rows)
    xsort = gather_rows(xn, src, rows=GATHER_ROWS)
    h = moe_up(xsort, w1, w3, tile_expert, tile_valid, tm=MOE_TM, tn=512)
    ys = moe_down(h, w2, gw[:, None], tile_expert, tile_valid, tm=MOE_TM, tn=512)
    nchunk = m // COMBINE_ROWS
    pos = dest.reshape(nchunk, COMBINE_ROWS, TOP_K).transpose(0, 2, 1).reshape(nchunk, 1, TOP_K * COMBINE_ROWS)
    return moe_combine(x_all, g_final, ys, pos, rows=COMBINE_ROWS, m_prompt=m_prompt)


def kernel(x_prompt, x_sample, mem_prompt, cache_swa_k, cache_swa_v, state_pool, cache_mem_k, cache_mem_v, rel_bias_table, g_attn, w_in, attn_sinks, pool_maps, pool_scale, g_mem, w_mem_kv, w_up_attn, w_up_pool, w_up_mem, w_o, g_ffn, ffn_w1, ffn_w3, ffn_w2, moe_router, moe_w1, moe_w3, moe_w2, g_final):
    batch, seq, d = x_prompt.shape
    nb_s = x_sample.shape[0]
    depth = w_in.shape[0]
    assert depth == 2 and x_sample.shape[1] == 1

    bias_hw = _rel_bias_by_distance(rel_bias_table).astype(F32)
    row = jnp.arange(WINDOW)[:, None]
    col = jnp.arange(2 * WINDOW)[None, :]
    bias2 = bias_hw[:, jnp.clip(row + WINDOW - col, 0, WINDOW - 1)]
    bias_s = bias_hw[:, ::-1]

    xp = x_prompt.reshape(batch * seq, d)
    xs = x_sample.reshape(nb_s, d)
    mem2d = mem_prompt.reshape(batch * N_MEM, d)
    p_k, p_v, p_pool, p_mk, p_mv, s_k, s_v, s_pool = [], [], [], [], [], [], [], []
    for l in range(depth):
        lw = dict(g_attn=g_attn[l][None, :], w_in=w_in[l].astype(BF16), sinks=attn_sinks[l],
                  pool_maps=pool_maps[l].astype(BF16), pool_scale=pool_scale[l][None, :],
                  g_mem=g_mem[l][None, :], w_mem_kv=w_mem_kv[l].astype(BF16),
                  w_up_attn=w_up_attn[l].astype(BF16), w_up_pool=w_up_pool[l].astype(BF16),
                  w_up_mem=w_up_mem[l].astype(BF16), w_o=w_o[l].astype(BF16))
        xp, nk, nv, npool, mk, mv = _mixer_prompt(xp, mem2d, lw, bias2, batch=batch, seq=seq)
        p_k.append(nk); p_v.append(nv); p_pool.append(npool); p_mk.append(mk); p_mv.append(mv)
        xs, nk, nv, npool = _mixer_sample(xs, cache_swa_k[l], cache_swa_v[l], state_pool[l],
                                          cache_mem_k[l], cache_mem_v[l], lw, bias_s)
        s_k.append(nk); s_v.append(nv); s_pool.append(npool)
        gl = g_ffn[l][None, :]
        i = l // 2
        if l % 2 == 0:
            w1, w3, w2 = ffn_w1[i].astype(BF16), ffn_w3[i].astype(BF16), ffn_w2[i].astype(BF16)
            xp = _dense_ffn(xp, gl, w1, w3, w2, tm=1024)
            xs = _dense_ffn(xs, gl, w1, w3, w2, tm=nb_s)
        else:
            yp, ys = _moe_ffn_final(xp, xs, gl, moe_router[i], moe_w1[i].astype(BF16),
                                    moe_w3[i].astype(BF16), moe_w2[i].astype(BF16), g_final[None, :])
    y_prompt = yp.reshape(batch, seq, d)
    y_sample = ys.reshape(nb_s, 1, d)
    return (y_prompt, y_sample,
            jnp.stack(p_k), jnp.stack(p_v), jnp.stack(p_pool), jnp.stack(p_mk), jnp.stack(p_mv),
            jnp.stack(s_k), jnp.stack(s_v), jnp.stack(s_pool))
```

```python
import functools
import math

import jax
import jax.numpy as jnp
from jax import lax
from jax.experimental import pallas as pl
from jax.experimental.pallas import tpu as pltpu

F32 = jnp.float32
BF16 = jnp.bfloat16

D_MODEL = 2048
N_HEADS = 16
N_KV_HEADS = 4
HEAD_DIM = 64
Q_GROUP = N_HEADS // N_KV_HEADS
WINDOW = 128
ATTN_WIDTH = N_HEADS * HEAD_DIM
KV_WIDTH = N_KV_HEADS * HEAD_DIM
REL_BUCKETS = 32
REL_MAX_DIST = 128
POOL_WINDOWS = (2, 4, 8, 16)
N_POOL_GROUPS = 4
POOL_CH = D_MODEL // 4
POOL_GROUP_CH = POOL_CH // N_POOL_GROUPS
POOL_BUF = max(POOL_WINDOWS) - 1
N_MEM = 256
MEM_HEADS = 4
MEM_HEAD_DIM = 128
MEM_WIDTH = MEM_HEADS * MEM_HEAD_DIM
N_BRANCHES = 3
Q_OFF = 0
K_OFF = Q_OFF + ATTN_WIDTH
V_OFF = K_OFF + KV_WIDTH
P_OFF = V_OFF + KV_WIDTH
M_OFF = P_OFF + POOL_CH
G_OFF = M_OFF + MEM_WIDTH
IN_COLS = G_OFF + N_BRANCHES * D_MODEL
N_EXPERTS = 8
TOP_K = 2
RMS_EPS = 1e-5
NEG_INF = -1e30
PAST_LEN = 8192

VMEM_LIMIT_BYTES = 56 * 1024 * 1024
LANES = 128
POOL_HALO = 16
ROUTER_PAD = LANES
ROW_SUB = D_MODEL // LANES
MOE_TM = 512
DISPATCH_ROWS = 256
COMBINE_ROWS = 128


def _cparams(n_axes):
    return pltpu.CompilerParams(dimension_semantics=("arbitrary",) * n_axes,
                                vmem_limit_bytes=VMEM_LIMIT_BYTES)


def _rms(x, g):
    return (x * lax.rsqrt(jnp.mean(x * x, axis=-1, keepdims=True) + RMS_EPS)) * g


_NN = (((1,), (0,)), ((), ()))
_NT = (((1,), (1,)), ((), ()))


def _dot(a, b, dims=_NN):
    return lax.dot_general(a.astype(BF16), b.astype(BF16), dims, preferred_element_type=F32)


def _layer_cols(layer, rows, tn):
    return pl.BlockSpec((None, rows, tn), lambda i, j: (layer, 0, j))


def _norm_mm_kernel(x_ref, g_ref, w_ref, o_ref, xn_ref, *, gate_from):
    j = pl.program_id(1)

    @pl.when(j == 0)
    def _():
        xn_ref[...] = _rms(x_ref[...], g_ref[...]).astype(BF16)

    acc = _dot(xn_ref[...], w_ref[...])
    if gate_from is None:
        o_ref[...] = acc
    else:
        @pl.when(j < gate_from)
        def _():
            o_ref[...] = acc

        @pl.when(j >= gate_from)
        def _():
            o_ref[...] = jax.nn.sigmoid(acc)


def norm_matmul(x, g, w, layer, *, tm, tn, gate_from_col=None):
    m, d = x.shape
    n = w.shape[2]
    gate_from = None if gate_from_col is None else gate_from_col // tn
    return pl.pallas_call(
        functools.partial(_norm_mm_kernel, gate_from=gate_from),
        grid=(m // tm, n // tn),
        in_specs=[pl.BlockSpec((tm, d), lambda i, j: (i, 0)),
                  pl.BlockSpec((1, d), lambda i, j: (0, 0)),
                  _layer_cols(layer, d, tn)],
        out_specs=pl.BlockSpec((tm, tn), lambda i, j: (i, j)),
        out_shape=jax.ShapeDtypeStruct((m, n), F32),
        scratch_shapes=[pltpu.VMEM((tm, d), BF16)],
        compiler_params=_cparams(2),
        name="norm_matmul",
    )(x, g, w)


def _norm_swiglu_kernel(x_ref, g_ref, w1_ref, w3_ref, o_ref, xn_ref):
    @pl.when(pl.program_id(1) == 0)
    def _():
        xn_ref[...] = _rms(x_ref[...], g_ref[...]).astype(BF16)

    xn = xn_ref[...]
    a = _dot(xn, w1_ref[...])
    b = _dot(xn, w3_ref[...])
    o_ref[...] = (jax.nn.silu(a) * b).astype(BF16)


def norm_swiglu(x, g, w1, w3, layer, *, tm, tn):
    m, d = x.shape
    n = w1.shape[2]
    return pl.pallas_call(
        _norm_swiglu_kernel,
        grid=(m // tm, n // tn),
        in_specs=[pl.BlockSpec((tm, d), lambda i, j: (i, 0)),
                  pl.BlockSpec((1, d), lambda i, j: (0, 0)),
                  _layer_cols(layer, d, tn), _layer_cols(layer, d, tn)],
        out_specs=pl.BlockSpec((tm, tn), lambda i, j: (i, j)),
        out_shape=jax.ShapeDtypeStruct((m, n), BF16),
        scratch_shapes=[pltpu.VMEM((tm, d), BF16)],
        compiler_params=_cparams(2),
        name="norm_swiglu",
    )(x, g, w1, w3)


def _mm_res_kernel(a_ref, w_ref, r_ref, o_ref):
    o_ref[...] = r_ref[...] + _dot(a_ref[...], w_ref[...])


def matmul_residual(a, w, layer, res, *, tm, tn):
    m, k = a.shape
    n = w.shape[2]
    return pl.pallas_call(
        _mm_res_kernel,
        grid=(m // tm, n // tn),
        in_specs=[pl.BlockSpec((tm, k), lambda i, j: (i, 0)),
                  _layer_cols(layer, k, tn),
                  pl.BlockSpec((tm, tn), lambda i, j: (i, j))],
        out_specs=pl.BlockSpec((tm, tn), lambda i, j: (i, j)),
        out_shape=jax.ShapeDtypeStruct((m, n), F32),
        compiler_params=_cparams(2),
        name="matmul_residual",
    )(a, w, res)


def _merge_kernel(ya_ref, yp_ref, ym_ref, wa_ref, wp_ref, wm_ref, g0_ref, g1_ref, g2_ref, o_ref):
    ua = _dot(ya_ref[...], wa_ref[...])
    up = _dot(yp_ref[...], wp_ref[...])
    um = _dot(ym_ref[...], wm_ref[...])
    o_ref[...] = (g0_ref[...] * ua + g1_ref[...] * up + g2_ref[...] * um).astype(BF16)


def merge_branches(ya, yp, ym, wa, wp, wm, layer, z, *, tm, tn):
    m = ya.shape[0]
    nj = D_MODEL // tn
    g_blk = G_OFF // tn

    def gate_spec(b):
        return pl.BlockSpec((tm, tn), lambda i, j: (i, g_blk + b * nj + j))

    return pl.pallas_call(
        _merge_kernel,
        grid=(m // tm, nj),
        in_specs=[pl.BlockSpec((tm, ATTN_WIDTH), lambda i, j: (i, 0)),
                  pl.BlockSpec((tm, POOL_CH), lambda i, j: (i, 0)),
                  pl.BlockSpec((tm, MEM_WIDTH), lambda i, j: (i, 0)),
                  _layer_cols(layer, ATTN_WIDTH, tn), _layer_cols(layer, POOL_CH, tn),
                  _layer_cols(layer, MEM_WIDTH, tn),
                  gate_spec(0), gate_spec(1), gate_spec(2)],
        out_specs=pl.BlockSpec((tm, tn), lambda i, j: (i, j)),
        out_shape=jax.ShapeDtypeStruct((m, D_MODEL), BF16),
        compiler_params=_cparams(2),
        name="merge_branches",
    )(ya, yp, ym, wa, wp, wm, z, z, z)


def _swa_prompt_kernel(q_ref, kp_ref, kc_ref, vp_ref, vc_ref, base_ref, sink_ref, o_ref, bias_ref):
    i = pl.program_id(1)

    @pl.when((pl.program_id(0) == 0) & (i == 0))
    def _():
        for h in range(N_HEADS):
            row0 = jnp.broadcast_to(base_ref[h:h + 1, :], (WINDOW, 2 * WINDOW))
            bias_ref[h] = pltpu.roll(row0, 0, 1, stride=1, stride_axis=0)

    q = q_ref[...].astype(BF16)
    k2 = jnp.concatenate([kp_ref[...], kc_ref[...]], axis=0).astype(BF16)
    v2 = jnp.concatenate([vp_ref[...], vc_ref[...]], axis=0).astype(BF16)
    row = lax.broadcasted_iota(jnp.int32, (WINDOW, 2 * WINDOW), 0)
    col = lax.broadcasted_iota(jnp.int32, (WINDOW, 2 * WINDOW), 1)
    dist = row + WINDOW - col
    valid = (dist >= 0) & (dist < WINDOW) & ((i > 0) | (col >= WINDOW))
    for h in range(N_HEADS):
        kv = h // Q_GROUP
        qh = q[:, h * HEAD_DIM:(h + 1) * HEAD_DIM]
        kh = k2[:, kv * HEAD_DIM:(kv + 1) * HEAD_DIM]
        vh = v2[:, kv * HEAD_DIM:(kv + 1) * HEAD_DIM]
        logits = _dot(qh, kh, _NT) * (HEAD_DIM ** -0.5)
        logits = jnp.where(valid, logits + bias_ref[h], NEG_INF)
        s = sink_ref[h]
        m = jnp.maximum(jnp.max(logits, axis=-1, keepdims=True), s)
        p = jnp.exp(logits - m)
        denom = jnp.sum(p, axis=-1, keepdims=True) + jnp.exp(s - m)
        o = _dot(p * (1.0 / denom), vh)
        o_ref[:, h * HEAD_DIM:(h + 1) * HEAD_DIM] = o.astype(BF16)


def swa_prompt(z, bias_base, sinks, *, batch, seq):
    nb = seq // WINDOW
    kblk = K_OFF // KV_WIDTH
    vblk = V_OFF // KV_WIDTH

    def cur(c):
        return pl.BlockSpec((WINDOW, KV_WIDTH), lambda b, i: (b * nb + i, c))

    def prev(c):
        return pl.BlockSpec((WINDOW, KV_WIDTH), lambda b, i: (b * nb + jnp.maximum(i - 1, 0), c))

    return pl.pallas_call(
        _swa_prompt_kernel,
        grid=(batch, nb),
        in_specs=[pl.BlockSpec((WINDOW, ATTN_WIDTH), lambda b, i: (b * nb + i, 0)),
                  prev(kblk), cur(kblk), prev(vblk), cur(vblk),
                  pl.BlockSpec((N_HEADS, 2 * WINDOW), lambda b, i: (0, 0)),
                  pl.BlockSpec(memory_space=pltpu.SMEM)],
        out_specs=pl.BlockSpec((WINDOW, ATTN_WIDTH), lambda b, i: (b * nb + i, 0)),
        out_shape=jax.ShapeDtypeStruct((batch * seq, ATTN_WIDTH), BF16),
        scratch_shapes=[pltpu.VMEM((N_HEADS, WINDOW, 2 * WINDOW), F32)],
        compiler_params=_cparams(2),
        name="swa_prompt",
    )(z, z, z, z, z, bias_base, sinks)


def _swa_sample_kernel(q_ref, kn_ref, vn_ref, ck_ref, cv_ref, bias_ref, sink_ref,
                       o_ref, ok_ref, ov_ref, *, bb):
    row = lax.broadcasted_iota(jnp.int32, (WINDOW, KV_WIDTH), 0)
    hrow = lax.broadcasted_iota(jnp.int32, (N_HEADS, KV_WIDTH), 0)
    hcol = lax.broadcasted_iota(jnp.int32, (N_HEADS, KV_WIDTH), 1)
    own = (hcol // HEAD_DIM) == (hrow // Q_GROUP)
    s = sink_ref[...]
    for b in range(bb):
        newk = jnp.where(row == WINDOW - 1, kn_ref[b:b + 1, :], pltpu.roll(ck_ref[b], WINDOW - 1, 0))
        newv = jnp.where(row == WINDOW - 1, vn_ref[b:b + 1, :], pltpu.roll(cv_ref[b], WINDOW - 1, 0))
        ok_ref[b] = newk
        ov_ref[b] = newv
        qb = q_ref[b]
        qbd = jnp.where(own, jnp.concatenate([qb] * N_KV_HEADS, axis=1), 0.0)
        logits = _dot(qbd, newk, _NT) * (HEAD_DIM ** -0.5)
        logits = logits + bias_ref[...]
        m = jnp.maximum(jnp.max(logits, axis=-1, keepdims=True), s)
        p = jnp.exp(logits - m)
        denom = jnp.sum(p, axis=-1, keepdims=True) + jnp.exp(s - m)
        o = jnp.where(own, _dot(p * (1.0 / denom), newv), 0.0)
        acc = o[:, 0:HEAD_DIM]
        for c in range(1, N_KV_HEADS):
            acc = acc + o[:, c * HEAD_DIM:(c + 1) * HEAD_DIM]
        o_ref[b] = acc.astype(BF16)


def swa_sample(q3, zs, cache_k, cache_v, bias_s, sinks_col, *, bb):
    nbatch = q3.shape[0]
    kblk = K_OFF // KV_WIDTH
    vblk = V_OFF // KV_WIDTH
    cache_spec = pl.BlockSpec((bb, WINDOW, KV_WIDTH), lambda i: (i, 0, 0))
    return pl.pallas_call(
        functools.partial(_swa_sample_kernel, bb=bb),
        grid=(nbatch // bb,),
        in_specs=[pl.BlockSpec((bb, N_HEADS, HEAD_DIM), lambda i: (i, 0, 0)),
                  pl.BlockSpec((bb, KV_WIDTH), lambda i: (i, kblk)),
                  pl.BlockSpec((bb, KV_WIDTH), lambda i: (i, vblk)),
                  cache_spec, cache_spec,
                  pl.BlockSpec((N_HEADS, WINDOW), lambda i: (0, 0)),
                  pl.BlockSpec((N_HEADS, 1), lambda i: (0, 0))],
        out_specs=[pl.BlockSpec((bb, N_HEADS, HEAD_DIM), lambda i: (i, 0, 0)), cache_spec, cache_spec],
        out_shape=[jax.ShapeDtypeStruct((nbatch, N_HEADS, HEAD_DIM), BF16),
                   jax.ShapeDtypeStruct((nbatch, WINDOW, KV_WIDTH), F32),
                   jax.ShapeDtypeStruct((nbatch, WINDOW, KV_WIDTH), F32)],
        compiler_params=_cparams(1),
        name="swa_sample",
    )(q3, zs, zs, cache_k, cache_v, bias_s, sinks_col)


def _pool_mix(sums, p, cnts, maps_ref, scale_ref, o_ref):
    for g in range(N_POOL_GROUPS):
        c0 = g * POOL_GROUP_CH
        c1 = c0 + POOL_GROUP_CH
        diff = sums[g] / cnts[g] - p[:, c0:c1]
        mixed = _dot(diff, maps_ref[g])
        o_ref[:, c0:c1] = (mixed * scale_ref[:, c0:c1]).astype(BF16)


def _pool_prompt_kernel(p_ref, halo_ref, maps_ref, scale_ref, o_ref, *, tt):
    i = pl.program_id(1)
    p = p_ref[...]
    halo = jnp.where(i == 0, 0.0, halo_ref[...])
    ext = jnp.concatenate([halo, p], axis=0)
    s = ext
    sums = []
    shift = 1
    for g in range(N_POOL_GROUPS):
        s = s[:, (POOL_GROUP_CH if g > 0 else 0):]
        s = s + pltpu.roll(s, shift, 0)
        shift *= 2
        sums.append(s[POOL_HALO:, 0:POOL_GROUP_CH])
    pos = i * tt + lax.broadcasted_iota(jnp.int32, (tt, 1), 0)
    cnts = [jnp.minimum(pos + 1, w).astype(F32) for w in POOL_WINDOWS]
    _pool_mix(sums, p, cnts, maps_ref, scale_ref, o_ref)


def pool_prompt(z, maps, scale, *, batch, seq, tt):
    nt = seq // tt
    pblk = P_OFF // POOL_CH
    hb = tt // POOL_HALO
    return pl.pallas_call(
        functools.partial(_pool_prompt_kernel, tt=tt),
        grid=(batch, nt),
        in_specs=[pl.BlockSpec((tt, POOL_CH), lambda b, i: (b * nt + i, pblk)),
                  pl.BlockSpec((POOL_HALO, POOL_CH),
                               lambda b, i: (jnp.maximum((b * nt + i) * hb - 1, 0), pblk)),
                  pl.BlockSpec((N_POOL_GROUPS, POOL_GROUP_CH, POOL_GROUP_CH), lambda b, i: (0, 0, 0)),
                  pl.BlockSpec((1, POOL_CH), lambda b, i: (0, 0))],
        out_specs=pl.BlockSpec((tt, POOL_CH), lambda b, i: (b * nt + i, 0)),
        out_shape=jax.ShapeDtypeStruct((batch * seq, POOL_CH), BF16),
        compiler_params=_cparams(2),
        name="pool_prompt",
    )(z, z, maps, scale)


def _pool_sample_kernel(p_ref, st_ref, maps_ref, scale_ref, o_ref, ns_ref):
    p = p_ref[...]
    rows = [st_ref[:, r, :] for r in range(POOL_BUF)]
    sums = []
    for g, w in enumerate(POOL_WINDOWS):
        c0 = g * POOL_GROUP_CH
        c1 = c0 + POOL_GROUP_CH
        s = p[:, c0:c1]
        for r in range(POOL_BUF - (w - 1), POOL_BUF):
            s = s + rows[r][:, c0:c1]
        sums.append(s)
    cnts = [float(min(PAST_LEN + 1, w)) for w in POOL_WINDOWS]
    _pool_mix(sums, p, cnts, maps_ref, scale_ref, o_ref)
    for r in range(POOL_BUF - 1):
        ns_ref[:, r, :] = rows[r + 1]
    ns_ref[:, POOL_BUF - 1, :] = p


def pool_sample(zs, state, maps, scale):
    nbatch = zs.shape[0]
    pblk = P_OFF // POOL_CH
    st_spec = pl.BlockSpec((nbatch, POOL_BUF, POOL_CH), lambda i: (0, 0, 0))
    return pl.pallas_call(
        _pool_sample_kernel,
        grid=(1,),
        in_specs=[pl.BlockSpec((nbatch, POOL_CH), lambda i: (0, pblk)),
                  st_spec,
                  pl.BlockSpec((N_POOL_GROUPS, POOL_GROUP_CH, POOL_GROUP_CH), lambda i: (0, 0, 0)),
                  pl.BlockSpec((1, POOL_CH), lambda i: (0, 0))],
        out_specs=[pl.BlockSpec((nbatch, POOL_CH), lambda i: (0, 0)), st_spec],
        out_shape=[jax.ShapeDtypeStruct((nbatch, POOL_CH), BF16),
                   jax.ShapeDtypeStruct((nbatch, POOL_BUF, POOL_CH), F32)],
        compiler_params=_cparams(1),
        name="pool_sample",
    )(zs, state, maps, scale)


def _mem_prompt_kernel(q_ref, k_ref, v_ref, o_ref):
    q = q_ref[...].astype(BF16)
    k = k_ref[...].astype(BF16)
    v = v_ref[...].astype(BF16)
    for h in range(MEM_HEADS):
        c0 = h * MEM_HEAD_DIM
        c1 = c0 + MEM_HEAD_DIM
        logits = _dot(q[:, c0:c1], k[:, c0:c1], _NT) * (MEM_HEAD_DIM ** -0.5)
        p = jnp.exp(logits - jnp.max(logits, axis=-1, keepdims=True))
        probs = p * (1.0 / jnp.sum(p, axis=-1, keepdims=True))
        o_ref[:, c0:c1] = _dot(probs, v[:, c0:c1]).astype(BF16)


def mem_prompt(z, kv, *, batch, seq, tq):
    nt = seq // tq
    mblk = M_OFF // MEM_WIDTH
    return pl.pallas_call(
        _mem_prompt_kernel,
        grid=(batch, nt),
        in_specs=[pl.BlockSpec((tq, MEM_WIDTH), lambda b, i: (b * nt + i, mblk)),
                  pl.BlockSpec((N_MEM, MEM_WIDTH), lambda b, i: (b, 0)),
                  pl.BlockSpec((N_MEM, MEM_WIDTH), lambda b, i: (b, 1))],
        out_specs=pl.BlockSpec((tq, MEM_WIDTH), lambda b, i: (b * nt + i, 0)),
        out_shape=jax.ShapeDtypeStruct((batch * seq, MEM_WIDTH), BF16),
        compiler_params=_cparams(2),
        name="mem_prompt",
    )(z, kv, kv)


def _bf16_round(x):
    return x.astype(BF16).astype(F32)


def _mem_sample_kernel(q_ref, k_ref, v_ref, o_ref, *, bb):
    rows = []
    for b in range(bb):
        heads = []
        for h in range(MEM_HEADS):
            c0 = h * MEM_HEAD_DIM
            q = _bf16_round(q_ref[b:b + 1, c0:c0 + MEM_HEAD_DIM])
            k = _bf16_round(k_ref[b, :, h, :])
            v = _bf16_round(v_ref[b, :, h, :])
            logits = jnp.sum(k * q, axis=-1, keepdims=True) * (MEM_HEAD_DIM ** -0.5)
            p = jnp.exp(logits - jnp.max(logits, axis=0, keepdims=True))
            probs = _bf16_round(p * (1.0 / jnp.sum(p, axis=0, keepdims=True)))
            heads.append(jnp.sum(probs * v, axis=0, keepdims=True))
        rows.append(jnp.concatenate(heads, axis=1))
    o_ref[...] = jnp.concatenate(rows, axis=0)


def mem_sample(zs, mem_k, mem_v, layer, *, bb):
    nbatch = zs.shape[0]
    mblk = M_OFF // MEM_WIDTH
    kv_spec = pl.BlockSpec((None, bb, N_MEM, MEM_HEADS, MEM_HEAD_DIM), lambda i: (layer, i, 0, 0, 0))
    return pl.pallas_call(
        functools.partial(_mem_sample_kernel, bb=bb),
        grid=(nbatch // bb,),
        in_specs=[pl.BlockSpec((bb, MEM_WIDTH), lambda i: (i, mblk)), kv_spec, kv_spec],
        out_specs=pl.BlockSpec((bb, MEM_WIDTH), lambda i: (i, 0)),
        out_shape=jax.ShapeDtypeStruct((nbatch, MEM_WIDTH), F32),
        compiler_params=_cparams(1),
        name="mem_sample",
    )(zs, mem_k, mem_v)


def _to_row_major(x, o_ref, rows):
    for c in range(ROW_SUB):
        o_ref[pl.ds(c, rows, stride=ROW_SUB), :] = x[:, c * LANES:(c + 1) * LANES]


def _from_row_major(ref, first_row, rows, c):
    return ref[pl.ds(first_row + c, rows, stride=ROW_SUB), :]


def _norm_router_kernel(xp_ref, xs_ref, g_ref, r_ref, xn_ref, idx_ref, wgt_ref, *, tm, n_prompt_steps):
    x = jnp.where(pl.program_id(0) < n_prompt_steps, xp_ref[...], xs_ref[...])
    xn = _rms(x, g_ref[...])
    _to_row_major(xn, xn_ref, tm)
    logits = _dot(xn, r_ref[...])
    lane = lax.broadcasted_iota(jnp.int32, logits.shape, 1)
    logits = jnp.where(lane < N_EXPERTS, logits, -jnp.inf)
    v1 = jnp.max(logits, axis=-1, keepdims=True)
    i1 = jnp.min(jnp.where(logits == v1, lane, ROUTER_PAD), axis=-1, keepdims=True)
    rest = jnp.where(lane == i1, -jnp.inf, logits)
    v2 = jnp.max(rest, axis=-1, keepdims=True)
    i2 = jnp.min(jnp.where(rest == v2, lane, ROUTER_PAD), axis=-1, keepdims=True)
    e2 = jnp.exp(v2 - v1)
    den = 1.0 + e2
    idx_ref[...] = jnp.where(lane == 0, i1, jnp.where(lane == 1, i2, 0))
    wgt_ref[...] = jnp.where(lane == 0, 1.0 / den, jnp.where(lane == 1, e2 / den, 0.0))


def norm_router(xp, xs, g, router_pad, *, tm):
    m_prompt, d = xp.shape
    m = m_prompt + xs.shape[0]
    nps = m_prompt // tm
    return pl.pallas_call(
        functools.partial(_norm_router_kernel, tm=tm, n_prompt_steps=nps),
        grid=(m // tm,),
        in_specs=[pl.BlockSpec((tm, d), lambda i: (jnp.minimum(i, nps - 1), 0)),
                  pl.BlockSpec((tm, d), lambda i: (jnp.maximum(i - nps, 0), 0)),
                  pl.BlockSpec((1, d), lambda i: (0, 0)),
                  pl.BlockSpec((d, ROUTER_PAD), lambda i: (0, 0))],
        out_specs=[pl.BlockSpec((tm * ROW_SUB, LANES), lambda i: (i, 0)),
                   pl.BlockSpec((tm, ROUTER_PAD), lambda i: (i, 0)),
                   pl.BlockSpec((tm, ROUTER_PAD), lambda i: (i, 0))],
        out_shape=[jax.ShapeDtypeStruct((m * ROW_SUB, LANES), F32),
                   jax.ShapeDtypeStruct((m, ROUTER_PAD), jnp.int32),
                   jax.ShapeDtypeStruct((m, ROUTER_PAD), F32)],
        compiler_params=_cparams(1),
        name="norm_router",
    )(xp, xs, g, router_pad)


def _token_copy(src_hbm, token, buf, slot, sem):
    return pltpu.make_async_copy(src_hbm.at[pl.ds(token * ROW_SUB, ROW_SUB)],
                                 buf.at[pl.ds(slot * ROW_SUB, ROW_SUB)], sem)


def _gather_tokens(src_hbm, tok_ref, buf, sem, count):
    def start(s, c):
        _token_copy(src_hbm, tok_ref[0, 0, s], buf, s, sem).start()
        return c

    def wait(s, c):
        _token_copy(src_hbm, tok_ref[0, 0, s], buf, s, sem).wait()
        return c

    lax.fori_loop(0, count, start, 0)
    lax.fori_loop(0, count, wait, 0)


def _dispatch_kernel(used_ref, tok_ref, src_hbm, o_ref, buf, sem, *, rows):
    @pl.when(pl.program_id(0) * rows < used_ref[0])
    def _():
        _gather_tokens(src_hbm, tok_ref, buf, sem, rows)
        for c in range(ROW_SUB):
            o_ref[:, c * LANES:(c + 1) * LANES] = _from_row_major(buf, 0, rows, c).astype(BF16)

    @pl.when(pl.program_id(0) * rows >= used_ref[0])
    def _():
        o_ref[...] = jnp.zeros_like(o_ref)


def moe_dispatch(xn, src_token, used_rows, *, rows):
    n = src_token.shape[0]
    nchunk = n // rows
    return pl.pallas_call(
        functools.partial(_dispatch_kernel, rows=rows),
        grid_spec=pltpu.PrefetchScalarGridSpec(
            num_scalar_prefetch=1,
            grid=(nchunk,),
            in_specs=[pl.BlockSpec((1, 1, rows), lambda i, used: (i, 0, 0), memory_space=pltpu.SMEM),
                      pl.BlockSpec(memory_space=pl.ANY)],
            out_specs=pl.BlockSpec((rows, D_MODEL), lambda i, used: (i, 0)),
            scratch_shapes=[pltpu.VMEM((rows * ROW_SUB, LANES), F32), pltpu.SemaphoreType.DMA(())]),
        out_shape=jax.ShapeDtypeStruct((n, D_MODEL), BF16),
        compiler_params=_cparams(1),
        name="moe_dispatch",
    )(used_rows, src_token.reshape(nchunk, 1, rows), xn)


def _moe_up_kernel(te_ref, tv_ref, x_ref, w1_ref, w3_ref, o_ref):
    @pl.when(tv_ref[pl.program_id(0)] > 0)
    def _():
        x = x_ref[...]
        a = _dot(x, w1_ref[0])
        b = _dot(x, w3_ref[0])
        o_ref[...] = (jax.nn.silu(a) * b).astype(BF16)

    @pl.when(tv_ref[pl.program_id(0)] == 0)
    def _():
        o_ref[...] = jnp.zeros_like(o_ref)


def _weight_map(nj):
    def w_map(t, j, te, tv):
        return (te[t], 0, jnp.where(tv[t] > 0, j, nj - 1))

    return w_map


def moe_up(xs, w1, w3, tiles, *, tm, tn):
    n, d = xs.shape
    f = w1.shape[2]
    nj = f // tn
    return pl.pallas_call(
        _moe_up_kernel,
        grid_spec=pltpu.PrefetchScalarGridSpec(
            num_scalar_prefetch=2,
            grid=(n // tm, nj),
            in_specs=[pl.BlockSpec((tm, d), lambda t, j, te, tv: (t, 0)),
                      pl.BlockSpec((1, d, tn), _weight_map(nj)),
                      pl.BlockSpec((1, d, tn), _weight_map(nj))],
            out_specs=pl.BlockSpec((tm, tn), lambda t, j, te, tv: (t, j))),
        out_shape=jax.ShapeDtypeStruct((n, f), BF16),
        compiler_params=_cparams(2),
        name="moe_up",
    )(*tiles, xs, w1, w3)


def _moe_down_kernel(te_ref, tv_ref, h_ref, w2_ref, o_ref, *, tm, tn):
    j = pl.program_id(1)
    groups = tn // LANES

    def put(val):
        for c in range(groups):
            o_ref[pl.ds(j * groups + c, tm, stride=ROW_SUB), :] = val[:, c * LANES:(c + 1) * LANES]

    @pl.when(tv_ref[pl.program_id(0)] > 0)
    def _():
        put(_dot(h_ref[...], w2_ref[0]))

    @pl.when(tv_ref[pl.program_id(0)] == 0)
    def _():
        put(jnp.zeros((tm, tn), F32))


def moe_down(h, w2, tiles, *, tm, tn):
    n, f = h.shape
    d = w2.shape[2]
    nj = d // tn
    return pl.pallas_call(
        functools.partial(_moe_down_kernel, tm=tm, tn=tn),
        grid_spec=pltpu.PrefetchScalarGridSpec(
            num_scalar_prefetch=2,
            grid=(n // tm, nj),
            in_specs=[pl.BlockSpec((tm, f), lambda t, j, te, tv: (t, 0)),
                      pl.BlockSpec((1, f, tn), _weight_map(nj))],
            out_specs=pl.BlockSpec((tm * ROW_SUB, LANES), lambda t, j, te, tv: (t, 0))),
        out_shape=jax.ShapeDtypeStruct((n * ROW_SUB, LANES), F32),
        compiler_params=_cparams(2),
        name="moe_down",
    )(*tiles, h, w2)


def _combine_kernel(pos_ref, xp_ref, xs_ref, w_ref, g_ref, ys_hbm, yp_ref, ysmp_ref, buf, sem,
                    *, rows, n_prompt_chunks):
    i = pl.program_id(0)
    _gather_tokens(ys_hbm, pos_ref, buf, sem, TOP_K * rows)
    w0 = w_ref[:, 0:1]
    w1 = w_ref[:, 1:2]

    def finish(x_ref, o_ref):
        for c in range(ROW_SUB):
            a = _from_row_major(buf, 0, rows, c)
            b = _from_row_major(buf, rows * ROW_SUB, rows, c)
            o_ref[:, c * LANES:(c + 1) * LANES] = x_ref[:, c * LANES:(c + 1) * LANES] + (w0 * a + w1 * b)
        o_ref[...] = _rms(o_ref[...], g_ref[...])

    @pl.when(i < n_prompt_chunks)
    def _():
        finish(xp_ref, yp_ref)

    @pl.when(i >= n_prompt_chunks)
    def _():
        finish(xs_ref, ysmp_ref)


def moe_combine(xp, xs, wgt, g_final, ys, dest, *, rows):
    m_prompt, d = xp.shape
    m = m_prompt + xs.shape[0]
    nchunk = m // rows
    npc = m_prompt // rows
    pos = dest.reshape(nchunk, rows, TOP_K).transpose(0, 2, 1).reshape(nchunk, 1, TOP_K * rows)

    def p_map(i):
        return (jnp.minimum(i, npc - 1), 0)

    def s_map(i):
        return (jnp.maximum(i - npc, 0), 0)

    return pl.pallas_call(
        functools.partial(_combine_kernel, rows=rows, n_prompt_chunks=npc),
        grid=(nchunk,),
        in_specs=[pl.BlockSpec((1, 1, TOP_K * rows), lambda i: (i, 0, 0), memory_space=pltpu.SMEM),
                  pl.BlockSpec((rows, d), p_map),
                  pl.BlockSpec((rows, d), s_map),
                  pl.BlockSpec((rows, ROUTER_PAD), lambda i: (i, 0)),
                  pl.BlockSpec((1, d), lambda i: (0, 0)),
                  pl.BlockSpec(memory_space=pl.ANY)],
        out_specs=[pl.BlockSpec((rows, d), p_map), pl.BlockSpec((rows, d), s_map)],
        out_shape=[jax.ShapeDtypeStruct((m_prompt, d), F32),
                   jax.ShapeDtypeStruct((m - m_prompt, d), F32)],
        scratch_shapes=[pltpu.VMEM((TOP_K * rows * ROW_SUB, LANES), F32), pltpu.SemaphoreType.DMA(())],
        compiler_params=_cparams(1),
        name="moe_combine",
    )(pos, xp, xs, wgt, g_final, ys)


def _routing_tables(idx2, *, tm, n_rows):
    e_flat = idx2.reshape(-1)
    onehot = (e_flat[:, None] == jnp.arange(N_EXPERTS, dtype=jnp.int32)[None, :]).astype(jnp.int32)
    csum = jnp.cumsum(onehot, axis=0)
    counts = csum[-1]
    rank = jnp.sum(csum * onehot, axis=1) - 1
    padded = ((counts + tm - 1) // tm) * tm
    ends = jnp.cumsum(padded)
    starts = ends - padded
    dest = (jnp.sum(onehot * starts[None, :], axis=1) + rank).astype(jnp.int32)
    token = jnp.arange(e_flat.shape[0], dtype=jnp.int32) // TOP_K
    src_token = jnp.zeros((n_rows,), jnp.int32).at[dest].set(token)
    tile_ids = jnp.arange(n_rows // tm, dtype=jnp.int32)
    used_tiles = ends[-1] // tm
    tile_valid = (tile_ids < used_tiles).astype(jnp.int32)
    tile_start = jnp.minimum(tile_ids, used_tiles - 1) * tm
    tile_expert = jnp.sum((tile_start[:, None] >= ends[None, :]).astype(jnp.int32), axis=1)
    return dest.reshape(-1, TOP_K), src_token, ends[-1:].astype(jnp.int32), (tile_expert, tile_valid)


def _rel_bias_by_distance(rel_bias_table):
    d = jnp.arange(WINDOW, dtype=jnp.int32)
    max_exact = REL_BUCKETS // 2
    d_f = jnp.maximum(d, 1).astype(F32)
    large = max_exact + (jnp.log(d_f / max_exact) / math.log(REL_MAX_DIST / max_exact)
                         * (REL_BUCKETS - max_exact)).astype(jnp.int32)
    large = jnp.minimum(large, REL_BUCKETS - 1)
    bucket = jnp.where(d < max_exact, d, large)
    return rel_bias_table[bucket].T


def _mixer_prompt(x, mem2d, wts, lw, layer, bias2, *, batch, seq):
    kv = norm_matmul(mem2d, lw["g_mem"], wts["w_mem_kv"], layer, tm=1024, tn=512)
    z = norm_matmul(x, lw["g_attn"], wts["w_in"], layer, tm=1024, tn=512, gate_from_col=G_OFF)
    ya = swa_prompt(z, bias2, lw["sinks"], batch=batch, seq=seq)
    yp = pool_prompt(z, lw["pool_maps"], lw["pool_scale"], batch=batch, seq=seq, tt=512)
    ym = mem_prompt(z, kv, batch=batch, seq=seq, tq=512)
    merged = merge_branches(ya, yp, ym, wts["w_up_attn"], wts["w_up_pool"], wts["w_up_mem"], layer, z,
                            tm=1024, tn=512)
    x = matmul_residual(merged, wts["w_o"], layer, x, tm=1024, tn=512)
    z3 = z.reshape(batch, seq, IN_COLS)
    new_k = z3[:, seq - WINDOW:, K_OFF:V_OFF].reshape(batch, WINDOW, N_KV_HEADS, HEAD_DIM)
    new_v = z3[:, seq - WINDOW:, V_OFF:P_OFF].reshape(batch, WINDOW, N_KV_HEADS, HEAD_DIM)
    new_pool = z3[:, seq - POOL_BUF:, P_OFF:M_OFF]
    kv3 = kv.reshape(batch, N_MEM, 2 * MEM_WIDTH)
    mk = kv3[..., :MEM_WIDTH].reshape(batch, N_MEM, MEM_HEADS, MEM_HEAD_DIM)
    mv = kv3[..., MEM_WIDTH:].reshape(batch, N_MEM, MEM_HEADS, MEM_HEAD_DIM)
    return x, new_k, new_v, new_pool, mk, mv


def _mixer_sample(x, cache_k, cache_v, state, mem_k, mem_v, wts, lw, layer, bias_s):
    nb = x.shape[0]
    z = norm_matmul(x, lw["g_attn"], wts["w_in"], layer, tm=nb, tn=512, gate_from_col=G_OFF)
    q3 = z[:, Q_OFF:K_OFF].reshape(nb, N_HEADS, HEAD_DIM)
    ya3, new_k, new_v = swa_sample(q3, z, cache_k.reshape(nb, WINDOW, KV_WIDTH),
                                   cache_v.reshape(nb, WINDOW, KV_WIDTH), bias_s, lw["sinks"][:, None], bb=8)
    yp, new_pool = pool_sample(z, state, lw["pool_maps"], lw["pool_scale"])
    ym = mem_sample(z, mem_k, mem_v, layer, bb=8)
    ya = ya3.reshape(nb, ATTN_WIDTH)
    merged = merge_branches(ya, yp, ym, wts["w_up_attn"], wts["w_up_pool"], wts["w_up_mem"], layer, z,
                            tm=nb, tn=512)
    x = matmul_residual(merged, wts["w_o"], layer, x, tm=nb, tn=512)
    return (x, new_k.reshape(nb, WINDOW, N_KV_HEADS, HEAD_DIM),
            new_v.reshape(nb, WINDOW, N_KV_HEADS, HEAD_DIM), new_pool)


def _dense_ffn(x, g, w1, w3, w2, layer, *, tm):
    h = norm_swiglu(x, g, w1, w3, layer, tm=tm, tn=512)
    return matmul_residual(h, w2, layer, x, tm=tm, tn=512)


def _moe_ffn_final(xp, xs, g, router, w1, w3, w2, g_final):
    m_prompt = xp.shape[0]
    m = m_prompt + xs.shape[0]
    router_pad = jnp.pad(router, ((0, 0), (0, ROUTER_PAD - N_EXPERTS)))
    xn, idx_pad, wgt = norm_router(xp, xs, g, router_pad, tm=COMBINE_ROWS)
    n_rows = pl.cdiv(TOP_K * m, MOE_TM) * MOE_TM + N_EXPERTS * MOE_TM
    dest, src_token, used_rows, tiles = _routing_tables(idx_pad[:, :TOP_K], tm=MOE_TM, n_rows=n_rows)
    xsort = moe_dispatch(xn, src_token, used_rows, rows=DISPATCH_ROWS)
    h = moe_up(xsort, w1, w3, tiles, tm=MOE_TM, tn=512)
    ys = moe_down(h, w2, tiles, tm=MOE_TM, tn=512)
    return moe_combine(xp, xs, wgt, g_final, ys, dest, rows=COMBINE_ROWS)


def kernel(x_prompt, x_sample, mem_prompt, cache_swa_k, cache_swa_v, state_pool, cache_mem_k, cache_mem_v, rel_bias_table, g_attn, w_in, attn_sinks, pool_maps, pool_scale, g_mem, w_mem_kv, w_up_attn, w_up_pool, w_up_mem, w_o, g_ffn, ffn_w1, ffn_w3, ffn_w2, moe_router, moe_w1, moe_w3, moe_w2, g_final):
    batch, seq, d = x_prompt.shape
    nb_s = x_sample.shape[0]
    depth = w_in.shape[0]
    assert depth == 2 and x_sample.shape[1] == 1

    bias_hw = _rel_bias_by_distance(rel_bias_table).astype(F32)
    col = jnp.arange(2 * WINDOW)
    bias2 = bias_hw[:, jnp.clip(WINDOW - col, 0, WINDOW - 1)]
    bias_s = bias_hw[:, ::-1]

    wts = dict(w_in=w_in.astype(BF16), w_mem_kv=w_mem_kv.astype(BF16), w_up_attn=w_up_attn.astype(BF16),
               w_up_pool=w_up_pool.astype(BF16), w_up_mem=w_up_mem.astype(BF16), w_o=w_o.astype(BF16))
    ffn = (ffn_w1.astype(BF16), ffn_w3.astype(BF16), ffn_w2.astype(BF16))

    xp = x_prompt.reshape(batch * seq, d)
    xs = x_sample.reshape(nb_s, d)
    mem2d = mem_prompt.reshape(batch * N_MEM, d)
    p_k, p_v, p_pool, p_mk, p_mv, s_k, s_v, s_pool = [], [], [], [], [], [], [], []
    for l in range(depth):
        lw = dict(g_attn=g_attn[l][None, :], sinks=attn_sinks[l], pool_maps=pool_maps[l].astype(BF16),
                  pool_scale=pool_scale[l][None, :], g_mem=g_mem[l][None, :])
        xp, nk, nv, npool, mk, mv = _mixer_prompt(xp, mem2d, wts, lw, l, bias2, batch=batch, seq=seq)
        p_k.append(nk); p_v.append(nv); p_pool.append(npool); p_mk.append(mk); p_mv.append(mv)
        xs, nk, nv, npool = _mixer_sample(xs, cache_swa_k[l], cache_swa_v[l], state_pool[l],
                                          cache_mem_k, cache_mem_v, wts, lw, l, bias_s)
        s_k.append(nk); s_v.append(nv); s_pool.append(npool)
        gl = g_ffn[l][None, :]
        i = l // 2
        if l % 2 == 0:
            xp = _dense_ffn(xp, gl, *ffn, i, tm=1024)
            xs = _dense_ffn(xs, gl, *ffn, i, tm=nb_s)
        else:
            yp, ys = _moe_ffn_final(xp, xs, gl, moe_router[i], moe_w1[i].astype(BF16),
                                    moe_w3[i].astype(BF16), moe_w2[i].astype(BF16), g_final[None, :])
    y_prompt = yp.reshape(batch, seq, d)
    y_sample = ys.reshape(nb_s, 1, d)
    return (y_prompt, y_sample,
            jnp.stack(p_k), jnp.stack(p_v), jnp.stack(p_pool), jnp.stack(p_mk), jnp.stack(p_mv),
            jnp.stack(s_k), jnp.stack(s_v), jnp.stack(s_pool))
```

```python
import functools
import math

import jax
import jax.numpy as jnp
from jax import lax
from jax.experimental import pallas as pl
from jax.experimental.pallas import tpu as pltpu

F32 = jnp.float32
BF16 = jnp.bfloat16

D_MODEL = 2048
N_HEADS = 16
N_KV_HEADS = 4
HEAD_DIM = 64
Q_GROUP = N_HEADS // N_KV_HEADS
WINDOW = 128
ATTN_WIDTH = N_HEADS * HEAD_DIM
KV_WIDTH = N_KV_HEADS * HEAD_DIM
REL_BUCKETS = 32
REL_MAX_DIST = 128
POOL_WINDOWS = (2, 4, 8, 16)
N_POOL_GROUPS = 4
POOL_CH = D_MODEL // 4
POOL_GROUP_CH = POOL_CH // N_POOL_GROUPS
POOL_BUF = max(POOL_WINDOWS) - 1
N_MEM = 256
MEM_HEADS = 4
MEM_HEAD_DIM = 128
MEM_WIDTH = MEM_HEADS * MEM_HEAD_DIM
N_BRANCHES = 3
Q_OFF = 0
K_OFF = Q_OFF + ATTN_WIDTH
V_OFF = K_OFF + KV_WIDTH
P_OFF = V_OFF + KV_WIDTH
M_OFF = P_OFF + POOL_CH
G_OFF = M_OFF + MEM_WIDTH
IN_COLS = G_OFF + N_BRANCHES * D_MODEL
N_EXPERTS = 8
TOP_K = 2
RMS_EPS = 1e-5
NEG_INF = -1e30
PAST_LEN = 8192

VMEM_LIMIT_BYTES = 56 * 1024 * 1024
LANES = 128
POOL_HALO = 16
ROUTER_PAD = LANES
ROW_SUB = D_MODEL // LANES
BUF_PITCH = ROW_SUB + 8
GATHER_UNROLL = 8
MOE_TM = 512
DISPATCH_ROWS = 256
COMBINE_ROWS = 128


def _cparams(n_axes):
    return pltpu.CompilerParams(dimension_semantics=("arbitrary",) * n_axes,
                                vmem_limit_bytes=VMEM_LIMIT_BYTES)


def _rms(x, g):
    return (x * lax.rsqrt(jnp.mean(x * x, axis=-1, keepdims=True) + RMS_EPS)) * g


_NN = (((1,), (0,)), ((), ()))
_NT = (((1,), (1,)), ((), ()))


def _dot(a, b, dims=_NN):
    return lax.dot_general(a.astype(BF16), b.astype(BF16), dims, preferred_element_type=F32)


def _layer_cols(layer, rows, tn):
    return pl.BlockSpec((None, rows, tn), lambda i, j: (layer, 0, j))


def _norm_mm_kernel(x_ref, g_ref, w_ref, o_ref, xn_ref, *, gate_from):
    j = pl.program_id(1)

    @pl.when(j == 0)
    def _():
        xn_ref[...] = _rms(x_ref[...], g_ref[...]).astype(BF16)

    acc = _dot(xn_ref[...], w_ref[...])
    if gate_from is None:
        o_ref[...] = acc
    else:
        @pl.when(j < gate_from)
        def _():
            o_ref[...] = acc

        @pl.when(j >= gate_from)
        def _():
            o_ref[...] = jax.nn.sigmoid(acc)


def norm_matmul(x, g, w, layer, *, tm, tn, gate_from_col=None):
    m, d = x.shape
    n = w.shape[2]
    gate_from = None if gate_from_col is None else gate_from_col // tn
    return pl.pallas_call(
        functools.partial(_norm_mm_kernel, gate_from=gate_from),
        grid=(m // tm, n // tn),
        in_specs=[pl.BlockSpec((tm, d), lambda i, j: (i, 0)),
                  pl.BlockSpec((1, d), lambda i, j: (0, 0)),
                  _layer_cols(layer, d, tn)],
        out_specs=pl.BlockSpec((tm, tn), lambda i, j: (i, j)),
        out_shape=jax.ShapeDtypeStruct((m, n), F32),
        scratch_shapes=[pltpu.VMEM((tm, d), BF16)],
        compiler_params=_cparams(2),
        name="norm_matmul",
    )(x, g, w)


def _norm_swiglu_kernel(x_ref, g_ref, w1_ref, w3_ref, o_ref, xn_ref):
    @pl.when(pl.program_id(1) == 0)
    def _():
        xn_ref[...] = _rms(x_ref[...], g_ref[...]).astype(BF16)

    xn = xn_ref[...]
    a = _dot(xn, w1_ref[...])
    b = _dot(xn, w3_ref[...])
    o_ref[...] = (jax.nn.silu(a) * b).astype(BF16)


def norm_swiglu(x, g, w1, w3, layer, *, tm, tn):
    m, d = x.shape
    n = w1.shape[2]
    return pl.pallas_call(
        _norm_swiglu_kernel,
        grid=(m // tm, n // tn),
        in_specs=[pl.BlockSpec((tm, d), lambda i, j: (i, 0)),
                  pl.BlockSpec((1, d), lambda i, j: (0, 0)),
                  _layer_cols(layer, d, tn), _layer_cols(layer, d, tn)],
        out_specs=pl.BlockSpec((tm, tn), lambda i, j: (i, j)),
        out_shape=jax.ShapeDtypeStruct((m, n), BF16),
        scratch_shapes=[pltpu.VMEM((tm, d), BF16)],
        compiler_params=_cparams(2),
        name="norm_swiglu",
    )(x, g, w1, w3)


def _mm_res_kernel(a_ref, w_ref, r_ref, o_ref):
    o_ref[...] = r_ref[...] + _dot(a_ref[...], w_ref[...])


def matmul_residual(a, w, layer, res, *, tm, tn):
    m, k = a.shape
    n = w.shape[2]
    return pl.pallas_call(
        _mm_res_kernel,
        grid=(m // tm, n // tn),
        in_specs=[pl.BlockSpec((tm, k), lambda i, j: (i, 0)),
                  _layer_cols(layer, k, tn),
                  pl.BlockSpec((tm, tn), lambda i, j: (i, j))],
        out_specs=pl.BlockSpec((tm, tn), lambda i, j: (i, j)),
        out_shape=jax.ShapeDtypeStruct((m, n), F32),
        compiler_params=_cparams(2),
        name="matmul_residual",
    )(a, w, res)


def _merge_kernel(ya_ref, yp_ref, ym_ref, wa_ref, wp_ref, wm_ref, g0_ref, g1_ref, g2_ref, o_ref):
    ua = _dot(ya_ref[...], wa_ref[...])
    up = _dot(yp_ref[...], wp_ref[...])
    um = _dot(ym_ref[...], wm_ref[...])
    o_ref[...] = (g0_ref[...] * ua + g1_ref[...] * up + g2_ref[...] * um).astype(BF16)


def merge_branches(ya, yp, ym, wa, wp, wm, layer, z, *, tm, tn):
    m = ya.shape[0]
    nj = D_MODEL // tn
    g_blk = G_OFF // tn

    def gate_spec(b):
        return pl.BlockSpec((tm, tn), lambda i, j: (i, g_blk + b * nj + j))

    return pl.pallas_call(
        _merge_kernel,
        grid=(m // tm, nj),
        in_specs=[pl.BlockSpec((tm, ATTN_WIDTH), lambda i, j: (i, 0)),
                  pl.BlockSpec((tm, POOL_CH), lambda i, j: (i, 0)),
                  pl.BlockSpec((tm, MEM_WIDTH), lambda i, j: (i, 0)),
                  _layer_cols(layer, ATTN_WIDTH, tn), _layer_cols(layer, POOL_CH, tn),
                  _layer_cols(layer, MEM_WIDTH, tn),
                  gate_spec(0), gate_spec(1), gate_spec(2)],
        out_specs=pl.BlockSpec((tm, tn), lambda i, j: (i, j)),
        out_shape=jax.ShapeDtypeStruct((m, D_MODEL), BF16),
        compiler_params=_cparams(2),
        name="merge_branches",
    )(ya, yp, ym, wa, wp, wm, z, z, z)


def _swa_prompt_kernel(q_ref, kp_ref, kc_ref, vp_ref, vc_ref, base_ref, sink_ref, o_ref, bias_ref):
    i = pl.program_id(1)

    @pl.when((pl.program_id(0) == 0) & (i == 0))
    def _():
        for h in range(N_HEADS):
            row0 = jnp.broadcast_to(base_ref[h:h + 1, :], (WINDOW, 2 * WINDOW))
            bias_ref[h] = pltpu.roll(row0, 0, 1, stride=1, stride_axis=0)

    q = q_ref[...].astype(BF16)
    k2 = jnp.concatenate([kp_ref[...], kc_ref[...]], axis=0).astype(BF16)
    v2 = jnp.concatenate([vp_ref[...], vc_ref[...]], axis=0).astype(BF16)
    row = lax.broadcasted_iota(jnp.int32, (WINDOW, 2 * WINDOW), 0)
    col = lax.broadcasted_iota(jnp.int32, (WINDOW, 2 * WINDOW), 1)
    dist = row + WINDOW - col
    valid = (dist >= 0) & (dist < WINDOW) & ((i > 0) | (col >= WINDOW))
    for h in range(N_HEADS):
        kv = h // Q_GROUP
        qh = q[:, h * HEAD_DIM:(h + 1) * HEAD_DIM]
        kh = k2[:, kv * HEAD_DIM:(kv + 1) * HEAD_DIM]
        vh = v2[:, kv * HEAD_DIM:(kv + 1) * HEAD_DIM]
        logits = _dot(qh, kh, _NT) * (HEAD_DIM ** -0.5)
        logits = jnp.where(valid, logits + bias_ref[h], NEG_INF)
        s = sink_ref[h]
        m = jnp.maximum(jnp.max(logits, axis=-1, keepdims=True), s)
        p = jnp.exp(logits - m)
        denom = jnp.sum(p, axis=-1, keepdims=True) + jnp.exp(s - m)
        o = _dot(p * (1.0 / denom), vh)
        o_ref[:, h * HEAD_DIM:(h + 1) * HEAD_DIM] = o.astype(BF16)


def swa_prompt(z, bias_base, sinks, *, batch, seq):
    nb = seq // WINDOW
    kblk = K_OFF // KV_WIDTH
    vblk = V_OFF // KV_WIDTH

    def cur(c):
        return pl.BlockSpec((WINDOW, KV_WIDTH), lambda b, i: (b * nb + i, c))

    def prev(c):
        return pl.BlockSpec((WINDOW, KV_WIDTH), lambda b, i: (b * nb + jnp.maximum(i - 1, 0), c))

    return pl.pallas_call(
        _swa_prompt_kernel,
        grid=(batch, nb),
        in_specs=[pl.BlockSpec((WINDOW, ATTN_WIDTH), lambda b, i: (b * nb + i, 0)),
                  prev(kblk), cur(kblk), prev(vblk), cur(vblk),
                  pl.BlockSpec((N_HEADS, 2 * WINDOW), lambda b, i: (0, 0)),
                  pl.BlockSpec(memory_space=pltpu.SMEM)],
        out_specs=pl.BlockSpec((WINDOW, ATTN_WIDTH), lambda b, i: (b * nb + i, 0)),
        out_shape=jax.ShapeDtypeStruct((batch * seq, ATTN_WIDTH), BF16),
        scratch_shapes=[pltpu.VMEM((N_HEADS, WINDOW, 2 * WINDOW), F32)],
        compiler_params=_cparams(2),
        name="swa_prompt",
    )(z, z, z, z, z, bias_base, sinks)


def _swa_sample_kernel(q_ref, kn_ref, vn_ref, ck_ref, cv_ref, bias_ref, sink_ref,
                       o_ref, ok_ref, ov_ref, *, bb):
    row = lax.broadcasted_iota(jnp.int32, (WINDOW, KV_WIDTH), 0)
    hrow = lax.broadcasted_iota(jnp.int32, (N_HEADS, KV_WIDTH), 0)
    hcol = lax.broadcasted_iota(jnp.int32, (N_HEADS, KV_WIDTH), 1)
    own = (hcol // HEAD_DIM) == (hrow // Q_GROUP)
    s = sink_ref[...]
    for b in range(bb):
        newk = jnp.where(row == WINDOW - 1, kn_ref[b:b + 1, :], pltpu.roll(ck_ref[b], WINDOW - 1, 0))
        newv = jnp.where(row == WINDOW - 1, vn_ref[b:b + 1, :], pltpu.roll(cv_ref[b], WINDOW - 1, 0))
        ok_ref[b] = newk
        ov_ref[b] = newv
        qb = q_ref[b]
        qbd = jnp.where(own, jnp.concatenate([qb] * N_KV_HEADS, axis=1), 0.0)
        logits = _dot(qbd, newk, _NT) * (HEAD_DIM ** -0.5)
        logits = logits + bias_ref[...]
        m = jnp.maximum(jnp.max(logits, axis=-1, keepdims=True), s)
        p = jnp.exp(logits - m)
        denom = jnp.sum(p, axis=-1, keepdims=True) + jnp.exp(s - m)
        o = jnp.where(own, _dot(p * (1.0 / denom), newv), 0.0)
        acc = o[:, 0:HEAD_DIM]
        for c in range(1, N_KV_HEADS):
            acc = acc + o[:, c * HEAD_DIM:(c + 1) * HEAD_DIM]
        o_ref[b] = acc.astype(BF16)


def swa_sample(q3, zs, cache_k, cache_v, bias_s, sinks_col, *, bb):
    nbatch = q3.shape[0]
    kblk = K_OFF // KV_WIDTH
    vblk = V_OFF // KV_WIDTH
    cache_spec = pl.BlockSpec((bb, WINDOW, KV_WIDTH), lambda i: (i, 0, 0))
    return pl.pallas_call(
        functools.partial(_swa_sample_kernel, bb=bb),
        grid=(nbatch // bb,),
        in_specs=[pl.BlockSpec((bb, N_HEADS, HEAD_DIM), lambda i: (i, 0, 0)),
                  pl.BlockSpec((bb, KV_WIDTH), lambda i: (i, kblk)),
                  pl.BlockSpec((bb, KV_WIDTH), lambda i: (i, vblk)),
                  cache_spec, cache_spec,
                  pl.BlockSpec((N_HEADS, WINDOW), lambda i: (0, 0)),
                  pl.BlockSpec((N_HEADS, 1), lambda i: (0, 0))],
        out_specs=[pl.BlockSpec((bb, N_HEADS, HEAD_DIM), lambda i: (i, 0, 0)), cache_spec, cache_spec],
        out_shape=[jax.ShapeDtypeStruct((nbatch, N_HEADS, HEAD_DIM), BF16),
                   jax.ShapeDtypeStruct((nbatch, WINDOW, KV_WIDTH), F32),
                   jax.ShapeDtypeStruct((nbatch, WINDOW, KV_WIDTH), F32)],
        compiler_params=_cparams(1),
        name="swa_sample",
    )(q3, zs, zs, cache_k, cache_v, bias_s, sinks_col)


def _pool_mix(sums, p, cnts, maps_ref, scale_ref, o_ref):
    for g in range(N_POOL_GROUPS):
        c0 = g * POOL_GROUP_CH
        c1 = c0 + POOL_GROUP_CH
        diff = sums[g] / cnts[g] - p[:, c0:c1]
        mixed = _dot(diff, maps_ref[g])
        o_ref[:, c0:c1] = (mixed * scale_ref[:, c0:c1]).astype(BF16)


def _pool_prompt_kernel(p_ref, halo_ref, maps_ref, scale_ref, o_ref, *, tt):
    i = pl.program_id(1)
    p = p_ref[...]
    halo = jnp.where(i == 0, 0.0, halo_ref[...])
    ext = jnp.concatenate([halo, p], axis=0)
    s = ext
    sums = []
    shift = 1
    for g in range(N_POOL_GROUPS):
        s = s[:, (POOL_GROUP_CH if g > 0 else 0):]
        s = s + pltpu.roll(s, shift, 0)
        shift *= 2
        sums.append(s[POOL_HALO:, 0:POOL_GROUP_CH])
    pos = i * tt + lax.broadcasted_iota(jnp.int32, (tt, 1), 0)
    cnts = [jnp.minimum(pos + 1, w).astype(F32) for w in POOL_WINDOWS]
    _pool_mix(sums, p, cnts, maps_ref, scale_ref, o_ref)


def pool_prompt(z, maps, scale, *, batch, seq, tt):
    nt = seq // tt
    pblk = P_OFF // POOL_CH
    hb = tt // POOL_HALO
    return pl.pallas_call(
        functools.partial(_pool_prompt_kernel, tt=tt),
        grid=(batch, nt),
        in_specs=[pl.BlockSpec((tt, POOL_CH), lambda b, i: (b * nt + i, pblk)),
                  pl.BlockSpec((POOL_HALO, POOL_CH),
                               lambda b, i: (jnp.maximum((b * nt + i) * hb - 1, 0), pblk)),
                  pl.BlockSpec((N_POOL_GROUPS, POOL_GROUP_CH, POOL_GROUP_CH), lambda b, i: (0, 0, 0)),
                  pl.BlockSpec((1, POOL_CH), lambda b, i: (0, 0))],
        out_specs=pl.BlockSpec((tt, POOL_CH), lambda b, i: (b * nt + i, 0)),
        out_shape=jax.ShapeDtypeStruct((batch * seq, POOL_CH), BF16),
        compiler_params=_cparams(2),
        name="pool_prompt",
    )(z, z, maps, scale)


def _pool_sample_kernel(p_ref, st_ref, maps_ref, scale_ref, o_ref, ns_ref):
    p = p_ref[...]
    rows = [st_ref[:, r, :] for r in range(POOL_BUF)]
    sums = []
    for g, w in enumerate(POOL_WINDOWS):
        c0 = g * POOL_GROUP_CH
        c1 = c0 + POOL_GROUP_CH
        s = p[:, c0:c1]
        for r in range(POOL_BUF - (w - 1), POOL_BUF):
            s = s + rows[r][:, c0:c1]
        sums.append(s)
    cnts = [float(min(PAST_LEN + 1, w)) for w in POOL_WINDOWS]
    _pool_mix(sums, p, cnts, maps_ref, scale_ref, o_ref)
    for r in range(POOL_BUF - 1):
        ns_ref[:, r, :] = rows[r + 1]
    ns_ref[:, POOL_BUF - 1, :] = p


def pool_sample(zs, state, maps, scale):
    nbatch = zs.shape[0]
    pblk = P_OFF // POOL_CH
    st_spec = pl.BlockSpec((nbatch, POOL_BUF, POOL_CH), lambda i: (0, 0, 0))
    return pl.pallas_call(
        _pool_sample_kernel,
        grid=(1,),
        in_specs=[pl.BlockSpec((nbatch, POOL_CH), lambda i: (0, pblk)),
                  st_spec,
                  pl.BlockSpec((N_POOL_GROUPS, POOL_GROUP_CH, POOL_GROUP_CH), lambda i: (0, 0, 0)),
                  pl.BlockSpec((1, POOL_CH), lambda i: (0, 0))],
        out_specs=[pl.BlockSpec((nbatch, POOL_CH), lambda i: (0, 0)), st_spec],
        out_shape=[jax.ShapeDtypeStruct((nbatch, POOL_CH), BF16),
                   jax.ShapeDtypeStruct((nbatch, POOL_BUF, POOL_CH), F32)],
        compiler_params=_cparams(1),
        name="pool_sample",
    )(zs, state, maps, scale)


def _mem_prompt_kernel(q_ref, k_ref, v_ref, o_ref):
    q = q_ref[...].astype(BF16)
    k = k_ref[...].astype(BF16)
    v = v_ref[...].astype(BF16)
    for h in range(MEM_HEADS):
        c0 = h * MEM_HEAD_DIM
        c1 = c0 + MEM_HEAD_DIM
        logits = _dot(q[:, c0:c1], k[:, c0:c1], _NT) * (MEM_HEAD_DIM ** -0.5)
        p = jnp.exp(logits - jnp.max(logits, axis=-1, keepdims=True))
        probs = p * (1.0 / jnp.sum(p, axis=-1, keepdims=True))
        o_ref[:, c0:c1] = _dot(probs, v[:, c0:c1]).astype(BF16)


def mem_prompt(z, kv, *, batch, seq, tq):
    nt = seq // tq
    mblk = M_OFF // MEM_WIDTH
    return pl.pallas_call(
        _mem_prompt_kernel,
        grid=(batch, nt),
        in_specs=[pl.BlockSpec((tq, MEM_WIDTH), lambda b, i: (b * nt + i, mblk)),
                  pl.BlockSpec((N_MEM, MEM_WIDTH), lambda b, i: (b, 0)),
                  pl.BlockSpec((N_MEM, MEM_WIDTH), lambda b, i: (b, 1))],
        out_specs=pl.BlockSpec((tq, MEM_WIDTH), lambda b, i: (b * nt + i, 0)),
        out_shape=jax.ShapeDtypeStruct((batch * seq, MEM_WIDTH), BF16),
        compiler_params=_cparams(2),
        name="mem_prompt",
    )(z, kv, kv)


def _bf16_round(x):
    return x.astype(BF16).astype(F32)


def _mem_sample_kernel(q_ref, k_ref, v_ref, o_ref, *, bb):
    rows = []
    for b in range(bb):
        heads = []
        for h in range(MEM_HEADS):
            c0 = h * MEM_HEAD_DIM
            q = _bf16_round(q_ref[b:b + 1, c0:c0 + MEM_HEAD_DIM])
            k = _bf16_round(k_ref[b, :, h, :])
            v = _bf16_round(v_ref[b, :, h, :])
            logits = jnp.sum(k * q, axis=-1, keepdims=True) * (MEM_HEAD_DIM ** -0.5)
            p = jnp.exp(logits - jnp.max(logits, axis=0, keepdims=True))
            probs = _bf16_round(p * (1.0 / jnp.sum(p, axis=0, keepdims=True)))
            heads.append(jnp.sum(probs * v, axis=0, keepdims=True))
        rows.append(jnp.concatenate(heads, axis=1))
    o_ref[...] = jnp.concatenate(rows, axis=0)


def mem_sample(zs, mem_k, mem_v, layer, *, bb):
    nbatch = zs.shape[0]
    mblk = M_OFF // MEM_WIDTH
    kv_spec = pl.BlockSpec((None, bb, N_MEM, MEM_HEADS, MEM_HEAD_DIM), lambda i: (layer, i, 0, 0, 0))
    return pl.pallas_call(
        functools.partial(_mem_sample_kernel, bb=bb),
        grid=(nbatch // bb,),
        in_specs=[pl.BlockSpec((bb, MEM_WIDTH), lambda i: (i, mblk)), kv_spec, kv_spec],
        out_specs=pl.BlockSpec((bb, MEM_WIDTH), lambda i: (i, 0)),
        out_shape=jax.ShapeDtypeStruct((nbatch, MEM_WIDTH), F32),
        compiler_params=_cparams(1),
        name="mem_sample",
    )(zs, mem_k, mem_v)


def _to_row_major(x, o_ref, rows):
    for c in range(ROW_SUB):
        o_ref[pl.ds(c, rows, stride=ROW_SUB), :] = x[:, c * LANES:(c + 1) * LANES]


def _from_row_major(ref, first_row, rows, c, pitch=ROW_SUB):
    return ref[pl.ds(first_row + c, rows, stride=pitch), :]


def _norm_router_kernel(xp_ref, xs_ref, g_ref, r_ref, xn_ref, idx_ref, wgt_ref, *, tm, n_prompt_steps):
    x = jnp.where(pl.program_id(0) < n_prompt_steps, xp_ref[...], xs_ref[...])
    xn = _rms(x, g_ref[...])
    _to_row_major(xn, xn_ref, tm)
    logits = _dot(xn, r_ref[...])
    lane = lax.broadcasted_iota(jnp.int32, logits.shape, 1)
    logits = jnp.where(lane < N_EXPERTS, logits, -jnp.inf)
    v1 = jnp.max(logits, axis=-1, keepdims=True)
    i1 = jnp.min(jnp.where(logits == v1, lane, ROUTER_PAD), axis=-1, keepdims=True)
    rest = jnp.where(lane == i1, -jnp.inf, logits)
    v2 = jnp.max(rest, axis=-1, keepdims=True)
    i2 = jnp.min(jnp.where(rest == v2, lane, ROUTER_PAD), axis=-1, keepdims=True)
    e2 = jnp.exp(v2 - v1)
    den = 1.0 + e2
    idx_ref[...] = jnp.where(lane == 0, i1, jnp.where(lane == 1, i2, 0))
    wgt_ref[...] = jnp.where(lane == 0, 1.0 / den, jnp.where(lane == 1, e2 / den, 0.0))


def norm_router(xp, xs, g, router_pad, *, tm):
    m_prompt, d = xp.shape
    m = m_prompt + xs.shape[0]
    nps = m_prompt // tm
    return pl.pallas_call(
        functools.partial(_norm_router_kernel, tm=tm, n_prompt_steps=nps),
        grid=(m // tm,),
        in_specs=[pl.BlockSpec((tm, d), lambda i: (jnp.minimum(i, nps - 1), 0)),
                  pl.BlockSpec((tm, d), lambda i: (jnp.maximum(i - nps, 0), 0)),
                  pl.BlockSpec((1, d), lambda i: (0, 0)),
                  pl.BlockSpec((d, ROUTER_PAD), lambda i: (0, 0))],
        out_specs=[pl.BlockSpec((tm * ROW_SUB, LANES), lambda i: (i, 0)),
                   pl.BlockSpec((tm, ROUTER_PAD), lambda i: (i, 0)),
                   pl.BlockSpec((tm, ROUTER_PAD), lambda i: (i, 0))],
        out_shape=[jax.ShapeDtypeStruct((m * ROW_SUB, LANES), F32),
                   jax.ShapeDtypeStruct((m, ROUTER_PAD), jnp.int32),
                   jax.ShapeDtypeStruct((m, ROUTER_PAD), F32)],
        compiler_params=_cparams(1),
        name="norm_router",
    )(xp, xs, g, router_pad)


def _token_copy(src_hbm, token, buf, slot, sem):
    return pltpu.make_async_copy(src_hbm.at[pl.ds(token * ROW_SUB, ROW_SUB)],
                                 buf.at[pl.ds(slot * BUF_PITCH, ROW_SUB)], sem)


def _gather_tokens(src_hbm, tok_ref, buf, sem, count):
    def start(s, c):
        _token_copy(src_hbm, tok_ref[0, 0, s], buf, s, sem).start()
        return c

    def wait(s, c):
        _token_copy(src_hbm, tok_ref[0, 0, s], buf, s, sem).wait()
        return c

    lax.fori_loop(0, count, start, 0, unroll=GATHER_UNROLL)
    lax.fori_loop(0, count, wait, 0, unroll=GATHER_UNROLL)


def _dispatch_kernel(used_ref, tok_ref, src_hbm, o_ref, buf, sem, *, rows):
    @pl.when(pl.program_id(0) * rows < used_ref[0])
    def _():
        _gather_tokens(src_hbm, tok_ref, buf, sem, rows)
        for c in range(ROW_SUB):
            o_ref[:, c * LANES:(c + 1) * LANES] = _from_row_major(buf, 0, rows, c, BUF_PITCH).astype(BF16)

    @pl.when(pl.program_id(0) * rows >= used_ref[0])
    def _():
        o_ref[...] = jnp.zeros_like(o_ref)


def moe_dispatch(xn, src_token, used_rows, *, rows):
    n = src_token.shape[0]
    nchunk = n // rows
    return pl.pallas_call(
        functools.partial(_dispatch_kernel, rows=rows),
        grid_spec=pltpu.PrefetchScalarGridSpec(
            num_scalar_prefetch=1,
            grid=(nchunk,),
            in_specs=[pl.BlockSpec((1, 1, rows), lambda i, used: (i, 0, 0), memory_space=pltpu.SMEM),
                      pl.BlockSpec(memory_space=pl.ANY)],
            out_specs=pl.BlockSpec((rows, D_MODEL), lambda i, used: (i, 0)),
            scratch_shapes=[pltpu.VMEM((rows * BUF_PITCH, LANES), F32), pltpu.SemaphoreType.DMA(())]),
        out_shape=jax.ShapeDtypeStruct((n, D_MODEL), BF16),
        compiler_params=_cparams(1),
        name="moe_dispatch",
    )(used_rows, src_token.reshape(nchunk, 1, rows), xn)


STEP_IDLE, STEP_COMPUTE, STEP_ZERO = 0, 1, 2


def _moe_up_kernel(xt_ref, ot_ref, oj_ref, we_ref, wj_ref, cast_ref, mode_ref,
                   x_ref, w1_ref, w3_ref, w2_ref, o_ref, w2b_ref, w1b_ref, w3b_ref):
    s = pl.program_id(0)

    @pl.when(cast_ref[s] > 0)
    def _():
        w1b_ref[...] = w1_ref[0].astype(BF16)
        w3b_ref[...] = w3_ref[0].astype(BF16)
        w2b_ref[0] = w2_ref[0].astype(BF16)

    @pl.when(mode_ref[s] == STEP_COMPUTE)
    def _():
        x = x_ref[...]
        a = _dot(x, w1b_ref[...])
        b = _dot(x, w3b_ref[...])
        o_ref[...] = (jax.nn.silu(a) * b).astype(BF16)

    @pl.when(mode_ref[s] == STEP_ZERO)
    def _():
        o_ref[...] = jnp.zeros_like(o_ref)


def _moe_up_steps(tiles_per_expert, first_tile, *, n_tiles, nj):
    n_steps = (n_tiles + N_EXPERTS) * nj
    used = jnp.sum(tiles_per_expert)
    n_compute = used * nj
    n_zero = (n_tiles - used) * nj
    empty = tiles_per_expert == 0
    n_cast_only = jnp.sum(empty.astype(jnp.int32)) * nj
    n_live = n_compute + n_zero + n_cast_only
    a = jnp.minimum(jnp.arange(n_steps, dtype=jnp.int32), n_live - 1)
    blk_end = jnp.cumsum(tiles_per_expert * nj)
    e_c = jnp.minimum(jnp.sum((a[:, None] >= blk_end[None, :]).astype(jnp.int32), axis=1), N_EXPERTS - 1)
    onehot_c = (e_c[:, None] == jnp.arange(N_EXPERTS, dtype=jnp.int32)[None, :]).astype(jnp.int32)
    pick = lambda v: jnp.sum(onehot_c * v[None, :], axis=1)
    within = a - (pick(blk_end) - pick(tiles_per_expert) * nj)
    t_e = jnp.maximum(pick(tiles_per_expert), 1)
    j_c = within // t_e
    r_c = within % t_e
    tile_c = pick(first_tile) + r_c
    z = a - n_compute
    tile_z = used + z // nj
    j_z = z % nj
    b = a - n_compute - n_zero
    empty_rank = jnp.cumsum(empty.astype(jnp.int32)) - 1
    k = b // nj
    e_b = jnp.sum(jnp.where(empty[None, :] & (empty_rank[None, :] == k[:, None]),
                            jnp.arange(N_EXPERTS, dtype=jnp.int32)[None, :], 0), axis=1)
    j_b = b % nj
    is_c = a < n_compute
    is_z = (~is_c) & (a < n_compute + n_zero)
    is_b = (~is_c) & (~is_z)
    live = jnp.arange(n_steps, dtype=jnp.int32) < n_live
    last_tile = used - 1
    last_e = jnp.sum((last_tile >= jnp.cumsum(tiles_per_expert)).astype(jnp.int32))
    x_tile = jnp.where(is_c, tile_c, last_tile)
    o_tile = jnp.where(is_c, tile_c, jnp.where(is_z, tile_z, n_tiles - 1))
    o_col = jnp.where(is_c, j_c, jnp.where(is_z, j_z, nj - 1))
    w_e = jnp.where(is_c, e_c, jnp.where(is_b, e_b, last_e))
    w_col = jnp.where(is_c, j_c, jnp.where(is_b, j_b, nj - 1))
    cast = (live & ((is_c & (r_c == 0)) | is_b)).astype(jnp.int32)
    mode = jnp.where(live & is_c, STEP_COMPUTE, jnp.where(live & is_z, STEP_ZERO, STEP_IDLE))
    tables = (x_tile, o_tile, o_col, w_e, w_col, cast, mode)
    return tuple(t.astype(jnp.int32) for t in tables), n_steps


def moe_up(xs, w1, w3, w2, tiles_per_expert, first_tile, *, tm, tn):
    n, d = xs.shape
    f = w1.shape[2]
    nj = f // tn
    tables, n_steps = _moe_up_steps(tiles_per_expert, first_tile, n_tiles=n // tm, nj=nj)

    def x_map(s, xt, ot, oj, we, wj, cast, mode):
        return (xt[s], 0)

    def w_map(s, xt, ot, oj, we, wj, cast, mode):
        return (we[s], 0, wj[s])

    def w2_map(s, xt, ot, oj, we, wj, cast, mode):
        return (we[s], wj[s], 0)

    def o_map(s, xt, ot, oj, we, wj, cast, mode):
        return (ot[s], oj[s])

    return pl.pallas_call(
        _moe_up_kernel,
        grid_spec=pltpu.PrefetchScalarGridSpec(
            num_scalar_prefetch=len(tables),
            grid=(n_steps,),
            in_specs=[pl.BlockSpec((tm, d), x_map),
                      pl.BlockSpec((1, d, tn), w_map),
                      pl.BlockSpec((1, d, tn), w_map),
                      pl.BlockSpec((1, tn, d), w2_map)],
            out_specs=[pl.BlockSpec((tm, tn), o_map),
                       pl.BlockSpec((1, tn, d), w2_map)],
            scratch_shapes=[pltpu.VMEM((d, tn), BF16), pltpu.VMEM((d, tn), BF16)]),
        out_shape=[jax.ShapeDtypeStruct((n, f), BF16),
                   jax.ShapeDtypeStruct(w2.shape, BF16)],
        compiler_params=_cparams(1),
        name="moe_up",
    )(*tables, xs, w1, w3, w2)


def _weight_map(nj):
    def w_map(t, j, te, tv):
        return (te[t], 0, jnp.where(tv[t] > 0, j, nj - 1))

    return w_map


def _moe_down_kernel(te_ref, tv_ref, h_ref, w2_ref, o_ref, *, tm, tn):
    j = pl.program_id(1)
    groups = tn // LANES

    def put(val):
        for c in range(groups):
            o_ref[pl.ds(j * groups + c, tm, stride=ROW_SUB), :] = val[:, c * LANES:(c + 1) * LANES]

    @pl.when(tv_ref[pl.program_id(0)] > 0)
    def _():
        put(_dot(h_ref[...], w2_ref[0]))

    @pl.when(tv_ref[pl.program_id(0)] == 0)
    def _():
        put(jnp.zeros((tm, tn), F32))


def moe_down(h, w2, tiles, *, tm, tn):
    n, f = h.shape
    d = w2.shape[2]
    nj = d // tn
    return pl.pallas_call(
        functools.partial(_moe_down_kernel, tm=tm, tn=tn),
        grid_spec=pltpu.PrefetchScalarGridSpec(
            num_scalar_prefetch=2,
            grid=(n // tm, nj),
            in_specs=[pl.BlockSpec((tm, f), lambda t, j, te, tv: (t, 0)),
                      pl.BlockSpec((1, f, tn), _weight_map(nj))],
            out_specs=pl.BlockSpec((tm * ROW_SUB, LANES), lambda t, j, te, tv: (t, 0))),
        out_shape=jax.ShapeDtypeStruct((n * ROW_SUB, LANES), F32),
        compiler_params=_cparams(2),
        name="moe_down",
    )(*tiles, h, w2)


def _combine_kernel(pos_ref, xp_ref, xs_ref, w_ref, g_ref, ys_hbm, yp_ref, ysmp_ref, buf, sem,
                    *, rows, n_prompt_chunks):
    i = pl.program_id(0)
    _gather_tokens(ys_hbm, pos_ref, buf, sem, TOP_K * rows)
    w0 = w_ref[:, 0:1]
    w1 = w_ref[:, 1:2]

    def finish(x_ref, o_ref):
        for c in range(ROW_SUB):
            a = _from_row_major(buf, 0, rows, c, BUF_PITCH)
            b = _from_row_major(buf, rows * BUF_PITCH, rows, c, BUF_PITCH)
            o_ref[:, c * LANES:(c + 1) * LANES] = x_ref[:, c * LANES:(c + 1) * LANES] + (w0 * a + w1 * b)
        o_ref[...] = _rms(o_ref[...], g_ref[...])

    @pl.when(i < n_prompt_chunks)
    def _():
        finish(xp_ref, yp_ref)

    @pl.when(i >= n_prompt_chunks)
    def _():
        finish(xs_ref, ysmp_ref)


def moe_combine(xp, xs, wgt, g_final, ys, dest, *, rows):
    m_prompt, d = xp.shape
    m = m_prompt + xs.shape[0]
    nchunk = m // rows
    npc = m_prompt // rows
    pos = dest.reshape(nchunk, rows, TOP_K).transpose(0, 2, 1).reshape(nchunk, 1, TOP_K * rows)

    def p_map(i):
        return (jnp.minimum(i, npc - 1), 0)

    def s_map(i):
        return (jnp.maximum(i - npc, 0), 0)

    return pl.pallas_call(
        functools.partial(_combine_kernel, rows=rows, n_prompt_chunks=npc),
        grid=(nchunk,),
        in_specs=[pl.BlockSpec((1, 1, TOP_K * rows), lambda i: (i, 0, 0), memory_space=pltpu.SMEM),
                  pl.BlockSpec((rows, d), p_map),
                  pl.BlockSpec((rows, d), s_map),
                  pl.BlockSpec((rows, ROUTER_PAD), lambda i: (i, 0)),
                  pl.BlockSpec((1, d), lambda i: (0, 0)),
                  pl.BlockSpec(memory_space=pl.ANY)],
        out_specs=[pl.BlockSpec((rows, d), p_map), pl.BlockSpec((rows, d), s_map)],
        out_shape=[jax.ShapeDtypeStruct((m_prompt, d), F32),
                   jax.ShapeDtypeStruct((m - m_prompt, d), F32)],
        scratch_shapes=[pltpu.VMEM((TOP_K * rows * BUF_PITCH, LANES), F32), pltpu.SemaphoreType.DMA(())],
        compiler_params=_cparams(1),
        name="moe_combine",
    )(pos, xp, xs, wgt, g_final, ys)


def _routing_tables(idx2, *, tm, n_rows):
    e_flat = idx2.reshape(-1)
    onehot = (e_flat[:, None] == jnp.arange(N_EXPERTS, dtype=jnp.int32)[None, :]).astype(jnp.int32)
    csum = jnp.cumsum(onehot, axis=0)
    counts = csum[-1]
    rank = jnp.sum(csum * onehot, axis=1) - 1
    padded = ((counts + tm - 1) // tm) * tm
    ends = jnp.cumsum(padded)
    starts = ends - padded
    dest = (jnp.sum(onehot * starts[None, :], axis=1) + rank).astype(jnp.int32)
    token = jnp.arange(e_flat.shape[0], dtype=jnp.int32) // TOP_K
    src_token = jnp.zeros((n_rows,), jnp.int32).at[dest].set(token)
    tile_ids = jnp.arange(n_rows // tm, dtype=jnp.int32)
    used_tiles = ends[-1] // tm
    tile_valid = (tile_ids < used_tiles).astype(jnp.int32)
    tile_start = jnp.minimum(tile_ids, used_tiles - 1) * tm
    tile_expert = jnp.sum((tile_start[:, None] >= ends[None, :]).astype(jnp.int32), axis=1)
    per_expert = ((padded // tm).astype(jnp.int32), (starts // tm).astype(jnp.int32))
    return dest.reshape(-1, TOP_K), src_token, ends[-1:].astype(jnp.int32), (tile_expert, tile_valid), per_expert


def _rel_bias_by_distance(rel_bias_table):
    d = jnp.arange(WINDOW, dtype=jnp.int32)
    max_exact = REL_BUCKETS // 2
    d_f = jnp.maximum(d, 1).astype(F32)
    large = max_exact + (jnp.log(d_f / max_exact) / math.log(REL_MAX_DIST / max_exact)
                         * (REL_BUCKETS - max_exact)).astype(jnp.int32)
    large = jnp.minimum(large, REL_BUCKETS - 1)
    bucket = jnp.where(d < max_exact, d, large)
    return rel_bias_table[bucket].T


def _mixer_prompt(x, mem2d, wts, lw, layer, bias2, *, batch, seq):
    kv = norm_matmul(mem2d, lw["g_mem"], wts["w_mem_kv"], layer, tm=1024, tn=512)
    z = norm_matmul(x, lw["g_attn"], wts["w_in"], layer, tm=1024, tn=512, gate_from_col=G_OFF)
    ya = swa_prompt(z, bias2, lw["sinks"], batch=batch, seq=seq)
    yp = pool_prompt(z, lw["pool_maps"], lw["pool_scale"], batch=batch, seq=seq, tt=512)
    ym = mem_prompt(z, kv, batch=batch, seq=seq, tq=512)
    merged = merge_branches(ya, yp, ym, wts["w_up_attn"], wts["w_up_pool"], wts["w_up_mem"], layer, z,
                            tm=1024, tn=512)
    x = matmul_residual(merged, wts["w_o"], layer, x, tm=1024, tn=512)
    z3 = z.reshape(batch, seq, IN_COLS)
    new_k = z3[:, seq - WINDOW:, K_OFF:V_OFF].reshape(batch, WINDOW, N_KV_HEADS, HEAD_DIM)
    new_v = z3[:, seq - WINDOW:, V_OFF:P_OFF].reshape(batch, WINDOW, N_KV_HEADS, HEAD_DIM)
    new_pool = z3[:, seq - POOL_BUF:, P_OFF:M_OFF]
    kv3 = kv.reshape(batch, N_MEM, 2 * MEM_WIDTH)
    mk = kv3[..., :MEM_WIDTH].reshape(batch, N_MEM, MEM_HEADS, MEM_HEAD_DIM)
    mv = kv3[..., MEM_WIDTH:].reshape(batch, N_MEM, MEM_HEADS, MEM_HEAD_DIM)
    return x, new_k, new_v, new_pool, mk, mv


def _mixer_sample(x, cache_k, cache_v, state, mem_k, mem_v, wts, lw, layer, bias_s):
    nb = x.shape[0]
    z = norm_matmul(x, lw["g_attn"], wts["w_in"], layer, tm=nb, tn=512, gate_from_col=G_OFF)
    q3 = z[:, Q_OFF:K_OFF].reshape(nb, N_HEADS, HEAD_DIM)
    ya3, new_k, new_v = swa_sample(q3, z, cache_k.reshape(nb, WINDOW, KV_WIDTH),
                                   cache_v.reshape(nb, WINDOW, KV_WIDTH), bias_s, lw["sinks"][:, None], bb=8)
    yp, new_pool = pool_sample(z, state, lw["pool_maps"], lw["pool_scale"])
    ym = mem_sample(z, mem_k, mem_v, layer, bb=8)
    ya = ya3.reshape(nb, ATTN_WIDTH)
    merged = merge_branches(ya, yp, ym, wts["w_up_attn"], wts["w_up_pool"], wts["w_up_mem"], layer, z,
                            tm=nb, tn=512)
    x = matmul_residual(merged, wts["w_o"], layer, x, tm=nb, tn=512)
    return (x, new_k.reshape(nb, WINDOW, N_KV_HEADS, HEAD_DIM),
            new_v.reshape(nb, WINDOW, N_KV_HEADS, HEAD_DIM), new_pool)


def _dense_ffn(x, g, w1, w3, w2, layer, *, tm):
    h = norm_swiglu(x, g, w1, w3, layer, tm=tm, tn=512)
    return matmul_residual(h, w2, layer, x, tm=tm, tn=512)


def _moe_ffn_final(xp, xs, g, router, w1, w3, w2, g_final):
    m_prompt = xp.shape[0]
    m = m_prompt + xs.shape[0]
    router_pad = jnp.pad(router, ((0, 0), (0, ROUTER_PAD - N_EXPERTS)))
    xn, idx_pad, wgt = norm_router(xp, xs, g, router_pad, tm=COMBINE_ROWS)
    n_rows = pl.cdiv(TOP_K * m, MOE_TM) * MOE_TM + N_EXPERTS * MOE_TM
    dest, src_token, used_rows, tiles, per_expert = _routing_tables(idx_pad[:, :TOP_K], tm=MOE_TM, n_rows=n_rows)
    xsort = moe_dispatch(xn, src_token, used_rows, rows=DISPATCH_ROWS)
    h, w2b = moe_up(xsort, w1, w3, w2, *per_expert, tm=MOE_TM, tn=512)
    ys = moe_down(h, w2b, tiles, tm=MOE_TM, tn=512)
    return moe_combine(xp, xs, wgt, g_final, ys, dest, rows=COMBINE_ROWS)


def kernel(x_prompt, x_sample, mem_prompt, cache_swa_k, cache_swa_v, state_pool, cache_mem_k, cache_mem_v, rel_bias_table, g_attn, w_in, attn_sinks, pool_maps, pool_scale, g_mem, w_mem_kv, w_up_attn, w_up_pool, w_up_mem, w_o, g_ffn, ffn_w1, ffn_w3, ffn_w2, moe_router, moe_w1, moe_w3, moe_w2, g_final):
    batch, seq, d = x_prompt.shape
    nb_s = x_sample.shape[0]
    depth = w_in.shape[0]
    assert depth == 2 and x_sample.shape[1] == 1

    bias_hw = _rel_bias_by_distance(rel_bias_table).astype(F32)
    col = jnp.arange(2 * WINDOW)
    bias2 = bias_hw[:, jnp.clip(WINDOW - col, 0, WINDOW - 1)]
    bias_s = bias_hw[:, ::-1]

    wts = dict(w_in=w_in.astype(BF16), w_mem_kv=w_mem_kv.astype(BF16), w_up_attn=w_up_attn.astype(BF16),
               w_up_pool=w_up_pool.astype(BF16), w_up_mem=w_up_mem.astype(BF16), w_o=w_o.astype(BF16))
    ffn = (ffn_w1.astype(BF16), ffn_w3.astype(BF16), ffn_w2.astype(BF16))

    xp = x_prompt.reshape(batch * seq, d)
    xs = x_sample.reshape(nb_s, d)
    mem2d = mem_prompt.reshape(batch * N_MEM, d)
    p_k, p_v, p_pool, p_mk, p_mv, s_k, s_v, s_pool = [], [], [], [], [], [], [], []
    for l in range(depth):
        lw = dict(g_attn=g_attn[l][None, :], sinks=attn_sinks[l], pool_maps=pool_maps[l].astype(BF16),
                  pool_scale=pool_scale[l][None, :], g_mem=g_mem[l][None, :])
        xp, nk, nv, npool, mk, mv = _mixer_prompt(xp, mem2d, wts, lw, l, bias2, batch=batch, seq=seq)
        p_k.append(nk); p_v.append(nv); p_pool.append(npool); p_mk.append(mk); p_mv.append(mv)
        xs, nk, nv, npool = _mixer_sample(xs, cache_swa_k[l], cache_swa_v[l], state_pool[l],
                                          cache_mem_k, cache_mem_v, wts, lw, l, bias_s)
        s_k.append(nk); s_v.append(nv); s_pool.append(npool)
        gl = g_ffn[l][None, :]
        i = l // 2
        if l % 2 == 0:
            xp = _dense_ffn(xp, gl, *ffn, i, tm=1024)
            xs = _dense_ffn(xs, gl, *ffn, i, tm=nb_s)
        else:
            yp, ys = _moe_ffn_final(xp, xs, gl, moe_router[i], moe_w1[i], moe_w3[i], moe_w2[i], g_final[None, :])
    y_prompt = yp.reshape(batch, seq, d)
    y_sample = ys.reshape(nb_s, 1, d)
    return (y_prompt, y_sample,
            jnp.stack(p_k), jnp.stack(p_v), jnp.stack(p_pool), jnp.stack(p_mk), jnp.stack(p_mv),
            jnp.stack(s_k), jnp.stack(s_v), jnp.stack(s_pool))
```

```python
import functools
import math

import jax
import jax.numpy as jnp
from jax import lax
from jax.experimental import pallas as pl
from jax.experimental.pallas import tpu as pltpu

F32 = jnp.float32
BF16 = jnp.bfloat16

D_MODEL = 2048
N_HEADS = 16
N_KV_HEADS = 4
HEAD_DIM = 64
Q_GROUP = N_HEADS // N_KV_HEADS
WINDOW = 128
ATTN_WIDTH = N_HEADS * HEAD_DIM
KV_WIDTH = N_KV_HEADS * HEAD_DIM
REL_BUCKETS = 32
REL_MAX_DIST = 128
POOL_WINDOWS = (2, 4, 8, 16)
N_POOL_GROUPS = 4
POOL_CH = D_MODEL // 4
POOL_GROUP_CH = POOL_CH // N_POOL_GROUPS
POOL_BUF = max(POOL_WINDOWS) - 1
N_MEM = 256
MEM_HEADS = 4
MEM_HEAD_DIM = 128
MEM_WIDTH = MEM_HEADS * MEM_HEAD_DIM
N_BRANCHES = 3
Q_OFF = 0
K_OFF = Q_OFF + ATTN_WIDTH
V_OFF = K_OFF + KV_WIDTH
P_OFF = V_OFF + KV_WIDTH
M_OFF = P_OFF + POOL_CH
G_OFF = M_OFF + MEM_WIDTH
IN_COLS = G_OFF + N_BRANCHES * D_MODEL
N_EXPERTS = 8
TOP_K = 2
RMS_EPS = 1e-5
NEG_INF = -1e30
PAST_LEN = 8192

VMEM_LIMIT_BYTES = 56 * 1024 * 1024
LANES = 128
POOL_HALO = 16
ROUTER_PAD = LANES
ROW_SUB = D_MODEL // LANES
BUF_PITCH = ROW_SUB + 8
GATHER_UNROLL = 8
EPILOGUE_ROWS = 512
MOE_TM = 512
DISPATCH_ROWS = 256
COMBINE_ROWS = 128


def _cparams(n_axes):
    return pltpu.CompilerParams(dimension_semantics=("arbitrary",) * n_axes,
                                vmem_limit_bytes=VMEM_LIMIT_BYTES)


def _rms(x, g):
    return (x * lax.rsqrt(jnp.mean(x * x, axis=-1, keepdims=True) + RMS_EPS)) * g


_NN = (((1,), (0,)), ((), ()))
_NT = (((1,), (1,)), ((), ()))


def _dot(a, b, dims=_NN):
    return lax.dot_general(a.astype(BF16), b.astype(BF16), dims, preferred_element_type=F32)


def _row_chunks(rows):
    n = max(1, rows // EPILOGUE_ROWS)
    return [slice(k * (rows // n), (k + 1) * (rows // n)) for k in range(n)]


def _layer_cols(layer, rows, tn):
    return pl.BlockSpec((None, rows, tn), lambda i, j: (layer, 0, j))


def _norm_mm_kernel(x_ref, g_ref, w_ref, o_ref, xn_ref, *, gate_from):
    j = pl.program_id(1)

    @pl.when(j == 0)
    def _():
        xn_ref[...] = _rms(x_ref[...], g_ref[...]).astype(BF16)

    def emit(epilogue):
        for rs in _row_chunks(o_ref.shape[0]):
            o_ref[rs, :] = epilogue(_dot(xn_ref[rs, :], w_ref[...]))

    if gate_from is None:
        emit(lambda acc: acc)
    else:
        @pl.when(j < gate_from)
        def _():
            emit(lambda acc: acc)

        @pl.when(j >= gate_from)
        def _():
            emit(jax.nn.sigmoid)


def norm_matmul(x, g, w, layer, *, tm, tn, gate_from_col=None):
    m, d = x.shape
    n = w.shape[2]
    gate_from = None if gate_from_col is None else gate_from_col // tn
    return pl.pallas_call(
        functools.partial(_norm_mm_kernel, gate_from=gate_from),
        grid=(m // tm, n // tn),
        in_specs=[pl.BlockSpec((tm, d), lambda i, j: (i, 0)),
                  pl.BlockSpec((1, d), lambda i, j: (0, 0)),
                  _layer_cols(layer, d, tn)],
        out_specs=pl.BlockSpec((tm, tn), lambda i, j: (i, j)),
        out_shape=jax.ShapeDtypeStruct((m, n), F32),
        scratch_shapes=[pltpu.VMEM((tm, d), BF16)],
        compiler_params=_cparams(2),
        name="norm_matmul",
    )(x, g, w)


def _norm_swiglu_kernel(x_ref, g_ref, w1_ref, w3_ref, o_ref, xn_ref):
    @pl.when(pl.program_id(1) == 0)
    def _():
        xn_ref[...] = _rms(x_ref[...], g_ref[...]).astype(BF16)

    for rs in _row_chunks(o_ref.shape[0]):
        xn = xn_ref[rs, :]
        a = _dot(xn, w1_ref[...])
        b = _dot(xn, w3_ref[...])
        o_ref[rs, :] = (jax.nn.silu(a) * b).astype(BF16)


def norm_swiglu(x, g, w1, w3, layer, *, tm, tn):
    m, d = x.shape
    n = w1.shape[2]
    return pl.pallas_call(
        _norm_swiglu_kernel,
        grid=(m // tm, n // tn),
        in_specs=[pl.BlockSpec((tm, d), lambda i, j: (i, 0)),
                  pl.BlockSpec((1, d), lambda i, j: (0, 0)),
                  _layer_cols(layer, d, tn), _layer_cols(layer, d, tn)],
        out_specs=pl.BlockSpec((tm, tn), lambda i, j: (i, j)),
        out_shape=jax.ShapeDtypeStruct((m, n), BF16),
        scratch_shapes=[pltpu.VMEM((tm, d), BF16)],
        compiler_params=_cparams(2),
        name="norm_swiglu",
    )(x, g, w1, w3)


def _mm_res_kernel(a_ref, w_ref, r_ref, o_ref):
    for rs in _row_chunks(o_ref.shape[0]):
        o_ref[rs, :] = r_ref[rs, :] + _dot(a_ref[rs, :], w_ref[...])


def matmul_residual(a, w, layer, res, *, tm, tn):
    m, k = a.shape
    n = w.shape[2]
    return pl.pallas_call(
        _mm_res_kernel,
        grid=(m // tm, n // tn),
        in_specs=[pl.BlockSpec((tm, k), lambda i, j: (i, 0)),
                  _layer_cols(layer, k, tn),
                  pl.BlockSpec((tm, tn), lambda i, j: (i, j))],
        out_specs=pl.BlockSpec((tm, tn), lambda i, j: (i, j)),
        out_shape=jax.ShapeDtypeStruct((m, n), F32),
        compiler_params=_cparams(2),
        name="matmul_residual",
    )(a, w, res)


def _merge_kernel(ya_ref, yp_ref, ym_ref, wa_ref, wp_ref, wm_ref, g0_ref, g1_ref, g2_ref, o_ref):
    for rs in _row_chunks(o_ref.shape[0]):
        ua = _dot(ya_ref[rs, :], wa_ref[...])
        up = _dot(yp_ref[rs, :], wp_ref[...])
        um = _dot(ym_ref[rs, :], wm_ref[...])
        o_ref[rs, :] = (g0_ref[rs, :] * ua + g1_ref[rs, :] * up + g2_ref[rs, :] * um).astype(BF16)


def merge_branches(ya, yp, ym, wa, wp, wm, layer, z, *, tm, tn):
    m = ya.shape[0]
    nj = D_MODEL // tn
    g_blk = G_OFF // tn

    def gate_spec(b):
        return pl.BlockSpec((tm, tn), lambda i, j: (i, g_blk + b * nj + j))

    return pl.pallas_call(
        _merge_kernel,
        grid=(m // tm, nj),
        in_specs=[pl.BlockSpec((tm, ATTN_WIDTH), lambda i, j: (i, 0)),
                  pl.BlockSpec((tm, POOL_CH), lambda i, j: (i, 0)),
                  pl.BlockSpec((tm, MEM_WIDTH), lambda i, j: (i, 0)),
                  _layer_cols(layer, ATTN_WIDTH, tn), _layer_cols(layer, POOL_CH, tn),
                  _layer_cols(layer, MEM_WIDTH, tn),
                  gate_spec(0), gate_spec(1), gate_spec(2)],
        out_specs=pl.BlockSpec((tm, tn), lambda i, j: (i, j)),
        out_shape=jax.ShapeDtypeStruct((m, D_MODEL), BF16),
        compiler_params=_cparams(2),
        name="merge_branches",
    )(ya, yp, ym, wa, wp, wm, z, z, z)


def _swa_prompt_kernel(q_ref, kp_ref, kc_ref, vp_ref, vc_ref, base_ref, sink_ref, o_ref, bias_ref):
    i = pl.program_id(1)

    @pl.when((pl.program_id(0) == 0) & (i == 0))
    def _():
        for h in range(N_HEADS):
            row0 = jnp.broadcast_to(base_ref[h:h + 1, :], (WINDOW, 2 * WINDOW))
            bias_ref[h] = pltpu.roll(row0, 0, 1, stride=1, stride_axis=0)

    q = q_ref[...].astype(BF16)
    k2 = jnp.concatenate([kp_ref[...], kc_ref[...]], axis=0).astype(BF16)
    v2 = jnp.concatenate([vp_ref[...], vc_ref[...]], axis=0).astype(BF16)
    row = lax.broadcasted_iota(jnp.int32, (WINDOW, 2 * WINDOW), 0)
    col = lax.broadcasted_iota(jnp.int32, (WINDOW, 2 * WINDOW), 1)
    dist = row + WINDOW - col
    valid = (dist >= 0) & (dist < WINDOW) & ((i > 0) | (col >= WINDOW))
    for h in range(N_HEADS):
        kv = h // Q_GROUP
        qh = q[:, h * HEAD_DIM:(h + 1) * HEAD_DIM]
        kh = k2[:, kv * HEAD_DIM:(kv + 1) * HEAD_DIM]
        vh = v2[:, kv * HEAD_DIM:(kv + 1) * HEAD_DIM]
        logits = _dot(qh, kh, _NT) * (HEAD_DIM ** -0.5)
        logits = jnp.where(valid, logits + bias_ref[h], NEG_INF)
        s = sink_ref[h]
        m = jnp.maximum(jnp.max(logits, axis=-1, keepdims=True), s)
        p = jnp.exp(logits - m)
        denom = jnp.sum(p, axis=-1, keepdims=True) + jnp.exp(s - m)
        o = _dot(p * (1.0 / denom), vh)
        o_ref[:, h * HEAD_DIM:(h + 1) * HEAD_DIM] = o.astype(BF16)


def swa_prompt(z, bias_base, sinks, *, batch, seq):
    nb = seq // WINDOW
    kblk = K_OFF // KV_WIDTH
    vblk = V_OFF // KV_WIDTH

    def cur(c):
        return pl.BlockSpec((WINDOW, KV_WIDTH), lambda b, i: (b * nb + i, c))

    def prev(c):
        return pl.BlockSpec((WINDOW, KV_WIDTH), lambda b, i: (b * nb + jnp.maximum(i - 1, 0), c))

    return pl.pallas_call(
        _swa_prompt_kernel,
        grid=(batch, nb),
        in_specs=[pl.BlockSpec((WINDOW, ATTN_WIDTH), lambda b, i: (b * nb + i, 0)),
                  prev(kblk), cur(kblk), prev(vblk), cur(vblk),
                  pl.BlockSpec((N_HEADS, 2 * WINDOW), lambda b, i: (0, 0)),
                  pl.BlockSpec(memory_space=pltpu.SMEM)],
        out_specs=pl.BlockSpec((WINDOW, ATTN_WIDTH), lambda b, i: (b * nb + i, 0)),
        out_shape=jax.ShapeDtypeStruct((batch * seq, ATTN_WIDTH), BF16),
        scratch_shapes=[pltpu.VMEM((N_HEADS, WINDOW, 2 * WINDOW), F32)],
        compiler_params=_cparams(2),
        name="swa_prompt",
    )(z, z, z, z, z, bias_base, sinks)


def _swa_sample_kernel(q_ref, kn_ref, vn_ref, ck_ref, cv_ref, bias_ref, sink_ref,
                       o_ref, ok_ref, ov_ref, *, bb):
    row = lax.broadcasted_iota(jnp.int32, (WINDOW, KV_WIDTH), 0)
    hrow = lax.broadcasted_iota(jnp.int32, (N_HEADS, KV_WIDTH), 0)
    hcol = lax.broadcasted_iota(jnp.int32, (N_HEADS, KV_WIDTH), 1)
    own = (hcol // HEAD_DIM) == (hrow // Q_GROUP)
    s = sink_ref[...]
    for b in range(bb):
        newk = jnp.where(row == WINDOW - 1, kn_ref[b:b + 1, :], pltpu.roll(ck_ref[b], WINDOW - 1, 0))
        newv = jnp.where(row == WINDOW - 1, vn_ref[b:b + 1, :], pltpu.roll(cv_ref[b], WINDOW - 1, 0))
        ok_ref[b] = newk
        ov_ref[b] = newv
        qb = q_ref[b]
        qbd = jnp.where(own, jnp.concatenate([qb] * N_KV_HEADS, axis=1), 0.0)
        logits = _dot(qbd, newk, _NT) * (HEAD_DIM ** -0.5)
        logits = logits + bias_ref[...]
        m = jnp.maximum(jnp.max(logits, axis=-1, keepdims=True), s)
        p = jnp.exp(logits - m)
        denom = jnp.sum(p, axis=-1, keepdims=True) + jnp.exp(s - m)
        o = jnp.where(own, _dot(p * (1.0 / denom), newv), 0.0)
        acc = o[:, 0:HEAD_DIM]
        for c in range(1, N_KV_HEADS):
            acc = acc + o[:, c * HEAD_DIM:(c + 1) * HEAD_DIM]
        o_ref[b] = acc.astype(BF16)


def swa_sample(q3, zs, cache_k, cache_v, bias_s, sinks_col, *, bb):
    nbatch = q3.shape[0]
    kblk = K_OFF // KV_WIDTH
    vblk = V_OFF // KV_WIDTH
    cache_spec = pl.BlockSpec((bb, WINDOW, KV_WIDTH), lambda i: (i, 0, 0))
    return pl.pallas_call(
        functools.partial(_swa_sample_kernel, bb=bb),
        grid=(nbatch // bb,),
        in_specs=[pl.BlockSpec((bb, N_HEADS, HEAD_DIM), lambda i: (i, 0, 0)),
                  pl.BlockSpec((bb, KV_WIDTH), lambda i: (i, kblk)),
                  pl.BlockSpec((bb, KV_WIDTH), lambda i: (i, vblk)),
                  cache_spec, cache_spec,
                  pl.BlockSpec((N_HEADS, WINDOW), lambda i: (0, 0)),
                  pl.BlockSpec((N_HEADS, 1), lambda i: (0, 0))],
        out_specs=[pl.BlockSpec((bb, N_HEADS, HEAD_DIM), lambda i: (i, 0, 0)), cache_spec, cache_spec],
        out_shape=[jax.ShapeDtypeStruct((nbatch, N_HEADS, HEAD_DIM), BF16),
                   jax.ShapeDtypeStruct((nbatch, WINDOW, KV_WIDTH), F32),
                   jax.ShapeDtypeStruct((nbatch, WINDOW, KV_WIDTH), F32)],
        compiler_params=_cparams(1),
        name="swa_sample",
    )(q3, zs, zs, cache_k, cache_v, bias_s, sinks_col)


def _pool_mix(sums, p, cnts, maps_ref, scale_ref, o_ref):
    for g in range(N_POOL_GROUPS):
        c0 = g * POOL_GROUP_CH
        c1 = c0 + POOL_GROUP_CH
        diff = sums[g] / cnts[g] - p[:, c0:c1]
        mixed = _dot(diff, maps_ref[g])
        o_ref[:, c0:c1] = (mixed * scale_ref[:, c0:c1]).astype(BF16)


def _pool_prompt_kernel(p_ref, halo_ref, maps_ref, scale_ref, o_ref, *, tt):
    i = pl.program_id(1)
    p = p_ref[...]
    halo = jnp.where(i == 0, 0.0, halo_ref[...])
    ext = jnp.concatenate([halo, p], axis=0)
    s = ext
    sums = []
    shift = 1
    for g in range(N_POOL_GROUPS):
        s = s[:, (POOL_GROUP_CH if g > 0 else 0):]
        s = s + pltpu.roll(s, shift, 0)
        shift *= 2
        sums.append(s[POOL_HALO:, 0:POOL_GROUP_CH])
    pos = i * tt + lax.broadcasted_iota(jnp.int32, (tt, 1), 0)
    cnts = [jnp.minimum(pos + 1, w).astype(F32) for w in POOL_WINDOWS]
    _pool_mix(sums, p, cnts, maps_ref, scale_ref, o_ref)


def pool_prompt(z, maps, scale, *, batch, seq, tt):
    nt = seq // tt
    pblk = P_OFF // POOL_CH
    hb = tt // POOL_HALO
    return pl.pallas_call(
        functools.partial(_pool_prompt_kernel, tt=tt),
        grid=(batch, nt),
        in_specs=[pl.BlockSpec((tt, POOL_CH), lambda b, i: (b * nt + i, pblk)),
                  pl.BlockSpec((POOL_HALO, POOL_CH),
                               lambda b, i: (jnp.maximum((b * nt + i) * hb - 1, 0), pblk)),
                  pl.BlockSpec((N_POOL_GROUPS, POOL_GROUP_CH, POOL_GROUP_CH), lambda b, i: (0, 0, 0)),
                  pl.BlockSpec((1, POOL_CH), lambda b, i: (0, 0))],
        out_specs=pl.BlockSpec((tt, POOL_CH), lambda b, i: (b * nt + i, 0)),
        out_shape=jax.ShapeDtypeStruct((batch * seq, POOL_CH), BF16),
        compiler_params=_cparams(2),
        name="pool_prompt",
    )(z, z, maps, scale)


def _pool_sample_kernel(p_ref, st_ref, maps_ref, scale_ref, o_ref, ns_ref):
    p = p_ref[...]
    rows = [st_ref[:, r, :] for r in range(POOL_BUF)]
    sums = []
    for g, w in enumerate(POOL_WINDOWS):
        c0 = g * POOL_GROUP_CH
        c1 = c0 + POOL_GROUP_CH
        s = p[:, c0:c1]
        for r in range(POOL_BUF - (w - 1), POOL_BUF):
            s = s + rows[r][:, c0:c1]
        sums.append(s)
    cnts = [float(min(PAST_LEN + 1, w)) for w in POOL_WINDOWS]
    _pool_mix(sums, p, cnts, maps_ref, scale_ref, o_ref)
    for r in range(POOL_BUF - 1):
        ns_ref[:, r, :] = rows[r + 1]
    ns_ref[:, POOL_BUF - 1, :] = p


def pool_sample(zs, state, maps, scale):
    nbatch = zs.shape[0]
    pblk = P_OFF // POOL_CH
    st_spec = pl.BlockSpec((nbatch, POOL_BUF, POOL_CH), lambda i: (0, 0, 0))
    return pl.pallas_call(
        _pool_sample_kernel,
        grid=(1,),
        in_specs=[pl.BlockSpec((nbatch, POOL_CH), lambda i: (0, pblk)),
                  st_spec,
                  pl.BlockSpec((N_POOL_GROUPS, POOL_GROUP_CH, POOL_GROUP_CH), lambda i: (0, 0, 0)),
                  pl.BlockSpec((1, POOL_CH), lambda i: (0, 0))],
        out_specs=[pl.BlockSpec((nbatch, POOL_CH), lambda i: (0, 0)), st_spec],
        out_shape=[jax.ShapeDtypeStruct((nbatch, POOL_CH), BF16),
                   jax.ShapeDtypeStruct((nbatch, POOL_BUF, POOL_CH), F32)],
        compiler_params=_cparams(1),
        name="pool_sample",
    )(zs, state, maps, scale)


def _mem_prompt_kernel(q_ref, k_ref, v_ref, o_ref):
    q = q_ref[...].astype(BF16)
    k = k_ref[...].astype(BF16)
    v = v_ref[...].astype(BF16)
    for h in range(MEM_HEADS):
        c0 = h * MEM_HEAD_DIM
        c1 = c0 + MEM_HEAD_DIM
        logits = _dot(q[:, c0:c1], k[:, c0:c1], _NT) * (MEM_HEAD_DIM ** -0.5)
        p = jnp.exp(logits - jnp.max(logits, axis=-1, keepdims=True))
        probs = p * (1.0 / jnp.sum(p, axis=-1, keepdims=True))
        o_ref[:, c0:c1] = _dot(probs, v[:, c0:c1]).astype(BF16)


def mem_prompt(z, kv, *, batch, seq, tq):
    nt = seq // tq
    mblk = M_OFF // MEM_WIDTH
    return pl.pallas_call(
        _mem_prompt_kernel,
        grid=(batch, nt),
        in_specs=[pl.BlockSpec((tq, MEM_WIDTH), lambda b, i: (b * nt + i, mblk)),
                  pl.BlockSpec((N_MEM, MEM_WIDTH), lambda b, i: (b, 0)),
                  pl.BlockSpec((N_MEM, MEM_WIDTH), lambda b, i: (b, 1))],
        out_specs=pl.BlockSpec((tq, MEM_WIDTH), lambda b, i: (b * nt + i, 0)),
        out_shape=jax.ShapeDtypeStruct((batch * seq, MEM_WIDTH), BF16),
        compiler_params=_cparams(2),
        name="mem_prompt",
    )(z, kv, kv)


def _bf16_round(x):
    return x.astype(BF16).astype(F32)


def _mem_sample_kernel(q_ref, k_ref, v_ref, o_ref, *, bb):
    rows = []
    for b in range(bb):
        heads = []
        for h in range(MEM_HEADS):
            c0 = h * MEM_HEAD_DIM
            q = _bf16_round(q_ref[b:b + 1, c0:c0 + MEM_HEAD_DIM])
            k = _bf16_round(k_ref[b, :, h, :])
            v = _bf16_round(v_ref[b, :, h, :])
            logits = jnp.sum(k * q, axis=-1, keepdims=True) * (MEM_HEAD_DIM ** -0.5)
            p = jnp.exp(logits - jnp.max(logits, axis=0, keepdims=True))
            probs = _bf16_round(p * (1.0 / jnp.sum(p, axis=0, keepdims=True)))
            heads.append(jnp.sum(probs * v, axis=0, keepdims=True))
        rows.append(jnp.concatenate(heads, axis=1))
    o_ref[...] = jnp.concatenate(rows, axis=0)


def mem_sample(zs, mem_k, mem_v, layer, *, bb):
    nbatch = zs.shape[0]
    mblk = M_OFF // MEM_WIDTH
    kv_spec = pl.BlockSpec((None, bb, N_MEM, MEM_HEADS, MEM_HEAD_DIM), lambda i: (layer, i, 0, 0, 0))
    return pl.pallas_call(
        functools.partial(_mem_sample_kernel, bb=bb),
        grid=(nbatch // bb,),
        in_specs=[pl.BlockSpec((bb, MEM_WIDTH), lambda i: (i, mblk)), kv_spec, kv_spec],
        out_specs=pl.BlockSpec((bb, MEM_WIDTH), lambda i: (i, 0)),
        out_shape=jax.ShapeDtypeStruct((nbatch, MEM_WIDTH), F32),
        compiler_params=_cparams(1),
        name="mem_sample",
    )(zs, mem_k, mem_v)


def _to_row_major(x, o_ref, rows):
    for c in range(ROW_SUB):
        o_ref[pl.ds(c, rows, stride=ROW_SUB), :] = x[:, c * LANES:(c + 1) * LANES]


def _from_row_major(ref, first_row, rows, c, pitch=ROW_SUB):
    return ref[pl.ds(first_row + c, rows, stride=pitch), :]


def _norm_router_kernel(xp_ref, xs_ref, g_ref, r_ref, xn_ref, idx_ref, wgt_ref, *, tm, n_prompt_steps):
    x = jnp.where(pl.program_id(0) < n_prompt_steps, xp_ref[...], xs_ref[...])
    xn = _rms(x, g_ref[...])
    _to_row_major(xn, xn_ref, tm)
    logits = _dot(xn, r_ref[...])
    lane = lax.broadcasted_iota(jnp.int32, logits.shape, 1)
    logits = jnp.where(lane < N_EXPERTS, logits, -jnp.inf)
    v1 = jnp.max(logits, axis=-1, keepdims=True)
    i1 = jnp.min(jnp.where(logits == v1, lane, ROUTER_PAD), axis=-1, keepdims=True)
    rest = jnp.where(lane == i1, -jnp.inf, logits)
    v2 = jnp.max(rest, axis=-1, keepdims=True)
    i2 = jnp.min(jnp.where(rest == v2, lane, ROUTER_PAD), axis=-1, keepdims=True)
    e2 = jnp.exp(v2 - v1)
    den = 1.0 + e2
    idx_ref[...] = jnp.where(lane == 0, i1, jnp.where(lane == 1, i2, 0))
    wgt_ref[...] = jnp.where(lane == 0, 1.0 / den, jnp.where(lane == 1, e2 / den, 0.0))


def norm_router(xp, xs, g, router_pad, *, tm):
    m_prompt, d = xp.shape
    m = m_prompt + xs.shape[0]
    nps = m_prompt // tm
    return pl.pallas_call(
        functools.partial(_norm_router_kernel, tm=tm, n_prompt_steps=nps),
        grid=(m // tm,),
        in_specs=[pl.BlockSpec((tm, d), lambda i: (jnp.minimum(i, nps - 1), 0)),
                  pl.BlockSpec((tm, d), lambda i: (jnp.maximum(i - nps, 0), 0)),
                  pl.BlockSpec((1, d), lambda i: (0, 0)),
                  pl.BlockSpec((d, ROUTER_PAD), lambda i: (0, 0))],
        out_specs=[pl.BlockSpec((tm * ROW_SUB, LANES), lambda i: (i, 0)),
                   pl.BlockSpec((tm, ROUTER_PAD), lambda i: (i, 0)),
                   pl.BlockSpec((tm, ROUTER_PAD), lambda i: (i, 0))],
        out_shape=[jax.ShapeDtypeStruct((m * ROW_SUB, LANES), F32),
                   jax.ShapeDtypeStruct((m, ROUTER_PAD), jnp.int32),
                   jax.ShapeDtypeStruct((m, ROUTER_PAD), F32)],
        compiler_params=_cparams(1),
        name="norm_router",
    )(xp, xs, g, router_pad)


def _token_copy(src_hbm, token, buf, slot, sem):
    return pltpu.make_async_copy(src_hbm.at[pl.ds(token * ROW_SUB, ROW_SUB)],
                                 buf.at[pl.ds(slot * BUF_PITCH, ROW_SUB)], sem)


def _gather_loop(src_hbm, tok_ref, buf, sem, count, *, wait):
    def body(s, c):
        copy = _token_copy(src_hbm, tok_ref[0, 0, s], buf, s, sem)
        if wait:
            copy.wait()
        else:
            copy.start()
        return c

    lax.fori_loop(0, count, body, 0, unroll=GATHER_UNROLL)


def _gather_step(i, n_steps, src_hbm, tok_ref, next_tok_ref, bufs, sems, count, valid, next_valid, consume):
    for slot in range(2):
        @pl.when(i % 2 == slot)
        def _():
            @pl.when((i == 0) & valid)
            def _():
                _gather_loop(src_hbm, tok_ref, bufs.at[slot], sems.at[slot], count, wait=False)

            @pl.when((i + 1 < n_steps) & next_valid)
            def _():
                _gather_loop(src_hbm, next_tok_ref, bufs.at[1 - slot], sems.at[1 - slot], count, wait=False)

            @pl.when(valid)
            def _():
                _gather_loop(src_hbm, tok_ref, bufs.at[slot], sems.at[slot], count, wait=True)
                consume(bufs.at[slot])


def _gather_specs(nchunk, count):
    def cur(i, *_):
        return (i, 0, 0)

    def nxt(i, *_):
        return (jnp.minimum(i + 1, nchunk - 1), 0, 0)

    return [pl.BlockSpec((1, 1, count), cur, memory_space=pltpu.SMEM),
            pl.BlockSpec((1, 1, count), nxt, memory_space=pltpu.SMEM)]


def _gather_scratch(count):
    return [pltpu.VMEM((2, count * BUF_PITCH, LANES), F32), pltpu.SemaphoreType.DMA((2,))]


def _dispatch_kernel(used_ref, tok_ref, next_tok_ref, src_hbm, o_ref, bufs, sems, *, rows, n_steps):
    i = pl.program_id(0)
    valid = i * rows < used_ref[0]

    def consume(buf):
        for c in range(ROW_SUB):
            o_ref[:, c * LANES:(c + 1) * LANES] = _from_row_major(buf, 0, rows, c, BUF_PITCH).astype(BF16)

    _gather_step(i, n_steps, src_hbm, tok_ref, next_tok_ref, bufs, sems, rows,
                 valid, (i + 1) * rows < used_ref[0], consume)

    @pl.when(jnp.logical_not(valid))
    def _():
        o_ref[...] = jnp.zeros_like(o_ref)


def moe_dispatch(xn, src_token, used_rows, *, rows):
    n = src_token.shape[0]
    nchunk = n // rows
    tokens = src_token.reshape(nchunk, 1, rows)
    return pl.pallas_call(
        functools.partial(_dispatch_kernel, rows=rows, n_steps=nchunk),
        grid_spec=pltpu.PrefetchScalarGridSpec(
            num_scalar_prefetch=1,
            grid=(nchunk,),
            in_specs=_gather_specs(nchunk, rows) + [pl.BlockSpec(memory_space=pl.ANY)],
            out_specs=pl.BlockSpec((rows, D_MODEL), lambda i, used: (i, 0)),
            scratch_shapes=_gather_scratch(rows)),
        out_shape=jax.ShapeDtypeStruct((n, D_MODEL), BF16),
        compiler_params=_cparams(1),
        name="moe_dispatch",
    )(used_rows, tokens, tokens, xn)


STEP_IDLE, STEP_COMPUTE, STEP_ZERO = 0, 1, 2


def _moe_up_kernel(xt_ref, ot_ref, oj_ref, we_ref, wj_ref, cast_ref, mode_ref,
                   x_ref, w1_ref, w3_ref, w2_ref, o_ref, w2b_ref, w1b_ref, w3b_ref):
    s = pl.program_id(0)

    @pl.when(cast_ref[s] > 0)
    def _():
        w1b_ref[...] = w1_ref[0].astype(BF16)
        w3b_ref[...] = w3_ref[0].astype(BF16)
        w2b_ref[0] = w2_ref[0].astype(BF16)

    @pl.when(mode_ref[s] == STEP_COMPUTE)
    def _():
        x = x_ref[...]
        a = _dot(x, w1b_ref[...])
        b = _dot(x, w3b_ref[...])
        o_ref[...] = (jax.nn.silu(a) * b).astype(BF16)

    @pl.when(mode_ref[s] == STEP_ZERO)
    def _():
        o_ref[...] = jnp.zeros_like(o_ref)


def _moe_up_steps(tiles_per_expert, first_tile, *, n_tiles, nj):
    n_steps = (n_tiles + N_EXPERTS) * nj
    used = jnp.sum(tiles_per_expert)
    n_compute = used * nj
    n_zero = (n_tiles - used) * nj
    empty = tiles_per_expert == 0
    n_cast_only = jnp.sum(empty.astype(jnp.int32)) * nj
    n_live = n_compute + n_zero + n_cast_only
    a = jnp.minimum(jnp.arange(n_steps, dtype=jnp.int32), n_live - 1)
    blk_end = jnp.cumsum(tiles_per_expert * nj)
    e_c = jnp.minimum(jnp.sum((a[:, None] >= blk_end[None, :]).astype(jnp.int32), axis=1), N_EXPERTS - 1)
    onehot_c = (e_c[:, None] == jnp.arange(N_EXPERTS, dtype=jnp.int32)[None, :]).astype(jnp.int32)
    pick = lambda v: jnp.sum(onehot_c * v[None, :], axis=1)
    within = a - (pick(blk_end) - pick(tiles_per_expert) * nj)
    t_e = jnp.maximum(pick(tiles_per_expert), 1)
    j_c = within // t_e
    r_c = within % t_e
    tile_c = pick(first_tile) + r_c
    z = a - n_compute
    tile_z = used + z // nj
    j_z = z % nj
    b = a - n_compute - n_zero
    empty_rank = jnp.cumsum(empty.astype(jnp.int32)) - 1
    k = b // nj
    e_b = jnp.sum(jnp.where(empty[None, :] & (empty_rank[None, :] == k[:, None]),
                            jnp.arange(N_EXPERTS, dtype=jnp.int32)[None, :], 0), axis=1)
    j_b = b % nj
    is_c = a < n_compute
    is_z = (~is_c) & (a < n_compute + n_zero)
    is_b = (~is_c) & (~is_z)
    live = jnp.arange(n_steps, dtype=jnp.int32) < n_live
    last_tile = used - 1
    last_e = jnp.sum((last_tile >= jnp.cumsum(tiles_per_expert)).astype(jnp.int32))
    x_tile = jnp.where(is_c, tile_c, last_tile)
    o_tile = jnp.where(is_c, tile_c, jnp.where(is_z, tile_z, n_tiles - 1))
    o_col = jnp.where(is_c, j_c, jnp.where(is_z, j_z, nj - 1))
    w_e = jnp.where(is_c, e_c, jnp.where(is_b, e_b, last_e))
    w_col = jnp.where(is_c, j_c, jnp.where(is_b, j_b, nj - 1))
    cast = (live & ((is_c & (r_c == 0)) | is_b)).astype(jnp.int32)
    mode = jnp.where(live & is_c, STEP_COMPUTE, jnp.where(live & is_z, STEP_ZERO, STEP_IDLE))
    tables = (x_tile, o_tile, o_col, w_e, w_col, cast, mode)
    return tuple(t.astype(jnp.int32) for t in tables), n_steps


def moe_up(xs, w1, w3, w2, tiles_per_expert, first_tile, *, tm, tn):
    n, d = xs.shape
    f = w1.shape[2]
    nj = f // tn
    tables, n_steps = _moe_up_steps(tiles_per_expert, first_tile, n_tiles=n // tm, nj=nj)

    def x_map(s, xt, ot, oj, we, wj, cast, mode):
        return (xt[s], 0)

    def w_map(s, xt, ot, oj, we, wj, cast, mode):
        return (we[s], 0, wj[s])

    def w2_map(s, xt, ot, oj, we, wj, cast, mode):
        return (we[s], wj[s], 0)

    def o_map(s, xt, ot, oj, we, wj, cast, mode):
        return (ot[s], oj[s])

    return pl.pallas_call(
        _moe_up_kernel,
        grid_spec=pltpu.PrefetchScalarGridSpec(
            num_scalar_prefetch=len(tables),
            grid=(n_steps,),
            in_specs=[pl.BlockSpec((tm, d), x_map),
                      pl.BlockSpec((1, d, tn), w_map),
                      pl.BlockSpec((1, d, tn), w_map),
                      pl.BlockSpec((1, tn, d), w2_map)],
            out_specs=[pl.BlockSpec((tm, tn), o_map),
                       pl.BlockSpec((1, tn, d), w2_map)],
            scratch_shapes=[pltpu.VMEM((d, tn), BF16), pltpu.VMEM((d, tn), BF16)]),
        out_shape=[jax.ShapeDtypeStruct((n, f), BF16),
                   jax.ShapeDtypeStruct(w2.shape, BF16)],
        compiler_params=_cparams(1),
        name="moe_up",
    )(*tables, xs, w1, w3, w2)


def _weight_map(nj):
    def w_map(t, j, te, tv):
        return (te[t], 0, jnp.where(tv[t] > 0, j, nj - 1))

    return w_map


def _moe_down_kernel(te_ref, tv_ref, h_ref, w2_ref, o_ref, *, tm, tn):
    j = pl.program_id(1)
    groups = tn // LANES

    def put(val):
        for c in range(groups):
            o_ref[pl.ds(j * groups + c, tm, stride=ROW_SUB), :] = val[:, c * LANES:(c + 1) * LANES]

    @pl.when(tv_ref[pl.program_id(0)] > 0)
    def _():
        put(_dot(h_ref[...], w2_ref[0]))

    @pl.when(tv_ref[pl.program_id(0)] == 0)
    def _():
        put(jnp.zeros((tm, tn), F32))


def moe_down(h, w2, tiles, *, tm, tn):
    n, f = h.shape
    d = w2.shape[2]
    nj = d // tn
    return pl.pallas_call(
        functools.partial(_moe_down_kernel, tm=tm, tn=tn),
        grid_spec=pltpu.PrefetchScalarGridSpec(
            num_scalar_prefetch=2,
            grid=(n // tm, nj),
            in_specs=[pl.BlockSpec((tm, f), lambda t, j, te, tv: (t, 0)),
                      pl.BlockSpec((1, f, tn), _weight_map(nj))],
            out_specs=pl.BlockSpec((tm * ROW_SUB, LANES), lambda t, j, te, tv: (t, 0))),
        out_shape=jax.ShapeDtypeStruct((n * ROW_SUB, LANES), F32),
        compiler_params=_cparams(2),
        name="moe_down",
    )(*tiles, h, w2)


def _combine_kernel(pos_ref, next_pos_ref, xp_ref, xs_ref, w_ref, g_ref, ys_hbm, yp_ref, ysmp_ref, bufs, sems,
                    *, rows, n_prompt_chunks, n_steps):
    i = pl.program_id(0)
    w0 = w_ref[:, 0:1]
    w1 = w_ref[:, 1:2]

    def finish(buf, x_ref, o_ref):
        for c in range(ROW_SUB):
            a = _from_row_major(buf, 0, rows, c, BUF_PITCH)
            b = _from_row_major(buf, rows * BUF_PITCH, rows, c, BUF_PITCH)
            o_ref[:, c * LANES:(c + 1) * LANES] = x_ref[:, c * LANES:(c + 1) * LANES] + (w0 * a + w1 * b)
        o_ref[...] = _rms(o_ref[...], g_ref[...])

    def consume(buf):
        @pl.when(i < n_prompt_chunks)
        def _():
            finish(buf, xp_ref, yp_ref)

        @pl.when(i >= n_prompt_chunks)
        def _():
            finish(buf, xs_ref, ysmp_ref)

    every_step = i >= 0
    _gather_step(i, n_steps, ys_hbm, pos_ref, next_pos_ref, bufs, sems, TOP_K * rows,
                 every_step, every_step, consume)


def moe_combine(xp, xs, wgt, g_final, ys, dest, *, rows):
    m_prompt, d = xp.shape
    m = m_prompt + xs.shape[0]
    nchunk = m // rows
    npc = m_prompt // rows
    pos = dest.reshape(nchunk, rows, TOP_K).transpose(0, 2, 1).reshape(nchunk, 1, TOP_K * rows)

    def p_map(i):
        return (jnp.minimum(i, npc - 1), 0)

    def s_map(i):
        return (jnp.maximum(i - npc, 0), 0)

    return pl.pallas_call(
        functools.partial(_combine_kernel, rows=rows, n_prompt_chunks=npc, n_steps=nchunk),
        grid=(nchunk,),
        in_specs=_gather_specs(nchunk, TOP_K * rows) + [
                  pl.BlockSpec((rows, d), p_map),
                  pl.BlockSpec((rows, d), s_map),
                  pl.BlockSpec((rows, ROUTER_PAD), lambda i: (i, 0)),
                  pl.BlockSpec((1, d), lambda i: (0, 0)),
                  pl.BlockSpec(memory_space=pl.ANY)],
        out_specs=[pl.BlockSpec((rows, d), p_map), pl.BlockSpec((rows, d), s_map)],
        out_shape=[jax.ShapeDtypeStruct((m_prompt, d), F32),
                   jax.ShapeDtypeStruct((m - m_prompt, d), F32)],
        scratch_shapes=_gather_scratch(TOP_K * rows),
        compiler_params=_cparams(1),
        name="moe_combine",
    )(pos, pos, xp, xs, wgt, g_final, ys)


def _routing_tables(idx2, *, tm, n_rows):
    e_flat = idx2.reshape(-1)
    onehot = (e_flat[:, None] == jnp.arange(N_EXPERTS, dtype=jnp.int32)[None, :]).astype(jnp.int32)
    csum = jnp.cumsum(onehot, axis=0)
    counts = csum[-1]
    rank = jnp.sum(csum * onehot, axis=1) - 1
    padded = ((counts + tm - 1) // tm) * tm
    ends = jnp.cumsum(padded)
    starts = ends - padded
    dest = (jnp.sum(onehot * starts[None, :], axis=1) + rank).astype(jnp.int32)
    token = jnp.arange(e_flat.shape[0], dtype=jnp.int32) // TOP_K
    src_token = jnp.zeros((n_rows,), jnp.int32).at[dest].set(token)
    tile_ids = jnp.arange(n_rows // tm, dtype=jnp.int32)
    used_tiles = ends[-1] // tm
    tile_valid = (tile_ids < used_tiles).astype(jnp.int32)
    tile_start = jnp.minimum(tile_ids, used_tiles - 1) * tm
    tile_expert = jnp.sum((tile_start[:, None] >= ends[None, :]).astype(jnp.int32), axis=1)
    per_expert = ((padded // tm).astype(jnp.int32), (starts // tm).astype(jnp.int32))
    return dest.reshape(-1, TOP_K), src_token, ends[-1:].astype(jnp.int32), (tile_expert, tile_valid), per_expert


def _rel_bias_by_distance(rel_bias_table):
    d = jnp.arange(WINDOW, dtype=jnp.int32)
    max_exact = REL_BUCKETS // 2
    d_f = jnp.maximum(d, 1).astype(F32)
    large = max_exact + (jnp.log(d_f / max_exact) / math.log(REL_MAX_DIST / max_exact)
                         * (REL_BUCKETS - max_exact)).astype(jnp.int32)
    large = jnp.minimum(large, REL_BUCKETS - 1)
    bucket = jnp.where(d < max_exact, d, large)
    return rel_bias_table[bucket].T


def _mixer_prompt(x, mem2d, wts, lw, layer, bias2, *, batch, seq):
    kv = norm_matmul(mem2d, lw["g_mem"], wts["w_mem_kv"], layer, tm=1024, tn=512)
    z = norm_matmul(x, lw["g_attn"], wts["w_in"], layer, tm=1024, tn=512, gate_from_col=G_OFF)
    ya = swa_prompt(z, bias2, lw["sinks"], batch=batch, seq=seq)
    yp = pool_prompt(z, lw["pool_maps"], lw["pool_scale"], batch=batch, seq=seq, tt=512)
    ym = mem_prompt(z, kv, batch=batch, seq=seq, tq=512)
    merged = merge_branches(ya, yp, ym, wts["w_up_attn"], wts["w_up_pool"], wts["w_up_mem"], layer, z,
                            tm=1024, tn=512)
    x = matmul_residual(merged, wts["w_o"], layer, x, tm=1024, tn=512)
    z3 = z.reshape(batch, seq, IN_COLS)
    new_k = z3[:, seq - WINDOW:, K_OFF:V_OFF].reshape(batch, WINDOW, N_KV_HEADS, HEAD_DIM)
    new_v = z3[:, seq - WINDOW:, V_OFF:P_OFF].reshape(batch, WINDOW, N_KV_HEADS, HEAD_DIM)
    new_pool = z3[:, seq - POOL_BUF:, P_OFF:M_OFF]
    kv3 = kv.reshape(batch, N_MEM, 2 * MEM_WIDTH)
    mk = kv3[..., :MEM_WIDTH].reshape(batch, N_MEM, MEM_HEADS, MEM_HEAD_DIM)
    mv = kv3[..., MEM_WIDTH:].reshape(batch, N_MEM, MEM_HEADS, MEM_HEAD_DIM)
    return x, new_k, new_v, new_pool, mk, mv


def _mixer_sample(x, cache_k, cache_v, state, mem_k, mem_v, wts, lw, layer, bias_s):
    nb = x.shape[0]
    z = norm_matmul(x, lw["g_attn"], wts["w_in"], layer, tm=nb, tn=512, gate_from_col=G_OFF)
    q3 = z[:, Q_OFF:K_OFF].reshape(nb, N_HEADS, HEAD_DIM)
    ya3, new_k, new_v = swa_sample(q3, z, cache_k.reshape(nb, WINDOW, KV_WIDTH),
                                   cache_v.reshape(nb, WINDOW, KV_WIDTH), bias_s, lw["sinks"][:, None], bb=8)
    yp, new_pool = pool_sample(z, state, lw["pool_maps"], lw["pool_scale"])
    ym = mem_sample(z, mem_k, mem_v, layer, bb=8)
    ya = ya3.reshape(nb, ATTN_WIDTH)
    merged = merge_branches(ya, yp, ym, wts["w_up_attn"], wts["w_up_pool"], wts["w_up_mem"], layer, z,
                            tm=nb, tn=512)
    x = matmul_residual(merged, wts["w_o"], layer, x, tm=nb, tn=512)
    return (x, new_k.reshape(nb, WINDOW, N_KV_HEADS, HEAD_DIM),
            new_v.reshape(nb, WINDOW, N_KV_HEADS, HEAD_DIM), new_pool)


def _dense_ffn(x, g, w1, w3, w2, layer, *, tm):
    h = norm_swiglu(x, g, w1, w3, layer, tm=tm, tn=512)
    return matmul_residual(h, w2, layer, x, tm=tm, tn=512)


def _moe_ffn_final(xp, xs, g, router, w1, w3, w2, g_final):
    m_prompt = xp.shape[0]
    m = m_prompt + xs.shape[0]
    router_pad = jnp.pad(router, ((0, 0), (0, ROUTER_PAD - N_EXPERTS)))
    xn, idx_pad, wgt = norm_router(xp, xs, g, router_pad, tm=COMBINE_ROWS)
    n_rows = pl.cdiv(TOP_K * m, MOE_TM) * MOE_TM + N_EXPERTS * MOE_TM
    dest, src_token, used_rows, tiles, per_expert = _routing_tables(idx_pad[:, :TOP_K], tm=MOE_TM, n_rows=n_rows)
    xsort = moe_dispatch(xn, src_token, used_rows, rows=DISPATCH_ROWS)
    h, w2b = moe_up(xsort, w1, w3, w2, *per_expert, tm=MOE_TM, tn=512)
    ys = moe_down(h, w2b, tiles, tm=MOE_TM, tn=512)
    return moe_combine(xp, xs, wgt, g_final, ys, dest, rows=COMBINE_ROWS)


def kernel(x_prompt, x_sample, mem_prompt, cache_swa_k, cache_swa_v, state_pool, cache_mem_k, cache_mem_v, rel_bias_table, g_attn, w_in, attn_sinks, pool_maps, pool_scale, g_mem, w_mem_kv, w_up_attn, w_up_pool, w_up_mem, w_o, g_ffn, ffn_w1, ffn_w3, ffn_w2, moe_router, moe_w1, moe_w3, moe_w2, g_final):
    batch, seq, d = x_prompt.shape
    nb_s = x_sample.shape[0]
    depth = w_in.shape[0]
    assert depth == 2 and x_sample.shape[1] == 1

    bias_hw = _rel_bias_by_distance(rel_bias_table).astype(F32)
    col = jnp.arange(2 * WINDOW)
    bias2 = bias_hw[:, jnp.clip(WINDOW - col, 0, WINDOW - 1)]
    bias_s = bias_hw[:, ::-1]

    wts = dict(w_in=w_in.astype(BF16), w_mem_kv=w_mem_kv.astype(BF16), w_up_attn=w_up_attn.astype(BF16),
               w_up_pool=w_up_pool.astype(BF16), w_up_mem=w_up_mem.astype(BF16), w_o=w_o.astype(BF16))
    ffn = (ffn_w1.astype(BF16), ffn_w3.astype(BF16), ffn_w2.astype(BF16))

    xp = x_prompt.reshape(batch * seq, d)
    xs = x_sample.reshape(nb_s, d)
    mem2d = mem_prompt.reshape(batch * N_MEM, d)
    p_k, p_v, p_pool, p_mk, p_mv, s_k, s_v, s_pool = [], [], [], [], [], [], [], []
    for l in range(depth):
        lw = dict(g_attn=g_attn[l][None, :], sinks=attn_sinks[l], pool_maps=pool_maps[l].astype(BF16),
                  pool_scale=pool_scale[l][None, :], g_mem=g_mem[l][None, :])
        xp, nk, nv, npool, mk, mv = _mixer_prompt(xp, mem2d, wts, lw, l, bias2, batch=batch, seq=seq)
        p_k.append(nk); p_v.append(nv); p_pool.append(npool); p_mk.append(mk); p_mv.append(mv)
        xs, nk, nv, npool = _mixer_sample(xs, cache_swa_k[l], cache_swa_v[l], state_pool[l],
                                          cache_mem_k, cache_mem_v, wts, lw, l, bias_s)
        s_k.append(nk); s_v.append(nv); s_pool.append(npool)
        gl = g_ffn[l][None, :]
        i = l // 2
        if l % 2 == 0:
            xp = _dense_ffn(xp, gl, *ffn, i, tm=1024)
            xs = _dense_ffn(xs, gl, *ffn, i, tm=nb_s)
        else:
            yp, ys = _moe_ffn_final(xp, xs, gl, moe_router[i], moe_w1[i], moe_w3[i], moe_w2[i], g_final[None, :])
    y_prompt = yp.reshape(batch, seq, d)
    y_sample = ys.reshape(nb_s, 1, d)
    return (y_prompt, y_sample,
            jnp.stack(p_k), jnp.stack(p_v), jnp.stack(p_pool), jnp.stack(p_mk), jnp.stack(p_mv),
            jnp.stack(s_k), jnp.stack(s_v), jnp.stack(s_pool))
```

```python
import functools
import math

import jax
import jax.numpy as jnp
from jax import lax
from jax.experimental import pallas as pl
from jax.experimental.pallas import tpu as pltpu

F32 = jnp.float32
BF16 = jnp.bfloat16

D_MODEL = 2048
N_HEADS = 16
N_KV_HEADS = 4
HEAD_DIM = 64
Q_GROUP = N_HEADS // N_KV_HEADS
WINDOW = 128
ATTN_WIDTH = N_HEADS * HEAD_DIM
KV_WIDTH = N_KV_HEADS * HEAD_DIM
REL_BUCKETS = 32
REL_MAX_DIST = 128
POOL_WINDOWS = (2, 4, 8, 16)
N_POOL_GROUPS = 4
POOL_CH = D_MODEL // 4
POOL_GROUP_CH = POOL_CH // N_POOL_GROUPS
POOL_BUF = max(POOL_WINDOWS) - 1
N_MEM = 256
MEM_HEADS = 4
MEM_HEAD_DIM = 128
MEM_WIDTH = MEM_HEADS * MEM_HEAD_DIM
N_BRANCHES = 3
Q_OFF = 0
K_OFF = Q_OFF + ATTN_WIDTH
V_OFF = K_OFF + KV_WIDTH
P_OFF = V_OFF + KV_WIDTH
M_OFF = P_OFF + POOL_CH
G_OFF = M_OFF + MEM_WIDTH
IN_COLS = G_OFF + N_BRANCHES * D_MODEL
N_EXPERTS = 8
TOP_K = 2
RMS_EPS = 1e-5
NEG_INF = -1e30
PAST_LEN = 8192

VMEM_LIMIT_BYTES = 56 * 1024 * 1024
LANES = 128
POOL_HALO = 16
ROUTER_PAD = LANES
ROW_SUB = D_MODEL // LANES
BUF_PITCH = ROW_SUB + 8
GATHER_UNROLL = 8
EPILOGUE_ROWS = 512
W2_SLAB_ROWS = 64
MOE_TM = 512
DISPATCH_ROWS = 256
COMBINE_ROWS = 128


def _cparams(n_axes):
    return pltpu.CompilerParams(dimension_semantics=("arbitrary",) * n_axes,
                                vmem_limit_bytes=VMEM_LIMIT_BYTES)


def _rms(x, g):
    return (x * lax.rsqrt(jnp.mean(x * x, axis=-1, keepdims=True) + RMS_EPS)) * g


_NN = (((1,), (0,)), ((), ()))
_NT = (((1,), (1,)), ((), ()))


def _dot(a, b, dims=_NN):
    return lax.dot_general(a.astype(BF16), b.astype(BF16), dims, preferred_element_type=F32)


def _row_chunks(rows):
    n = max(1, rows // EPILOGUE_ROWS)
    return [slice(k * (rows // n), (k + 1) * (rows // n)) for k in range(n)]


def _layer_cols(layer, rows, tn):
    return pl.BlockSpec((None, rows, tn), lambda i, j: (layer, 0, j))


def _norm_mm_kernel(x_ref, g_ref, w_ref, o_ref, xn_ref, *, gate_from):
    j = pl.program_id(1)

    @pl.when(j == 0)
    def _():
        xn_ref[...] = _rms(x_ref[...], g_ref[...]).astype(BF16)

    def emit(epilogue):
        for rs in _row_chunks(o_ref.shape[0]):
            o_ref[rs, :] = epilogue(_dot(xn_ref[rs, :], w_ref[...]))

    if gate_from is None:
        emit(lambda acc: acc)
    else:
        @pl.when(j < gate_from)
        def _():
            emit(lambda acc: acc)

        @pl.when(j >= gate_from)
        def _():
            emit(jax.nn.sigmoid)


def norm_matmul(x, g, w, layer, *, tm, tn, gate_from_col=None):
    m, d = x.shape
    n = w.shape[2]
    gate_from = None if gate_from_col is None else gate_from_col // tn
    return pl.pallas_call(
        functools.partial(_norm_mm_kernel, gate_from=gate_from),
        grid=(m // tm, n // tn),
        in_specs=[pl.BlockSpec((tm, d), lambda i, j: (i, 0)),
                  pl.BlockSpec((1, d), lambda i, j: (0, 0)),
                  _layer_cols(layer, d, tn)],
        out_specs=pl.BlockSpec((tm, tn), lambda i, j: (i, j)),
        out_shape=jax.ShapeDtypeStruct((m, n), F32),
        scratch_shapes=[pltpu.VMEM((tm, d), BF16)],
        compiler_params=_cparams(2),
        name="norm_matmul",
    )(x, g, w)


def _norm_swiglu_kernel(x_ref, g_ref, w1_ref, w3_ref, o_ref, xn_ref):
    @pl.when(pl.program_id(1) == 0)
    def _():
        xn_ref[...] = _rms(x_ref[...], g_ref[...]).astype(BF16)

    for rs in _row_chunks(o_ref.shape[0]):
        xn = xn_ref[rs, :]
        a = _dot(xn, w1_ref[...])
        b = _dot(xn, w3_ref[...])
        o_ref[rs, :] = (jax.nn.silu(a) * b).astype(BF16)


def norm_swiglu(x, g, w1, w3, layer, *, tm, tn):
    m, d = x.shape
    n = w1.shape[2]
    return pl.pallas_call(
        _norm_swiglu_kernel,
        grid=(m // tm, n // tn),
        in_specs=[pl.BlockSpec((tm, d), lambda i, j: (i, 0)),
                  pl.BlockSpec((1, d), lambda i, j: (0, 0)),
                  _layer_cols(layer, d, tn), _layer_cols(layer, d, tn)],
        out_specs=pl.BlockSpec((tm, tn), lambda i, j: (i, j)),
        out_shape=jax.ShapeDtypeStruct((m, n), BF16),
        scratch_shapes=[pltpu.VMEM((tm, d), BF16)],
        compiler_params=_cparams(2),
        name="norm_swiglu",
    )(x, g, w1, w3)


def _mm_res_kernel(a_ref, w_ref, r_ref, o_ref):
    for rs in _row_chunks(o_ref.shape[0]):
        o_ref[rs, :] = r_ref[rs, :] + _dot(a_ref[rs, :], w_ref[...])


def matmul_residual(a, w, layer, res, *, tm, tn):
    m, k = a.shape
    n = w.shape[2]
    return pl.pallas_call(
        _mm_res_kernel,
        grid=(m // tm, n // tn),
        in_specs=[pl.BlockSpec((tm, k), lambda i, j: (i, 0)),
                  _layer_cols(layer, k, tn),
                  pl.BlockSpec((tm, tn), lambda i, j: (i, j))],
        out_specs=pl.BlockSpec((tm, tn), lambda i, j: (i, j)),
        out_shape=jax.ShapeDtypeStruct((m, n), F32),
        compiler_params=_cparams(2),
        name="matmul_residual",
    )(a, w, res)


def _merge_kernel(ya_ref, yp_ref, ym_ref, wa_ref, wp_ref, wm_ref, g0_ref, g1_ref, g2_ref, o_ref):
    for rs in _row_chunks(o_ref.shape[0]):
        ua = _dot(ya_ref[rs, :], wa_ref[...])
        up = _dot(yp_ref[rs, :], wp_ref[...])
        um = _dot(ym_ref[rs, :], wm_ref[...])
        o_ref[rs, :] = (g0_ref[rs, :] * ua + g1_ref[rs, :] * up + g2_ref[rs, :] * um).astype(BF16)


def merge_branches(ya, yp, ym, wa, wp, wm, layer, z, *, tm, tn):
    m = ya.shape[0]
    nj = D_MODEL // tn
    g_blk = G_OFF // tn

    def gate_spec(b):
        return pl.BlockSpec((tm, tn), lambda i, j: (i, g_blk + b * nj + j))

    return pl.pallas_call(
        _merge_kernel,
        grid=(m // tm, nj),
        in_specs=[pl.BlockSpec((tm, ATTN_WIDTH), lambda i, j: (i, 0)),
                  pl.BlockSpec((tm, POOL_CH), lambda i, j: (i, 0)),
                  pl.BlockSpec((tm, MEM_WIDTH), lambda i, j: (i, 0)),
                  _layer_cols(layer, ATTN_WIDTH, tn), _layer_cols(layer, POOL_CH, tn),
                  _layer_cols(layer, MEM_WIDTH, tn),
                  gate_spec(0), gate_spec(1), gate_spec(2)],
        out_specs=pl.BlockSpec((tm, tn), lambda i, j: (i, j)),
        out_shape=jax.ShapeDtypeStruct((m, D_MODEL), BF16),
        compiler_params=_cparams(2),
        name="merge_branches",
    )(ya, yp, ym, wa, wp, wm, z, z, z)


def _swa_prompt_kernel(q_ref, kp_ref, kc_ref, vp_ref, vc_ref, base_ref, sink_ref, o_ref, bias_ref):
    i = pl.program_id(1)

    @pl.when((pl.program_id(0) == 0) & (i == 0))
    def _():
        for h in range(N_HEADS):
            row0 = jnp.broadcast_to(base_ref[h:h + 1, :], (WINDOW, 2 * WINDOW))
            bias_ref[h] = pltpu.roll(row0, 0, 1, stride=1, stride_axis=0)

    q = q_ref[...].astype(BF16)
    k2 = jnp.concatenate([kp_ref[...], kc_ref[...]], axis=0).astype(BF16)
    v2 = jnp.concatenate([vp_ref[...], vc_ref[...]], axis=0).astype(BF16)
    row = lax.broadcasted_iota(jnp.int32, (WINDOW, 2 * WINDOW), 0)
    col = lax.broadcasted_iota(jnp.int32, (WINDOW, 2 * WINDOW), 1)
    dist = row + WINDOW - col
    valid = (dist >= 0) & (dist < WINDOW) & ((i > 0) | (col >= WINDOW))
    for h in range(N_HEADS):
        kv = h // Q_GROUP
        qh = q[:, h * HEAD_DIM:(h + 1) * HEAD_DIM]
        kh = k2[:, kv * HEAD_DIM:(kv + 1) * HEAD_DIM]
        vh = v2[:, kv * HEAD_DIM:(kv + 1) * HEAD_DIM]
        logits = _dot(qh, kh, _NT) * (HEAD_DIM ** -0.5)
        logits = jnp.where(valid, logits + bias_ref[h], NEG_INF)
        s = sink_ref[h]
        m = jnp.maximum(jnp.max(logits, axis=-1, keepdims=True), s)
        p = jnp.exp(logits - m)
        denom = jnp.sum(p, axis=-1, keepdims=True) + jnp.exp(s - m)
        o = _dot(p * (1.0 / denom), vh)
        o_ref[:, h * HEAD_DIM:(h + 1) * HEAD_DIM] = o.astype(BF16)


def swa_prompt(z, bias_base, sinks, *, batch, seq):
    nb = seq // WINDOW
    kblk = K_OFF // KV_WIDTH
    vblk = V_OFF // KV_WIDTH

    def cur(c):
        return pl.BlockSpec((WINDOW, KV_WIDTH), lambda b, i: (b * nb + i, c))

    def prev(c):
        return pl.BlockSpec((WINDOW, KV_WIDTH), lambda b, i: (b * nb + jnp.maximum(i - 1, 0), c))

    return pl.pallas_call(
        _swa_prompt_kernel,
        grid=(batch, nb),
        in_specs=[pl.BlockSpec((WINDOW, ATTN_WIDTH), lambda b, i: (b * nb + i, 0)),
                  prev(kblk), cur(kblk), prev(vblk), cur(vblk),
                  pl.BlockSpec((N_HEADS, 2 * WINDOW), lambda b, i: (0, 0)),
                  pl.BlockSpec(memory_space=pltpu.SMEM)],
        out_specs=pl.BlockSpec((WINDOW, ATTN_WIDTH), lambda b, i: (b * nb + i, 0)),
        out_shape=jax.ShapeDtypeStruct((batch * seq, ATTN_WIDTH), BF16),
        scratch_shapes=[pltpu.VMEM((N_HEADS, WINDOW, 2 * WINDOW), F32)],
        compiler_params=_cparams(2),
        name="swa_prompt",
    )(z, z, z, z, z, bias_base, sinks)


def _swa_sample_kernel(q_ref, kn_ref, vn_ref, ck_ref, cv_ref, bias_ref, sink_ref,
                       o_ref, ok_ref, ov_ref, *, bb):
    row = lax.broadcasted_iota(jnp.int32, (WINDOW, KV_WIDTH), 0)
    hrow = lax.broadcasted_iota(jnp.int32, (N_HEADS, KV_WIDTH), 0)
    hcol = lax.broadcasted_iota(jnp.int32, (N_HEADS, KV_WIDTH), 1)
    own = (hcol // HEAD_DIM) == (hrow // Q_GROUP)
    s = sink_ref[...]
    for b in range(bb):
        newk = jnp.where(row == WINDOW - 1, kn_ref[b:b + 1, :], pltpu.roll(ck_ref[b], WINDOW - 1, 0))
        newv = jnp.where(row == WINDOW - 1, vn_ref[b:b + 1, :], pltpu.roll(cv_ref[b], WINDOW - 1, 0))
        ok_ref[b] = newk
        ov_ref[b] = newv
        qb = q_ref[b]
        qbd = jnp.where(own, jnp.concatenate([qb] * N_KV_HEADS, axis=1), 0.0)
        logits = _dot(qbd, newk, _NT) * (HEAD_DIM ** -0.5)
        logits = logits + bias_ref[...]
        m = jnp.maximum(jnp.max(logits, axis=-1, keepdims=True), s)
        p = jnp.exp(logits - m)
        denom = jnp.sum(p, axis=-1, keepdims=True) + jnp.exp(s - m)
        o = jnp.where(own, _dot(p * (1.0 / denom), newv), 0.0)
        acc = o[:, 0:HEAD_DIM]
        for c in range(1, N_KV_HEADS):
            acc = acc + o[:, c * HEAD_DIM:(c + 1) * HEAD_DIM]
        o_ref[b] = acc.astype(BF16)


def swa_sample(q3, zs, cache_k, cache_v, bias_s, sinks_col, *, bb):
    nbatch = q3.shape[0]
    kblk = K_OFF // KV_WIDTH
    vblk = V_OFF // KV_WIDTH
    cache_spec = pl.BlockSpec((bb, WINDOW, KV_WIDTH), lambda i: (i, 0, 0))
    return pl.pallas_call(
        functools.partial(_swa_sample_kernel, bb=bb),
        grid=(nbatch // bb,),
        in_specs=[pl.BlockSpec((bb, N_HEADS, HEAD_DIM), lambda i: (i, 0, 0)),
                  pl.BlockSpec((bb, KV_WIDTH), lambda i: (i, kblk)),
                  pl.BlockSpec((bb, KV_WIDTH), lambda i: (i, vblk)),
                  cache_spec, cache_spec,
                  pl.BlockSpec((N_HEADS, WINDOW), lambda i: (0, 0)),
                  pl.BlockSpec((N_HEADS, 1), lambda i: (0, 0))],
        out_specs=[pl.BlockSpec((bb, N_HEADS, HEAD_DIM), lambda i: (i, 0, 0)), cache_spec, cache_spec],
        out_shape=[jax.ShapeDtypeStruct((nbatch, N_HEADS, HEAD_DIM), BF16),
                   jax.ShapeDtypeStruct((nbatch, WINDOW, KV_WIDTH), F32),
                   jax.ShapeDtypeStruct((nbatch, WINDOW, KV_WIDTH), F32)],
        compiler_params=_cparams(1),
        name="swa_sample",
    )(q3, zs, zs, cache_k, cache_v, bias_s, sinks_col)


def _pool_mix(sums, p, cnts, maps_ref, scale_ref, o_ref):
    for g in range(N_POOL_GROUPS):
        c0 = g * POOL_GROUP_CH
        c1 = c0 + POOL_GROUP_CH
        diff = sums[g] / cnts[g] - p[:, c0:c1]
        mixed = _dot(diff, maps_ref[g])
        o_ref[:, c0:c1] = (mixed * scale_ref[:, c0:c1]).astype(BF16)


def _pool_prompt_kernel(p_ref, halo_ref, maps_ref, scale_ref, o_ref, *, tt):
    i = pl.program_id(1)
    p = p_ref[...]
    halo = jnp.where(i == 0, 0.0, halo_ref[...])
    ext = jnp.concatenate([halo, p], axis=0)
    s = ext
    sums = []
    shift = 1
    for g in range(N_POOL_GROUPS):
        s = s[:, (POOL_GROUP_CH if g > 0 else 0):]
        s = s + pltpu.roll(s, shift, 0)
        shift *= 2
        sums.append(s[POOL_HALO:, 0:POOL_GROUP_CH])
    pos = i * tt + lax.broadcasted_iota(jnp.int32, (tt, 1), 0)
    cnts = [jnp.minimum(pos + 1, w).astype(F32) for w in POOL_WINDOWS]
    _pool_mix(sums, p, cnts, maps_ref, scale_ref, o_ref)


def pool_prompt(z, maps, scale, *, batch, seq, tt):
    nt = seq // tt
    pblk = P_OFF // POOL_CH
    hb = tt // POOL_HALO
    return pl.pallas_call(
        functools.partial(_pool_prompt_kernel, tt=tt),
        grid=(batch, nt),
        in_specs=[pl.BlockSpec((tt, POOL_CH), lambda b, i: (b * nt + i, pblk)),
                  pl.BlockSpec((POOL_HALO, POOL_CH),
                               lambda b, i: (jnp.maximum((b * nt + i) * hb - 1, 0), pblk)),
                  pl.BlockSpec((N_POOL_GROUPS, POOL_GROUP_CH, POOL_GROUP_CH), lambda b, i: (0, 0, 0)),
                  pl.BlockSpec((1, POOL_CH), lambda b, i: (0, 0))],
        out_specs=pl.BlockSpec((tt, POOL_CH), lambda b, i: (b * nt + i, 0)),
        out_shape=jax.ShapeDtypeStruct((batch * seq, POOL_CH), BF16),
        compiler_params=_cparams(2),
        name="pool_prompt",
    )(z, z, maps, scale)


def _pool_sample_kernel(p_ref, st_ref, maps_ref, scale_ref, o_ref, ns_ref):
    p = p_ref[...]
    rows = [st_ref[:, r, :] for r in range(POOL_BUF)]
    sums = []
    for g, w in enumerate(POOL_WINDOWS):
        c0 = g * POOL_GROUP_CH
        c1 = c0 + POOL_GROUP_CH
        s = p[:, c0:c1]
        for r in range(POOL_BUF - (w - 1), POOL_BUF):
            s = s + rows[r][:, c0:c1]
        sums.append(s)
    cnts = [float(min(PAST_LEN + 1, w)) for w in POOL_WINDOWS]
    _pool_mix(sums, p, cnts, maps_ref, scale_ref, o_ref)
    for r in range(POOL_BUF - 1):
        ns_ref[:, r, :] = rows[r + 1]
    ns_ref[:, POOL_BUF - 1, :] = p


def pool_sample(zs, state, maps, scale):
    nbatch = zs.shape[0]
    pblk = P_OFF // POOL_CH
    st_spec = pl.BlockSpec((nbatch, POOL_BUF, POOL_CH), lambda i: (0, 0, 0))
    return pl.pallas_call(
        _pool_sample_kernel,
        grid=(1,),
        in_specs=[pl.BlockSpec((nbatch, POOL_CH), lambda i: (0, pblk)),
                  st_spec,
                  pl.BlockSpec((N_POOL_GROUPS, POOL_GROUP_CH, POOL_GROUP_CH), lambda i: (0, 0, 0)),
                  pl.BlockSpec((1, POOL_CH), lambda i: (0, 0))],
        out_specs=[pl.BlockSpec((nbatch, POOL_CH), lambda i: (0, 0)), st_spec],
        out_shape=[jax.ShapeDtypeStruct((nbatch, POOL_CH), BF16),
                   jax.ShapeDtypeStruct((nbatch, POOL_BUF, POOL_CH), F32)],
        compiler_params=_cparams(1),
        name="pool_sample",
    )(zs, state, maps, scale)


def _mem_prompt_kernel(q_ref, k_ref, v_ref, o_ref):
    q = q_ref[...].astype(BF16)
    k = k_ref[...].astype(BF16)
    v = v_ref[...].astype(BF16)
    for h in range(MEM_HEADS):
        c0 = h * MEM_HEAD_DIM
        c1 = c0 + MEM_HEAD_DIM
        logits = _dot(q[:, c0:c1], k[:, c0:c1], _NT) * (MEM_HEAD_DIM ** -0.5)
        p = jnp.exp(logits - jnp.max(logits, axis=-1, keepdims=True))
        probs = p * (1.0 / jnp.sum(p, axis=-1, keepdims=True))
        o_ref[:, c0:c1] = _dot(probs, v[:, c0:c1]).astype(BF16)


def mem_prompt(z, kv, *, batch, seq, tq):
    nt = seq // tq
    mblk = M_OFF // MEM_WIDTH
    return pl.pallas_call(
        _mem_prompt_kernel,
        grid=(batch, nt),
        in_specs=[pl.BlockSpec((tq, MEM_WIDTH), lambda b, i: (b * nt + i, mblk)),
                  pl.BlockSpec((N_MEM, MEM_WIDTH), lambda b, i: (b, 0)),
                  pl.BlockSpec((N_MEM, MEM_WIDTH), lambda b, i: (b, 1))],
        out_specs=pl.BlockSpec((tq, MEM_WIDTH), lambda b, i: (b * nt + i, 0)),
        out_shape=jax.ShapeDtypeStruct((batch * seq, MEM_WIDTH), BF16),
        compiler_params=_cparams(2),
        name="mem_prompt",
    )(z, kv, kv)


MEM_Q_ROWS = 16


def _mem_sample_kernel(q_ref, k_ref, v_ref, o_ref, *, bb):
    hrow = lax.broadcasted_iota(jnp.int32, (MEM_Q_ROWS, MEM_WIDTH), 0)
    hcol = lax.broadcasted_iota(jnp.int32, (MEM_Q_ROWS, MEM_WIDTH), 1)
    own = (hcol // MEM_HEAD_DIM) == hrow
    rows = []
    for b in range(bb):
        k = jnp.concatenate([k_ref[b, :, h, :] for h in range(MEM_HEADS)], axis=1)
        v = jnp.concatenate([v_ref[b, :, h, :] for h in range(MEM_HEADS)], axis=1)
        qbd = jnp.where(own, jnp.broadcast_to(q_ref[b:b + 1, :], (MEM_Q_ROWS, MEM_WIDTH)), 0.0)
        logits = _dot(qbd, k, _NT) * (MEM_HEAD_DIM ** -0.5)
        p = jnp.exp(logits - jnp.max(logits, axis=-1, keepdims=True))
        probs = p * (1.0 / jnp.sum(p, axis=-1, keepdims=True))
        o = jnp.where(own, _dot(probs, v), 0.0)
        rows.append(jnp.sum(o, axis=0, keepdims=True))
    o_ref[...] = jnp.concatenate(rows, axis=0)


def mem_sample(zs, mem_k, mem_v, layer, *, bb):
    nbatch = zs.shape[0]
    mblk = M_OFF // MEM_WIDTH
    kv_spec = pl.BlockSpec((None, bb, N_MEM, MEM_HEADS, MEM_HEAD_DIM), lambda i: (layer, i, 0, 0, 0))
    return pl.pallas_call(
        functools.partial(_mem_sample_kernel, bb=bb),
        grid=(nbatch // bb,),
        in_specs=[pl.BlockSpec((bb, MEM_WIDTH), lambda i: (i, mblk)), kv_spec, kv_spec],
        out_specs=pl.BlockSpec((bb, MEM_WIDTH), lambda i: (i, 0)),
        out_shape=jax.ShapeDtypeStruct((nbatch, MEM_WIDTH), F32),
        compiler_params=_cparams(1),
        name="mem_sample",
    )(zs, mem_k, mem_v)


def _to_row_major(x, o_ref, rows):
    for c in range(ROW_SUB):
        o_ref[pl.ds(c, rows, stride=ROW_SUB), :] = x[:, c * LANES:(c + 1) * LANES]


def _from_row_major(ref, first_row, rows, c, pitch=ROW_SUB):
    return ref[pl.ds(first_row + c, rows, stride=pitch), :]


def _norm_router_kernel(xp_ref, xs_ref, g_ref, r_ref, xn_ref, idx_ref, wgt_ref, *, tm, n_prompt_steps):
    x = jnp.where(pl.program_id(0) < n_prompt_steps, xp_ref[...], xs_ref[...])
    xn = _rms(x, g_ref[...])
    _to_row_major(xn, xn_ref, tm)
    logits = _dot(xn, r_ref[...])
    lane = lax.broadcasted_iota(jnp.int32, logits.shape, 1)
    logits = jnp.where(lane < N_EXPERTS, logits, -jnp.inf)
    v1 = jnp.max(logits, axis=-1, keepdims=True)
    i1 = jnp.min(jnp.where(logits == v1, lane, ROUTER_PAD), axis=-1, keepdims=True)
    rest = jnp.where(lane == i1, -jnp.inf, logits)
    v2 = jnp.max(rest, axis=-1, keepdims=True)
    i2 = jnp.min(jnp.where(rest == v2, lane, ROUTER_PAD), axis=-1, keepdims=True)
    e2 = jnp.exp(v2 - v1)
    den = 1.0 + e2
    idx_ref[...] = jnp.where(lane == 0, i1, jnp.where(lane == 1, i2, 0))
    wgt_ref[...] = jnp.where(lane == 0, 1.0 / den, jnp.where(lane == 1, e2 / den, 0.0))


def norm_router(xp, xs, g, router_pad, *, tm):
    m_prompt, d = xp.shape
    m = m_prompt + xs.shape[0]
    nps = m_prompt // tm
    return pl.pallas_call(
        functools.partial(_norm_router_kernel, tm=tm, n_prompt_steps=nps),
        grid=(m // tm,),
        in_specs=[pl.BlockSpec((tm, d), lambda i: (jnp.minimum(i, nps - 1), 0)),
                  pl.BlockSpec((tm, d), lambda i: (jnp.maximum(i - nps, 0), 0)),
                  pl.BlockSpec((1, d), lambda i: (0, 0)),
                  pl.BlockSpec((d, ROUTER_PAD), lambda i: (0, 0))],
        out_specs=[pl.BlockSpec((tm * ROW_SUB, LANES), lambda i: (i, 0)),
                   pl.BlockSpec((tm, ROUTER_PAD), lambda i: (i, 0)),
                   pl.BlockSpec((tm, ROUTER_PAD), lambda i: (i, 0))],
        out_shape=[jax.ShapeDtypeStruct((m * ROW_SUB, LANES), F32),
                   jax.ShapeDtypeStruct((m, ROUTER_PAD), jnp.int32),
                   jax.ShapeDtypeStruct((m, ROUTER_PAD), F32)],
        compiler_params=_cparams(1),
        name="norm_router",
    )(xp, xs, g, router_pad)


def _token_copy(src_hbm, token, buf, slot, sem):
    return pltpu.make_async_copy(src_hbm.at[pl.ds(token * ROW_SUB, ROW_SUB)],
                                 buf.at[pl.ds(slot * BUF_PITCH, ROW_SUB)], sem)


def _gather_loop(src_hbm, tok_ref, buf, sem, count, *, wait):
    def body(s, c):
        copy = _token_copy(src_hbm, tok_ref[0, 0, s], buf, s, sem)
        if wait:
            copy.wait()
        else:
            copy.start()
        return c

    lax.fori_loop(0, count, body, 0, unroll=GATHER_UNROLL)


def _gather_step(i, n_steps, src_hbm, tok_ref, next_tok_ref, bufs, sems, count, valid, next_valid, consume):
    for slot in range(2):
        @pl.when(i % 2 == slot)
        def _():
            @pl.when((i == 0) & valid)
            def _():
                _gather_loop(src_hbm, tok_ref, bufs.at[slot], sems.at[slot], count, wait=False)

            @pl.when((i + 1 < n_steps) & next_valid)
            def _():
                _gather_loop(src_hbm, next_tok_ref, bufs.at[1 - slot], sems.at[1 - slot], count, wait=False)

            @pl.when(valid)
            def _():
                _gather_loop(src_hbm, tok_ref, bufs.at[slot], sems.at[slot], count, wait=True)
                consume(bufs.at[slot])


def _gather_specs(nchunk, count):
    def cur(i, *_):
        return (i, 0, 0)

    def nxt(i, *_):
        return (jnp.minimum(i + 1, nchunk - 1), 0, 0)

    return [pl.BlockSpec((1, 1, count), cur, memory_space=pltpu.SMEM),
            pl.BlockSpec((1, 1, count), nxt, memory_space=pltpu.SMEM)]


def _gather_scratch(count):
    return [pltpu.VMEM((2, count * BUF_PITCH, LANES), F32), pltpu.SemaphoreType.DMA((2,))]


def _dispatch_kernel(used_ref, tok_ref, next_tok_ref, src_hbm, o_ref, bufs, sems, *, rows, n_steps):
    i = pl.program_id(0)
    valid = i * rows < used_ref[0]

    def consume(buf):
        for c in range(ROW_SUB):
            o_ref[:, c * LANES:(c + 1) * LANES] = _from_row_major(buf, 0, rows, c, BUF_PITCH).astype(BF16)

    _gather_step(i, n_steps, src_hbm, tok_ref, next_tok_ref, bufs, sems, rows,
                 valid, (i + 1) * rows < used_ref[0], consume)

    @pl.when(jnp.logical_not(valid))
    def _():
        o_ref[...] = jnp.zeros_like(o_ref)


def moe_dispatch(xn, src_token, used_rows, *, rows):
    n = src_token.shape[0]
    nchunk = n // rows
    tokens = src_token.reshape(nchunk, 1, rows)
    return pl.pallas_call(
        functools.partial(_dispatch_kernel, rows=rows, n_steps=nchunk),
        grid_spec=pltpu.PrefetchScalarGridSpec(
            num_scalar_prefetch=1,
            grid=(nchunk,),
            in_specs=_gather_specs(nchunk, rows) + [pl.BlockSpec(memory_space=pl.ANY)],
            out_specs=pl.BlockSpec((rows, D_MODEL), lambda i, used: (i, 0)),
            scratch_shapes=_gather_scratch(rows)),
        out_shape=jax.ShapeDtypeStruct((n, D_MODEL), BF16),
        compiler_params=_cparams(1),
        name="moe_dispatch",
    )(used_rows, tokens, tokens, xn)


STEP_IDLE, STEP_COMPUTE, STEP_ZERO = 0, 1, 2


def _moe_up_kernel(xt_ref, ot_ref, oj_ref, we_ref, wj_ref, cast_ref, mode_ref,
                   x_ref, w1_ref, w3_ref, w2_ref, o_ref, w2b_ref, w1b_ref, w3b_ref, *, n_w2_blocks):
    s = pl.program_id(0)

    @pl.when(cast_ref[s] > 0)
    def _():
        w1b_ref[...] = w1_ref[0].astype(BF16)
        w3b_ref[...] = w3_ref[0].astype(BF16)

    @pl.when(s < n_w2_blocks)
    def _():
        w2b_ref[...] = w2_ref[...].astype(BF16)

    @pl.when(mode_ref[s] == STEP_COMPUTE)
    def _():
        x = x_ref[...]
        a = _dot(x, w1b_ref[...])
        b = _dot(x, w3b_ref[...])
        o_ref[...] = (jax.nn.silu(a) * b).astype(BF16)

    @pl.when(mode_ref[s] == STEP_ZERO)
    def _():
        o_ref[...] = jnp.zeros_like(o_ref)


def _moe_up_steps(tiles_per_expert, first_tile, *, n_tiles, nj):
    n_steps = (n_tiles + N_EXPERTS) * nj
    used = jnp.sum(tiles_per_expert)
    n_compute = used * nj
    n_zero = (n_tiles - used) * nj
    empty = tiles_per_expert == 0
    n_cast_only = jnp.sum(empty.astype(jnp.int32)) * nj
    n_live = n_compute + n_zero + n_cast_only
    a = jnp.minimum(jnp.arange(n_steps, dtype=jnp.int32), n_live - 1)
    blk_end = jnp.cumsum(tiles_per_expert * nj)
    e_c = jnp.minimum(jnp.sum((a[:, None] >= blk_end[None, :]).astype(jnp.int32), axis=1), N_EXPERTS - 1)
    onehot_c = (e_c[:, None] == jnp.arange(N_EXPERTS, dtype=jnp.int32)[None, :]).astype(jnp.int32)
    pick = lambda v: jnp.sum(onehot_c * v[None, :], axis=1)
    within = a - (pick(blk_end) - pick(tiles_per_expert) * nj)
    t_e = jnp.maximum(pick(tiles_per_expert), 1)
    j_c = within // t_e
    r_c = within % t_e
    tile_c = pick(first_tile) + r_c
    z = a - n_compute
    tile_z = used + z // nj
    j_z = z % nj
    b = a - n_compute - n_zero
    empty_rank = jnp.cumsum(empty.astype(jnp.int32)) - 1
    k = b // nj
    e_b = jnp.sum(jnp.where(empty[None, :] & (empty_rank[None, :] == k[:, None]),
                            jnp.arange(N_EXPERTS, dtype=jnp.int32)[None, :], 0), axis=1)
    j_b = b % nj
    is_c = a < n_compute
    is_z = (~is_c) & (a < n_compute + n_zero)
    is_b = (~is_c) & (~is_z)
    live = jnp.arange(n_steps, dtype=jnp.int32) < n_live
    last_tile = used - 1
    last_e = jnp.sum((last_tile >= jnp.cumsum(tiles_per_expert)).astype(jnp.int32))
    x_tile = jnp.where(is_c, tile_c, last_tile)
    o_tile = jnp.where(is_c, tile_c, jnp.where(is_z, tile_z, n_tiles - 1))
    o_col = jnp.where(is_c, j_c, jnp.where(is_z, j_z, nj - 1))
    w_e = jnp.where(is_c, e_c, jnp.where(is_b, e_b, last_e))
    w_col = jnp.where(is_c, j_c, jnp.where(is_b, j_b, nj - 1))
    cast = (live & ((is_c & (r_c == 0)) | is_b)).astype(jnp.int32)
    mode = jnp.where(live & is_c, STEP_COMPUTE, jnp.where(live & is_z, STEP_ZERO, STEP_IDLE))
    tables = (x_tile, o_tile, o_col, w_e, w_col, cast, mode)
    return tuple(t.astype(jnp.int32) for t in tables), n_steps


def moe_up(xs, w1, w3, w2, tiles_per_expert, first_tile, *, tm, tn):
    n, d = xs.shape
    f = w1.shape[2]
    nj = f // tn
    tables, n_steps = _moe_up_steps(tiles_per_expert, first_tile, n_tiles=n // tm, nj=nj)

    def x_map(s, xt, ot, oj, we, wj, cast, mode):
        return (xt[s], 0)

    def w_map(s, xt, ot, oj, we, wj, cast, mode):
        return (we[s], 0, wj[s])

    def o_map(s, xt, ot, oj, we, wj, cast, mode):
        return (ot[s], oj[s])

    w2_slabs = w2.reshape(-1, W2_SLAB_ROWS, d)
    n_slabs = w2_slabs.shape[0]
    assert n_slabs <= n_steps

    def w2_map(s, *_):
        return (jnp.minimum(s, n_slabs - 1), 0, 0)

    h, w2b = pl.pallas_call(
        functools.partial(_moe_up_kernel, n_w2_blocks=n_slabs),
        grid_spec=pltpu.PrefetchScalarGridSpec(
            num_scalar_prefetch=len(tables),
            grid=(n_steps,),
            in_specs=[pl.BlockSpec((tm, d), x_map),
                      pl.BlockSpec((1, d, tn), w_map),
                      pl.BlockSpec((1, d, tn), w_map),
                      pl.BlockSpec((1, W2_SLAB_ROWS, d), w2_map)],
            out_specs=[pl.BlockSpec((tm, tn), o_map),
                       pl.BlockSpec((1, W2_SLAB_ROWS, d), w2_map)],
            scratch_shapes=[pltpu.VMEM((d, tn), BF16), pltpu.VMEM((d, tn), BF16)]),
        out_shape=[jax.ShapeDtypeStruct((n, f), BF16),
                   jax.ShapeDtypeStruct(w2_slabs.shape, BF16)],
        compiler_params=_cparams(1),
        name="moe_up",
    )(*tables, xs, w1, w3, w2_slabs)
    return h, w2b.reshape(w2.shape)


def _weight_map(nj):
    def w_map(t, j, te, tv):
        return (te[t], 0, jnp.where(tv[t] > 0, j, nj - 1))

    return w_map


def _moe_down_kernel(te_ref, tv_ref, h_ref, w2_ref, o_ref, *, tm, tn):
    j = pl.program_id(1)
    groups = tn // LANES

    def put(val):
        for c in range(groups):
            o_ref[pl.ds(j * groups + c, tm, stride=ROW_SUB), :] = val[:, c * LANES:(c + 1) * LANES]

    @pl.when(tv_ref[pl.program_id(0)] > 0)
    def _():
        put(_dot(h_ref[...], w2_ref[0]))

    @pl.when(tv_ref[pl.program_id(0)] == 0)
    def _():
        put(jnp.zeros((tm, tn), F32))


def moe_down(h, w2, tiles, *, tm, tn):
    n, f = h.shape
    d = w2.shape[2]
    nj = d // tn
    return pl.pallas_call(
        functools.partial(_moe_down_kernel, tm=tm, tn=tn),
        grid_spec=pltpu.PrefetchScalarGridSpec(
            num_scalar_prefetch=2,
            grid=(n // tm, nj),
            in_specs=[pl.BlockSpec((tm, f), lambda t, j, te, tv: (t, 0)),
                      pl.BlockSpec((1, f, tn), _weight_map(nj))],
            out_specs=pl.BlockSpec((tm * ROW_SUB, LANES), lambda t, j, te, tv: (t, 0))),
        out_shape=jax.ShapeDtypeStruct((n * ROW_SUB, LANES), F32),
        compiler_params=_cparams(2),
        name="moe_down",
    )(*tiles, h, w2)


def _combine_kernel(pos_ref, next_pos_ref, xp_ref, xs_ref, w_ref, g_ref, ys_hbm, yp_ref, ysmp_ref, bufs, sems,
                    *, rows, n_prompt_chunks, n_steps):
    i = pl.program_id(0)
    w0 = w_ref[:, 0:1]
    w1 = w_ref[:, 1:2]

    def finish(buf, x_ref, o_ref):
        for c in range(ROW_SUB):
            a = _from_row_major(buf, 0, rows, c, BUF_PITCH)
            b = _from_row_major(buf, rows * BUF_PITCH, rows, c, BUF_PITCH)
            o_ref[:, c * LANES:(c + 1) * LANES] = x_ref[:, c * LANES:(c + 1) * LANES] + (w0 * a + w1 * b)
        o_ref[...] = _rms(o_ref[...], g_ref[...])

    def consume(buf):
        @pl.when(i < n_prompt_chunks)
        def _():
            finish(buf, xp_ref, yp_ref)

        @pl.when(i >= n_prompt_chunks)
        def _():
            finish(buf, xs_ref, ysmp_ref)

    every_step = i >= 0
    _gather_step(i, n_steps, ys_hbm, pos_ref, next_pos_ref, bufs, sems, TOP_K * rows,
                 every_step, every_step, consume)


def moe_combine(xp, xs, wgt, g_final, ys, dest, *, rows):
    m_prompt, d = xp.shape
    m = m_prompt + xs.shape[0]
    nchunk = m // rows
    npc = m_prompt // rows
    pos = dest.reshape(nchunk, rows, TOP_K).transpose(0, 2, 1).reshape(nchunk, 1, TOP_K * rows)

    def p_map(i):
        return (jnp.minimum(i, npc - 1), 0)

    def s_map(i):
        return (jnp.maximum(i - npc, 0), 0)

    return pl.pallas_call(
        functools.partial(_combine_kernel, rows=rows, n_prompt_chunks=npc, n_steps=nchunk),
        grid=(nchunk,),
        in_specs=_gather_specs(nchunk, TOP_K * rows) + [
                  pl.BlockSpec((rows, d), p_map),
                  pl.BlockSpec((rows, d), s_map),
                  pl.BlockSpec((rows, ROUTER_PAD), lambda i: (i, 0)),
                  pl.BlockSpec((1, d), lambda i: (0, 0)),
                  pl.BlockSpec(memory_space=pl.ANY)],
        out_specs=[pl.BlockSpec((rows, d), p_map), pl.BlockSpec((rows, d), s_map)],
        out_shape=[jax.ShapeDtypeStruct((m_prompt, d), F32),
                   jax.ShapeDtypeStruct((m - m_prompt, d), F32)],
        scratch_shapes=_gather_scratch(TOP_K * rows),
        compiler_params=_cparams(1),
        name="moe_combine",
    )(pos, pos, xp, xs, wgt, g_final, ys)


def _routing_tables(idx2, *, tm, n_rows):
    e_flat = idx2.reshape(-1)
    onehot = (e_flat[:, None] == jnp.arange(N_EXPERTS, dtype=jnp.int32)[None, :]).astype(jnp.int32)
    csum = jnp.cumsum(onehot, axis=0)
    counts = csum[-1]
    rank = jnp.sum(csum * onehot, axis=1) - 1
    padded = ((counts + tm - 1) // tm) * tm
    ends = jnp.cumsum(padded)
    starts = ends - padded
    dest = (jnp.sum(onehot * starts[None, :], axis=1) + rank).astype(jnp.int32)
    token = jnp.arange(e_flat.shape[0], dtype=jnp.int32) // TOP_K
    src_token = jnp.zeros((n_rows,), jnp.int32).at[dest].set(token)
    tile_ids = jnp.arange(n_rows // tm, dtype=jnp.int32)
    used_tiles = ends[-1] // tm
    tile_valid = (tile_ids < used_tiles).astype(jnp.int32)
    tile_start = jnp.minimum(tile_ids, used_tiles - 1) * tm
    tile_expert = jnp.sum((tile_start[:, None] >= ends[None, :]).astype(jnp.int32), axis=1)
    per_expert = ((padded // tm).astype(jnp.int32), (starts // tm).astype(jnp.int32))
    return dest.reshape(-1, TOP_K), src_token, ends[-1:].astype(jnp.int32), (tile_expert, tile_valid), per_expert


def _rel_bias_by_distance(rel_bias_table):
    d = jnp.arange(WINDOW, dtype=jnp.int32)
    max_exact = REL_BUCKETS // 2
    d_f = jnp.maximum(d, 1).astype(F32)
    large = max_exact + (jnp.log(d_f / max_exact) / math.log(REL_MAX_DIST / max_exact)
                         * (REL_BUCKETS - max_exact)).astype(jnp.int32)
    large = jnp.minimum(large, REL_BUCKETS - 1)
    bucket = jnp.where(d < max_exact, d, large)
    return rel_bias_table[bucket].T


def _mixer_prompt(x, mem2d, wts, lw, layer, bias2, *, batch, seq):
    kv = norm_matmul(mem2d, lw["g_mem"], wts["w_mem_kv"], layer, tm=1024, tn=512)
    z = norm_matmul(x, lw["g_attn"], wts["w_in"], layer, tm=1024, tn=512, gate_from_col=G_OFF)
    ya = swa_prompt(z, bias2, lw["sinks"], batch=batch, seq=seq)
    yp = pool_prompt(z, lw["pool_maps"], lw["pool_scale"], batch=batch, seq=seq, tt=512)
    ym = mem_prompt(z, kv, batch=batch, seq=seq, tq=512)
    merged = merge_branches(ya, yp, ym, wts["w_up_attn"], wts["w_up_pool"], wts["w_up_mem"], layer, z,
                            tm=1024, tn=512)
    x = matmul_residual(merged, wts["w_o"], layer, x, tm=1024, tn=512)
    z3 = z.reshape(batch, seq, IN_COLS)
    new_k = z3[:, seq - WINDOW:, K_OFF:V_OFF].reshape(batch, WINDOW, N_KV_HEADS, HEAD_DIM)
    new_v = z3[:, seq - WINDOW:, V_OFF:P_OFF].reshape(batch, WINDOW, N_KV_HEADS, HEAD_DIM)
    new_pool = z3[:, seq - POOL_BUF:, P_OFF:M_OFF]
    kv3 = kv.reshape(batch, N_MEM, 2 * MEM_WIDTH)
    mk = kv3[..., :MEM_WIDTH].reshape(batch, N_MEM, MEM_HEADS, MEM_HEAD_DIM)
    mv = kv3[..., MEM_WIDTH:].reshape(batch, N_MEM, MEM_HEADS, MEM_HEAD_DIM)
    return x, new_k, new_v, new_pool, mk, mv


def _mixer_sample(x, cache_k, cache_v, state, mem_k, mem_v, wts, lw, layer, bias_s):
    nb = x.shape[0]
    z = norm_matmul(x, lw["g_attn"], wts["w_in"], layer, tm=nb, tn=512, gate_from_col=G_OFF)
    q3 = z[:, Q_OFF:K_OFF].reshape(nb, N_HEADS, HEAD_DIM)
    ya3, new_k, new_v = swa_sample(q3, z, cache_k.reshape(nb, WINDOW, KV_WIDTH),
                                   cache_v.reshape(nb, WINDOW, KV_WIDTH), bias_s, lw["sinks"][:, None], bb=8)
    yp, new_pool = pool_sample(z, state, lw["pool_maps"], lw["pool_scale"])
    ym = mem_sample(z, mem_k, mem_v, layer, bb=8)
    ya = ya3.reshape(nb, ATTN_WIDTH)
    merged = merge_branches(ya, yp, ym, wts["w_up_attn"], wts["w_up_pool"], wts["w_up_mem"], layer, z,
                            tm=nb, tn=512)
    x = matmul_residual(merged, wts["w_o"], layer, x, tm=nb, tn=512)
    return (x, new_k.reshape(nb, WINDOW, N_KV_HEADS, HEAD_DIM),
            new_v.reshape(nb, WINDOW, N_KV_HEADS, HEAD_DIM), new_pool)


def _dense_ffn(x, g, w1, w3, w2, layer, *, tm):
    h = norm_swiglu(x, g, w1, w3, layer, tm=tm, tn=512)
    return matmul_residual(h, w2, layer, x, tm=tm, tn=512)


def _moe_ffn_final(xp, xs, g, router, w1, w3, w2, g_final):
    m_prompt = xp.shape[0]
    m = m_prompt + xs.shape[0]
    router_pad = jnp.pad(router, ((0, 0), (0, ROUTER_PAD - N_EXPERTS)))
    xn, idx_pad, wgt = norm_router(xp, xs, g, router_pad, tm=COMBINE_ROWS)
    n_rows = pl.cdiv(TOP_K * m, MOE_TM) * MOE_TM + N_EXPERTS * MOE_TM
    dest, src_token, used_rows, tiles, per_expert = _routing_tables(idx_pad[:, :TOP_K], tm=MOE_TM, n_rows=n_rows)
    xsort = moe_dispatch(xn, src_token, used_rows, rows=DISPATCH_ROWS)
    h, w2b = moe_up(xsort, w1, w3, w2, *per_expert, tm=MOE_TM, tn=512)
    ys = moe_down(h, w2b, tiles, tm=MOE_TM, tn=512)
    return moe_combine(xp, xs, wgt, g_final, ys, dest, rows=COMBINE_ROWS)


def kernel(x_prompt, x_sample, mem_prompt, cache_swa_k, cache_swa_v, state_pool, cache_mem_k, cache_mem_v, rel_bias_table, g_attn, w_in, attn_sinks, pool_maps, pool_scale, g_mem, w_mem_kv, w_up_attn, w_up_pool, w_up_mem, w_o, g_ffn, ffn_w1, ffn_w3, ffn_w2, moe_router, moe_w1, moe_w3, moe_w2, g_final):
    batch, seq, d = x_prompt.shape
    nb_s = x_sample.shape[0]
    depth = w_in.shape[0]
    assert depth == 2 and x_sample.shape[1] == 1

    bias_hw = _rel_bias_by_distance(rel_bias_table).astype(F32)
    col = jnp.arange(2 * WINDOW)
    bias2 = bias_hw[:, jnp.clip(WINDOW - col, 0, WINDOW - 1)]
    bias_s = bias_hw[:, ::-1]

    wts = dict(w_in=w_in.astype(BF16), w_mem_kv=w_mem_kv.astype(BF16), w_up_attn=w_up_attn.astype(BF16),
               w_up_pool=w_up_pool.astype(BF16), w_up_mem=w_up_mem.astype(BF16), w_o=w_o.astype(BF16))
    ffn = (ffn_w1.astype(BF16), ffn_w3.astype(BF16), ffn_w2.astype(BF16))

    xp = x_prompt.reshape(batch * seq, d)
    xs = x_sample.reshape(nb_s, d)
    mem2d = mem_prompt.reshape(batch * N_MEM, d)
    p_k, p_v, p_pool, p_mk, p_mv, s_k, s_v, s_pool = [], [], [], [], [], [], [], []
    for l in range(depth):
        lw = dict(g_attn=g_attn[l][None, :], sinks=attn_sinks[l], pool_maps=pool_maps[l].astype(BF16),
                  pool_scale=pool_scale[l][None, :], g_mem=g_mem[l][None, :])
        xp, nk, nv, npool, mk, mv = _mixer_prompt(xp, mem2d, wts, lw, l, bias2, batch=batch, seq=seq)
        p_k.append(nk); p_v.append(nv); p_pool.append(npool); p_mk.append(mk); p_mv.append(mv)
        xs, nk, nv, npool = _mixer_sample(xs, cache_swa_k[l], cache_swa_v[l], state_pool[l],
                                          cache_mem_k, cache_mem_v, wts, lw, l, bias_s)
        s_k.append(nk); s_v.append(nv); s_pool.append(npool)
        gl = g_ffn[l][None, :]
        i = l // 2
        if l % 2 == 0:
            xp = _dense_ffn(xp, gl, *ffn, i, tm=1024)
            xs = _dense_ffn(xs, gl, *ffn, i, tm=nb_s)
        else:
            yp, ys = _moe_ffn_final(xp, xs, gl, moe_router[i], moe_w1[i], moe_w3[i], moe_w2[i], g_final[None, :])
    y_prompt = yp.reshape(batch, seq, d)
    y_sample = ys.reshape(nb_s, 1, d)
    return (y_prompt, y_sample,
            jnp.stack(p_k), jnp.stack(p_v), jnp.stack(p_pool), jnp.stack(p_mk), jnp.stack(p_mv),
            jnp.stack(s_k), jnp.stack(s_v), jnp.stack(s_pool))
```

```python
import functools
import math

import jax
import jax.numpy as jnp
from jax import lax
from jax.experimental import pallas as pl
from jax.experimental.pallas import tpu as pltpu

F32 = jnp.float32
BF16 = jnp.bfloat16

D_MODEL = 2048
N_HEADS = 16
N_KV_HEADS = 4
HEAD_DIM = 64
Q_GROUP = N_HEADS // N_KV_HEADS
WINDOW = 128
ATTN_WIDTH = N_HEADS * HEAD_DIM
KV_WIDTH = N_KV_HEADS * HEAD_DIM
REL_BUCKETS = 32
REL_MAX_DIST = 128
POOL_WINDOWS = (2, 4, 8, 16)
N_POOL_GROUPS = 4
POOL_CH = D_MODEL // 4
POOL_GROUP_CH = POOL_CH // N_POOL_GROUPS
POOL_BUF = max(POOL_WINDOWS) - 1
N_MEM = 256
MEM_HEADS = 4
MEM_HEAD_DIM = 128
MEM_WIDTH = MEM_HEADS * MEM_HEAD_DIM
N_BRANCHES = 3
Q_OFF = 0
K_OFF = Q_OFF + ATTN_WIDTH
V_OFF = K_OFF + KV_WIDTH
P_OFF = V_OFF + KV_WIDTH
M_OFF = P_OFF + POOL_CH
G_OFF = M_OFF + MEM_WIDTH
IN_COLS = G_OFF + N_BRANCHES * D_MODEL
N_EXPERTS = 8
TOP_K = 2
RMS_EPS = 1e-5
NEG_INF = -1e30
PAST_LEN = 8192

VMEM_LIMIT_BYTES = 56 * 1024 * 1024
LANES = 128
POOL_HALO = 16
ROUTER_PAD = LANES
ROW_SUB = D_MODEL // LANES
BUF_PITCH = ROW_SUB + 8
GATHER_UNROLL = 8
EPILOGUE_ROWS = 512
W2_SLAB_ROWS = 64
MOE_TM = 512
DISPATCH_ROWS = 256
COMBINE_ROWS = 128


def _cparams(n_axes):
    return pltpu.CompilerParams(dimension_semantics=("arbitrary",) * n_axes,
                                vmem_limit_bytes=VMEM_LIMIT_BYTES)


def _rms(x, g):
    return (x * lax.rsqrt(jnp.mean(x * x, axis=-1, keepdims=True) + RMS_EPS)) * g


_NN = (((1,), (0,)), ((), ()))
_NT = (((1,), (1,)), ((), ()))


def _dot(a, b, dims=_NN):
    return lax.dot_general(a.astype(BF16), b.astype(BF16), dims, preferred_element_type=F32)


def _row_chunks(rows):
    n = max(1, rows // EPILOGUE_ROWS)
    return [slice(k * (rows // n), (k + 1) * (rows // n)) for k in range(n)]


def _layer_cols(layer, rows, tn):
    return pl.BlockSpec((None, rows, tn), lambda i, j: (layer, 0, j))


def _norm_mm_kernel(x_ref, g_ref, w_ref, o_ref, xn_ref):
    @pl.when(pl.program_id(1) == 0)
    def _():
        xn_ref[...] = _rms(x_ref[...], g_ref[...]).astype(BF16)

    for rs in _row_chunks(o_ref.shape[0]):
        o_ref[rs, :] = _dot(xn_ref[rs, :], w_ref[...])


def norm_matmul(x, g, w, layer, *, tm, tn, n_cols=None):
    m, d = x.shape
    n = w.shape[2] if n_cols is None else n_cols
    return pl.pallas_call(
        _norm_mm_kernel,
        grid=(m // tm, n // tn),
        in_specs=[pl.BlockSpec((tm, d), lambda i, j: (i, 0)),
                  pl.BlockSpec((1, d), lambda i, j: (0, 0)),
                  _layer_cols(layer, d, tn)],
        out_specs=pl.BlockSpec((tm, tn), lambda i, j: (i, j)),
        out_shape=jax.ShapeDtypeStruct((m, n), F32),
        scratch_shapes=[pltpu.VMEM((tm, d), BF16)],
        compiler_params=_cparams(2),
        name="norm_matmul",
    )(x, g, w)


def _norm_swiglu_kernel(x_ref, g_ref, w1_ref, w3_ref, o_ref, xn_ref):
    @pl.when(pl.program_id(1) == 0)
    def _():
        xn_ref[...] = _rms(x_ref[...], g_ref[...]).astype(BF16)

    for rs in _row_chunks(o_ref.shape[0]):
        xn = xn_ref[rs, :]
        a = _dot(xn, w1_ref[...])
        b = _dot(xn, w3_ref[...])
        o_ref[rs, :] = (jax.nn.silu(a) * b).astype(BF16)


def norm_swiglu(x, g, w1, w3, layer, *, tm, tn):
    m, d = x.shape
    n = w1.shape[2]
    return pl.pallas_call(
        _norm_swiglu_kernel,
        grid=(m // tm, n // tn),
        in_specs=[pl.BlockSpec((tm, d), lambda i, j: (i, 0)),
                  pl.BlockSpec((1, d), lambda i, j: (0, 0)),
                  _layer_cols(layer, d, tn), _layer_cols(layer, d, tn)],
        out_specs=pl.BlockSpec((tm, tn), lambda i, j: (i, j)),
        out_shape=jax.ShapeDtypeStruct((m, n), BF16),
        scratch_shapes=[pltpu.VMEM((tm, d), BF16)],
        compiler_params=_cparams(2),
        name="norm_swiglu",
    )(x, g, w1, w3)


def _mm_res_kernel(a_ref, w_ref, r_ref, o_ref):
    for rs in _row_chunks(o_ref.shape[0]):
        o_ref[rs, :] = r_ref[rs, :] + _dot(a_ref[rs, :], w_ref[...])


def matmul_residual(a, w, layer, res, *, tm, tn):
    m, k = a.shape
    n = w.shape[2]
    return pl.pallas_call(
        _mm_res_kernel,
        grid=(m // tm, n // tn),
        in_specs=[pl.BlockSpec((tm, k), lambda i, j: (i, 0)),
                  _layer_cols(layer, k, tn),
                  pl.BlockSpec((tm, tn), lambda i, j: (i, j))],
        out_specs=pl.BlockSpec((tm, tn), lambda i, j: (i, j)),
        out_shape=jax.ShapeDtypeStruct((m, n), F32),
        compiler_params=_cparams(2),
        name="matmul_residual",
    )(a, w, res)


def _merge_kernel(x_ref, g_ref, ya_ref, yp_ref, ym_ref, wa_ref, wp_ref, wm_ref, wg0_ref, wg1_ref, wg2_ref,
                  o_ref, xn_ref):
    @pl.when(pl.program_id(1) == 0)
    def _():
        xn_ref[...] = _rms(x_ref[...], g_ref[...]).astype(BF16)

    for rs in _row_chunks(o_ref.shape[0]):
        xn = xn_ref[rs, :]
        acc = jax.nn.sigmoid(_dot(xn, wg0_ref[...])) * _dot(ya_ref[rs, :], wa_ref[...])
        acc = acc + jax.nn.sigmoid(_dot(xn, wg1_ref[...])) * _dot(yp_ref[rs, :], wp_ref[...])
        acc = acc + jax.nn.sigmoid(_dot(xn, wg2_ref[...])) * _dot(ym_ref[rs, :], wm_ref[...])
        o_ref[rs, :] = acc.astype(BF16)


def merge_branches(x, g, ya, yp, ym, wa, wp, wm, w_in, layer, *, tm, tn):
    m, d = x.shape
    nj = D_MODEL // tn
    g_blk = G_OFF // tn

    def gate_cols(b):
        return pl.BlockSpec((None, d, tn), lambda i, j: (layer, 0, g_blk + b * nj + j))

    return pl.pallas_call(
        _merge_kernel,
        grid=(m // tm, nj),
        in_specs=[pl.BlockSpec((tm, d), lambda i, j: (i, 0)),
                  pl.BlockSpec((1, d), lambda i, j: (0, 0)),
                  pl.BlockSpec((tm, ATTN_WIDTH), lambda i, j: (i, 0)),
                  pl.BlockSpec((tm, POOL_CH), lambda i, j: (i, 0)),
                  pl.BlockSpec((tm, MEM_WIDTH), lambda i, j: (i, 0)),
                  _layer_cols(layer, ATTN_WIDTH, tn), _layer_cols(layer, POOL_CH, tn),
                  _layer_cols(layer, MEM_WIDTH, tn),
                  gate_cols(0), gate_cols(1), gate_cols(2)],
        out_specs=pl.BlockSpec((tm, tn), lambda i, j: (i, j)),
        out_shape=jax.ShapeDtypeStruct((m, D_MODEL), BF16),
        scratch_shapes=[pltpu.VMEM((tm, d), BF16)],
        compiler_params=_cparams(2),
        name="merge_branches",
    )(x, g, ya, yp, ym, wa, wp, wm, w_in, w_in, w_in)


def _swa_prompt_kernel(q_ref, kp_ref, kc_ref, vp_ref, vc_ref, base_ref, sink_ref, o_ref, bias_ref):
    i = pl.program_id(1)

    @pl.when((pl.program_id(0) == 0) & (i == 0))
    def _():
        for h in range(N_HEADS):
            row0 = jnp.broadcast_to(base_ref[h:h + 1, :], (WINDOW, 2 * WINDOW))
            bias_ref[h] = pltpu.roll(row0, 0, 1, stride=1, stride_axis=0)

    q = q_ref[...].astype(BF16)
    k2 = jnp.concatenate([kp_ref[...], kc_ref[...]], axis=0).astype(BF16)
    v2 = jnp.concatenate([vp_ref[...], vc_ref[...]], axis=0).astype(BF16)
    row = lax.broadcasted_iota(jnp.int32, (WINDOW, 2 * WINDOW), 0)
    col = lax.broadcasted_iota(jnp.int32, (WINDOW, 2 * WINDOW), 1)
    dist = row + WINDOW - col
    valid = (dist >= 0) & (dist < WINDOW) & ((i > 0) | (col >= WINDOW))
    for h in range(N_HEADS):
        kv = h // Q_GROUP
        qh = q[:, h * HEAD_DIM:(h + 1) * HEAD_DIM]
        kh = k2[:, kv * HEAD_DIM:(kv + 1) * HEAD_DIM]
        vh = v2[:, kv * HEAD_DIM:(kv + 1) * HEAD_DIM]
        logits = _dot(qh, kh, _NT) * (HEAD_DIM ** -0.5)
        logits = jnp.where(valid, logits + bias_ref[h], NEG_INF)
        s = sink_ref[h]
        m = jnp.maximum(jnp.max(logits, axis=-1, keepdims=True), s)
        p = jnp.exp(logits - m)
        denom = jnp.sum(p, axis=-1, keepdims=True) + jnp.exp(s - m)
        o = _dot(p * (1.0 / denom), vh)
        o_ref[:, h * HEAD_DIM:(h + 1) * HEAD_DIM] = o.astype(BF16)


def swa_prompt(z, bias_base, sinks, *, batch, seq):
    nb = seq // WINDOW
    kblk = K_OFF // KV_WIDTH
    vblk = V_OFF // KV_WIDTH

    def cur(c):
        return pl.BlockSpec((WINDOW, KV_WIDTH), lambda b, i: (b * nb + i, c))

    def prev(c):
        return pl.BlockSpec((WINDOW, KV_WIDTH), lambda b, i: (b * nb + jnp.maximum(i - 1, 0), c))

    return pl.pallas_call(
        _swa_prompt_kernel,
        grid=(batch, nb),
        in_specs=[pl.BlockSpec((WINDOW, ATTN_WIDTH), lambda b, i: (b * nb + i, 0)),
                  prev(kblk), cur(kblk), prev(vblk), cur(vblk),
                  pl.BlockSpec((N_HEADS, 2 * WINDOW), lambda b, i: (0, 0)),
                  pl.BlockSpec(memory_space=pltpu.SMEM)],
        out_specs=pl.BlockSpec((WINDOW, ATTN_WIDTH), lambda b, i: (b * nb + i, 0)),
        out_shape=jax.ShapeDtypeStruct((batch * seq, ATTN_WIDTH), BF16),
        scratch_shapes=[pltpu.VMEM((N_HEADS, WINDOW, 2 * WINDOW), F32)],
        compiler_params=_cparams(2),
        name="swa_prompt",
    )(z, z, z, z, z, bias_base, sinks)


def _swa_sample_kernel(q_ref, kn_ref, vn_ref, ck_ref, cv_ref, bias_ref, sink_ref,
                       o_ref, ok_ref, ov_ref, *, bb):
    row = lax.broadcasted_iota(jnp.int32, (WINDOW, KV_WIDTH), 0)
    hrow = lax.broadcasted_iota(jnp.int32, (N_HEADS, KV_WIDTH), 0)
    hcol = lax.broadcasted_iota(jnp.int32, (N_HEADS, KV_WIDTH), 1)
    own = (hcol // HEAD_DIM) == (hrow // Q_GROUP)
    s = sink_ref[...]
    for b in range(bb):
        newk = jnp.where(row == WINDOW - 1, kn_ref[b:b + 1, :], pltpu.roll(ck_ref[b], WINDOW - 1, 0))
        newv = jnp.where(row == WINDOW - 1, vn_ref[b:b + 1, :], pltpu.roll(cv_ref[b], WINDOW - 1, 0))
        ok_ref[b] = newk
        ov_ref[b] = newv
        qb = q_ref[b]
        qbd = jnp.where(own, jnp.concatenate([qb] * N_KV_HEADS, axis=1), 0.0)
        logits = _dot(qbd, newk, _NT) * (HEAD_DIM ** -0.5)
        logits = logits + bias_ref[...]
        m = jnp.maximum(jnp.max(logits, axis=-1, keepdims=True), s)
        p = jnp.exp(logits - m)
        denom = jnp.sum(p, axis=-1, keepdims=True) + jnp.exp(s - m)
        o = jnp.where(own, _dot(p * (1.0 / denom), newv), 0.0)
        acc = o[:, 0:HEAD_DIM]
        for c in range(1, N_KV_HEADS):
            acc = acc + o[:, c * HEAD_DIM:(c + 1) * HEAD_DIM]
        o_ref[b] = acc.astype(BF16)


def swa_sample(q3, zs, cache_k, cache_v, bias_s, sinks_col, *, bb):
    nbatch = q3.shape[0]
    kblk = K_OFF // KV_WIDTH
    vblk = V_OFF // KV_WIDTH
    cache_spec = pl.BlockSpec((bb, WINDOW, KV_WIDTH), lambda i: (i, 0, 0))
    return pl.pallas_call(
        functools.partial(_swa_sample_kernel, bb=bb),
        grid=(nbatch // bb,),
        in_specs=[pl.BlockSpec((bb, N_HEADS, HEAD_DIM), lambda i: (i, 0, 0)),
                  pl.BlockSpec((bb, KV_WIDTH), lambda i: (i, kblk)),
                  pl.BlockSpec((bb, KV_WIDTH), lambda i: (i, vblk)),
                  cache_spec, cache_spec,
                  pl.BlockSpec((N_HEADS, WINDOW), lambda i: (0, 0)),
                  pl.BlockSpec((N_HEADS, 1), lambda i: (0, 0))],
        out_specs=[pl.BlockSpec((bb, N_HEADS, HEAD_DIM), lambda i: (i, 0, 0)), cache_spec, cache_spec],
        out_shape=[jax.ShapeDtypeStruct((nbatch, N_HEADS, HEAD_DIM), BF16),
                   jax.ShapeDtypeStruct((nbatch, WINDOW, KV_WIDTH), F32),
                   jax.ShapeDtypeStruct((nbatch, WINDOW, KV_WIDTH), F32)],
        compiler_params=_cparams(1),
        name="swa_sample",
    )(q3, zs, zs, cache_k, cache_v, bias_s, sinks_col)


def _pool_mix(sums, p, cnts, maps_ref, scale_ref, o_ref):
    for g in range(N_POOL_GROUPS):
        c0 = g * POOL_GROUP_CH
        c1 = c0 + POOL_GROUP_CH
        diff = sums[g] / cnts[g] - p[:, c0:c1]
        mixed = _dot(diff, maps_ref[g])
        o_ref[:, c0:c1] = (mixed * scale_ref[:, c0:c1]).astype(BF16)


def _pool_prompt_kernel(p_ref, halo_ref, maps_ref, scale_ref, o_ref, *, tt):
    i = pl.program_id(1)
    p = p_ref[...]
    halo = jnp.where(i == 0, 0.0, halo_ref[...])
    ext = jnp.concatenate([halo, p], axis=0)
    s = ext
    sums = []
    shift = 1
    for g in range(N_POOL_GROUPS):
        s = s[:, (POOL_GROUP_CH if g > 0 else 0):]
        s = s + pltpu.roll(s, shift, 0)
        shift *= 2
        sums.append(s[POOL_HALO:, 0:POOL_GROUP_CH])
    pos = i * tt + lax.broadcasted_iota(jnp.int32, (tt, 1), 0)
    cnts = [jnp.minimum(pos + 1, w).astype(F32) for w in POOL_WINDOWS]
    _pool_mix(sums, p, cnts, maps_ref, scale_ref, o_ref)


def pool_prompt(z, maps, scale, *, batch, seq, tt):
    nt = seq // tt
    pblk = P_OFF // POOL_CH
    hb = tt // POOL_HALO
    return pl.pallas_call(
        functools.partial(_pool_prompt_kernel, tt=tt),
        grid=(batch, nt),
        in_specs=[pl.BlockSpec((tt, POOL_CH), lambda b, i: (b * nt + i, pblk)),
                  pl.BlockSpec((POOL_HALO, POOL_CH),
                               lambda b, i: (jnp.maximum((b * nt + i) * hb - 1, 0), pblk)),
                  pl.BlockSpec((N_POOL_GROUPS, POOL_GROUP_CH, POOL_GROUP_CH), lambda b, i: (0, 0, 0)),
                  pl.BlockSpec((1, POOL_CH), lambda b, i: (0, 0))],
        out_specs=pl.BlockSpec((tt, POOL_CH), lambda b, i: (b * nt + i, 0)),
        out_shape=jax.ShapeDtypeStruct((batch * seq, POOL_CH), BF16),
        compiler_params=_cparams(2),
        name="pool_prompt",
    )(z, z, maps, scale)


def _pool_sample_kernel(p_ref, st_ref, maps_ref, scale_ref, o_ref, ns_ref):
    p = p_ref[...]
    rows = [st_ref[:, r, :] for r in range(POOL_BUF)]
    sums = []
    for g, w in enumerate(POOL_WINDOWS):
        c0 = g * POOL_GROUP_CH
        c1 = c0 + POOL_GROUP_CH
        s = p[:, c0:c1]
        for r in range(POOL_BUF - (w - 1), POOL_BUF):
            s = s + rows[r][:, c0:c1]
        sums.append(s)
    cnts = [float(min(PAST_LEN + 1, w)) for w in POOL_WINDOWS]
    _pool_mix(sums, p, cnts, maps_ref, scale_ref, o_ref)
    for r in range(POOL_BUF - 1):
        ns_ref[:, r, :] = rows[r + 1]
    ns_ref[:, POOL_BUF - 1, :] = p


def pool_sample(zs, state, maps, scale):
    nbatch = zs.shape[0]
    pblk = P_OFF // POOL_CH
    st_spec = pl.BlockSpec((nbatch, POOL_BUF, POOL_CH), lambda i: (0, 0, 0))
    return pl.pallas_call(
        _pool_sample_kernel,
        grid=(1,),
        in_specs=[pl.BlockSpec((nbatch, POOL_CH), lambda i: (0, pblk)),
                  st_spec,
                  pl.BlockSpec((N_POOL_GROUPS, POOL_GROUP_CH, POOL_GROUP_CH), lambda i: (0, 0, 0)),
                  pl.BlockSpec((1, POOL_CH), lambda i: (0, 0))],
        out_specs=[pl.BlockSpec((nbatch, POOL_CH), lambda i: (0, 0)), st_spec],
        out_shape=[jax.ShapeDtypeStruct((nbatch, POOL_CH), BF16),
                   jax.ShapeDtypeStruct((nbatch, POOL_BUF, POOL_CH), F32)],
        compiler_params=_cparams(1),
        name="pool_sample",
    )(zs, state, maps, scale)


def _mem_prompt_kernel(q_ref, k_ref, v_ref, o_ref):
    q = q_ref[...].astype(BF16)
    k = k_ref[...].astype(BF16)
    v = v_ref[...].astype(BF16)
    for h in range(MEM_HEADS):
        c0 = h * MEM_HEAD_DIM
        c1 = c0 + MEM_HEAD_DIM
        logits = _dot(q[:, c0:c1], k[:, c0:c1], _NT) * (MEM_HEAD_DIM ** -0.5)
        p = jnp.exp(logits - jnp.max(logits, axis=-1, keepdims=True))
        probs = p * (1.0 / jnp.sum(p, axis=-1, keepdims=True))
        o_ref[:, c0:c1] = _dot(probs, v[:, c0:c1]).astype(BF16)


def mem_prompt(z, kv, *, batch, seq, tq):
    nt = seq // tq
    mblk = M_OFF // MEM_WIDTH
    return pl.pallas_call(
        _mem_prompt_kernel,
        grid=(batch, nt),
        in_specs=[pl.BlockSpec((tq, MEM_WIDTH), lambda b, i: (b * nt + i, mblk)),
                  pl.BlockSpec((N_MEM, MEM_WIDTH), lambda b, i: (b, 0)),
                  pl.BlockSpec((N_MEM, MEM_WIDTH), lambda b, i: (b, 1))],
        out_specs=pl.BlockSpec((tq, MEM_WIDTH), lambda b, i: (b * nt + i, 0)),
        out_shape=jax.ShapeDtypeStruct((batch * seq, MEM_WIDTH), BF16),
        compiler_params=_cparams(2),
        name="mem_prompt",
    )(z, kv, kv)


MEM_Q_ROWS = 16


def _mem_sample_kernel(q_ref, k_ref, v_ref, o_ref, *, bb):
    hrow = lax.broadcasted_iota(jnp.int32, (MEM_Q_ROWS, MEM_WIDTH), 0)
    hcol = lax.broadcasted_iota(jnp.int32, (MEM_Q_ROWS, MEM_WIDTH), 1)
    own = (hcol // MEM_HEAD_DIM) == hrow
    rows = []
    for b in range(bb):
        k = jnp.concatenate([k_ref[b, :, h, :] for h in range(MEM_HEADS)], axis=1)
        v = jnp.concatenate([v_ref[b, :, h, :] for h in range(MEM_HEADS)], axis=1)
        qbd = jnp.where(own, jnp.broadcast_to(q_ref[b:b + 1, :], (MEM_Q_ROWS, MEM_WIDTH)), 0.0)
        logits = _dot(qbd, k, _NT) * (MEM_HEAD_DIM ** -0.5)
        p = jnp.exp(logits - jnp.max(logits, axis=-1, keepdims=True))
        probs = p * (1.0 / jnp.sum(p, axis=-1, keepdims=True))
        o = jnp.where(own, _dot(probs, v), 0.0)
        rows.append(jnp.sum(o, axis=0, keepdims=True))
    o_ref[...] = jnp.concatenate(rows, axis=0)


def mem_sample(zs, mem_k, mem_v, layer, *, bb):
    nbatch = zs.shape[0]
    mblk = M_OFF // MEM_WIDTH
    kv_spec = pl.BlockSpec((None, bb, N_MEM, MEM_HEADS, MEM_HEAD_DIM), lambda i: (layer, i, 0, 0, 0))
    return pl.pallas_call(
        functools.partial(_mem_sample_kernel, bb=bb),
        grid=(nbatch // bb,),
        in_specs=[pl.BlockSpec((bb, MEM_WIDTH), lambda i: (i, mblk)), kv_spec, kv_spec],
        out_specs=pl.BlockSpec((bb, MEM_WIDTH), lambda i: (i, 0)),
        out_shape=jax.ShapeDtypeStruct((nbatch, MEM_WIDTH), F32),
        compiler_params=_cparams(1),
        name="mem_sample",
    )(zs, mem_k, mem_v)


def _to_row_major(x, o_ref, rows):
    for c in range(ROW_SUB):
        o_ref[pl.ds(c, rows, stride=ROW_SUB), :] = x[:, c * LANES:(c + 1) * LANES]


def _from_row_major(ref, first_row, rows, c, pitch=ROW_SUB):
    return ref[pl.ds(first_row + c, rows, stride=pitch), :]


def _norm_router_kernel(xp_ref, xs_ref, g_ref, r_ref, xn_ref, idx_ref, wgt_ref, *, tm, n_prompt_steps):
    x = jnp.where(pl.program_id(0) < n_prompt_steps, xp_ref[...], xs_ref[...])
    xn = _rms(x, g_ref[...])
    _to_row_major(xn, xn_ref, tm)
    logits = _dot(xn, r_ref[...])
    lane = lax.broadcasted_iota(jnp.int32, logits.shape, 1)
    logits = jnp.where(lane < N_EXPERTS, logits, -jnp.inf)
    v1 = jnp.max(logits, axis=-1, keepdims=True)
    i1 = jnp.min(jnp.where(logits == v1, lane, ROUTER_PAD), axis=-1, keepdims=True)
    rest = jnp.where(lane == i1, -jnp.inf, logits)
    v2 = jnp.max(rest, axis=-1, keepdims=True)
    i2 = jnp.min(jnp.where(rest == v2, lane, ROUTER_PAD), axis=-1, keepdims=True)
    e2 = jnp.exp(v2 - v1)
    den = 1.0 + e2
    idx_ref[...] = jnp.where(lane == 0, i1, jnp.where(lane == 1, i2, 0))
    wgt_ref[...] = jnp.where(lane == 0, 1.0 / den, jnp.where(lane == 1, e2 / den, 0.0))


def norm_router(xp, xs, g, router_pad, *, tm):
    m_prompt, d = xp.shape
    m = m_prompt + xs.shape[0]
    nps = m_prompt // tm
    return pl.pallas_call(
        functools.partial(_norm_router_kernel, tm=tm, n_prompt_steps=nps),
        grid=(m // tm,),
        in_specs=[pl.BlockSpec((tm, d), lambda i: (jnp.minimum(i, nps - 1), 0)),
                  pl.BlockSpec((tm, d), lambda i: (jnp.maximum(i - nps, 0), 0)),
                  pl.BlockSpec((1, d), lambda i: (0, 0)),
                  pl.BlockSpec((d, ROUTER_PAD), lambda i: (0, 0))],
        out_specs=[pl.BlockSpec((tm * ROW_SUB, LANES), lambda i: (i, 0)),
                   pl.BlockSpec((tm, ROUTER_PAD), lambda i: (i, 0)),
                   pl.BlockSpec((tm, ROUTER_PAD), lambda i: (i, 0))],
        out_shape=[jax.ShapeDtypeStruct((m * ROW_SUB, LANES), F32),
                   jax.ShapeDtypeStruct((m, ROUTER_PAD), jnp.int32),
                   jax.ShapeDtypeStruct((m, ROUTER_PAD), F32)],
        compiler_params=_cparams(1),
        name="norm_router",
    )(xp, xs, g, router_pad)


def _token_copy(src_hbm, token, buf, slot, sem):
    return pltpu.make_async_copy(src_hbm.at[pl.ds(token * ROW_SUB, ROW_SUB)],
                                 buf.at[pl.ds(slot * BUF_PITCH, ROW_SUB)], sem)


def _gather_loop(src_hbm, tok_ref, buf, sem, count, *, wait):
    def body(s, c):
        copy = _token_copy(src_hbm, tok_ref[0, 0, s], buf, s, sem)
        if wait:
            copy.wait()
        else:
            copy.start()
        return c

    lax.fori_loop(0, count, body, 0, unroll=GATHER_UNROLL)


def _gather_step(i, n_steps, src_hbm, tok_ref, next_tok_ref, bufs, sems, count, valid, next_valid, consume):
    for slot in range(2):
        @pl.when(i % 2 == slot)
        def _():
            @pl.when((i == 0) & valid)
            def _():
                _gather_loop(src_hbm, tok_ref, bufs.at[slot], sems.at[slot], count, wait=False)

            @pl.when((i + 1 < n_steps) & next_valid)
            def _():
                _gather_loop(src_hbm, next_tok_ref, bufs.at[1 - slot], sems.at[1 - slot], count, wait=False)

            @pl.when(valid)
            def _():
                _gather_loop(src_hbm, tok_ref, bufs.at[slot], sems.at[slot], count, wait=True)
                consume(bufs.at[slot])


def _gather_specs(nchunk, count):
    def cur(i, *_):
        return (i, 0, 0)

    def nxt(i, *_):
        return (jnp.minimum(i + 1, nchunk - 1), 0, 0)

    return [pl.BlockSpec((1, 1, count), cur, memory_space=pltpu.SMEM),
            pl.BlockSpec((1, 1, count), nxt, memory_space=pltpu.SMEM)]


def _gather_scratch(count):
    return [pltpu.VMEM((2, count * BUF_PITCH, LANES), F32), pltpu.SemaphoreType.DMA((2,))]


def _dispatch_kernel(used_ref, tok_ref, next_tok_ref, src_hbm, o_ref, bufs, sems, *, rows, n_steps):
    i = pl.program_id(0)
    valid = i * rows < used_ref[0]

    def consume(buf):
        for c in range(ROW_SUB):
            o_ref[:, c * LANES:(c + 1) * LANES] = _from_row_major(buf, 0, rows, c, BUF_PITCH).astype(BF16)

    _gather_step(i, n_steps, src_hbm, tok_ref, next_tok_ref, bufs, sems, rows,
                 valid, (i + 1) * rows < used_ref[0], consume)

    @pl.when(jnp.logical_not(valid))
    def _():
        o_ref[...] = jnp.zeros_like(o_ref)


def moe_dispatch(xn, src_token, used_rows, *, rows):
    n = src_token.shape[0]
    nchunk = n // rows
    tokens = src_token.reshape(nchunk, 1, rows)
    return pl.pallas_call(
        functools.partial(_dispatch_kernel, rows=rows, n_steps=nchunk),
        grid_spec=pltpu.PrefetchScalarGridSpec(
            num_scalar_prefetch=1,
            grid=(nchunk,),
            in_specs=_gather_specs(nchunk, rows) + [pl.BlockSpec(memory_space=pl.ANY)],
            out_specs=pl.BlockSpec((rows, D_MODEL), lambda i, used: (i, 0)),
            scratch_shapes=_gather_scratch(rows)),
        out_shape=jax.ShapeDtypeStruct((n, D_MODEL), BF16),
        compiler_params=_cparams(1),
        name="moe_dispatch",
    )(used_rows, tokens, tokens, xn)


STEP_IDLE, STEP_COMPUTE, STEP_ZERO = 0, 1, 2


def _moe_up_kernel(xt_ref, ot_ref, oj_ref, we_ref, wj_ref, cast_ref, mode_ref,
                   x_ref, w1_ref, w3_ref, w2_ref, o_ref, w2b_ref, w1b_ref, w3b_ref, *, n_w2_blocks):
    s = pl.program_id(0)

    @pl.when(cast_ref[s] > 0)
    def _():
        w1b_ref[...] = w1_ref[0].astype(BF16)
        w3b_ref[...] = w3_ref[0].astype(BF16)

    @pl.when(s < n_w2_blocks)
    def _():
        w2b_ref[...] = w2_ref[...].astype(BF16)

    @pl.when(mode_ref[s] == STEP_COMPUTE)
    def _():
        x = x_ref[...]
        a = _dot(x, w1b_ref[...])
        b = _dot(x, w3b_ref[...])
        o_ref[...] = (jax.nn.silu(a) * b).astype(BF16)

    @pl.when(mode_ref[s] == STEP_ZERO)
    def _():
        o_ref[...] = jnp.zeros_like(o_ref)


def _moe_up_steps(tiles_per_expert, first_tile, *, n_tiles, nj):
    n_steps = (n_tiles + N_EXPERTS) * nj
    used = jnp.sum(tiles_per_expert)
    n_compute = used * nj
    n_zero = (n_tiles - used) * nj
    empty = tiles_per_expert == 0
    n_cast_only = jnp.sum(empty.astype(jnp.int32)) * nj
    n_live = n_compute + n_zero + n_cast_only
    a = jnp.minimum(jnp.arange(n_steps, dtype=jnp.int32), n_live - 1)
    blk_end = jnp.cumsum(tiles_per_expert * nj)
    e_c = jnp.minimum(jnp.sum((a[:, None] >= blk_end[None, :]).astype(jnp.int32), axis=1), N_EXPERTS - 1)
    onehot_c = (e_c[:, None] == jnp.arange(N_EXPERTS, dtype=jnp.int32)[None, :]).astype(jnp.int32)
    pick = lambda v: jnp.sum(onehot_c * v[None, :], axis=1)
    within = a - (pick(blk_end) - pick(tiles_per_expert) * nj)
    t_e = jnp.maximum(pick(tiles_per_expert), 1)
    j_c = within // t_e
    r_c = within % t_e
    tile_c = pick(first_tile) + r_c
    z = a - n_compute
    tile_z = used + z // nj
    j_z = z % nj
    b = a - n_compute - n_zero
    empty_rank = jnp.cumsum(empty.astype(jnp.int32)) - 1
    k = b // nj
    e_b = jnp.sum(jnp.where(empty[None, :] & (empty_rank[None, :] == k[:, None]),
                            jnp.arange(N_EXPERTS, dtype=jnp.int32)[None, :], 0), axis=1)
    j_b = b % nj
    is_c = a < n_compute
    is_z = (~is_c) & (a < n_compute + n_zero)
    is_b = (~is_c) & (~is_z)
    live = jnp.arange(n_steps, dtype=jnp.int32) < n_live
    last_tile = used - 1
    last_e = jnp.sum((last_tile >= jnp.cumsum(tiles_per_expert)).astype(jnp.int32))
    x_tile = jnp.where(is_c, tile_c, last_tile)
    o_tile = jnp.where(is_c, tile_c, jnp.where(is_z, tile_z, n_tiles - 1))
    o_col = jnp.where(is_c, j_c, jnp.where(is_z, j_z, nj - 1))
    w_e = jnp.where(is_c, e_c, jnp.where(is_b, e_b, last_e))
    w_col = jnp.where(is_c, j_c, jnp.where(is_b, j_b, nj - 1))
    cast = (live & ((is_c & (r_c == 0)) | is_b)).astype(jnp.int32)
    mode = jnp.where(live & is_c, STEP_COMPUTE, jnp.where(live & is_z, STEP_ZERO, STEP_IDLE))
    tables = (x_tile, o_tile, o_col, w_e, w_col, cast, mode)
    return tuple(t.astype(jnp.int32) for t in tables), n_steps


def moe_up(xs, w1, w3, w2, tiles_per_expert, first_tile, *, tm, tn):
    n, d = xs.shape
    f = w1.shape[2]
    nj = f // tn
    tables, n_steps = _moe_up_steps(tiles_per_expert, first_tile, n_tiles=n // tm, nj=nj)

    def x_map(s, xt, ot, oj, we, wj, cast, mode):
        return (xt[s], 0)

    def w_map(s, xt, ot, oj, we, wj, cast, mode):
        return (we[s], 0, wj[s])

    def o_map(s, xt, ot, oj, we, wj, cast, mode):
        return (ot[s], oj[s])

    w2_slabs = w2.reshape(-1, W2_SLAB_ROWS, d)
    n_slabs = w2_slabs.shape[0]
    assert n_slabs <= n_steps

    def w2_map(s, *_):
        return (jnp.minimum(s, n_slabs - 1), 0, 0)

    h, w2b = pl.pallas_call(
        functools.partial(_moe_up_kernel, n_w2_blocks=n_slabs),
        grid_spec=pltpu.PrefetchScalarGridSpec(
            num_scalar_prefetch=len(tables),
            grid=(n_steps,),
            in_specs=[pl.BlockSpec((tm, d), x_map),
                      pl.BlockSpec((1, d, tn), w_map),
                      pl.BlockSpec((1, d, tn), w_map),
                      pl.BlockSpec((1, W2_SLAB_ROWS, d), w2_map)],
            out_specs=[pl.BlockSpec((tm, tn), o_map),
                       pl.BlockSpec((1, W2_SLAB_ROWS, d), w2_map)],
            scratch_shapes=[pltpu.VMEM((d, tn), BF16), pltpu.VMEM((d, tn), BF16)]),
        out_shape=[jax.ShapeDtypeStruct((n, f), BF16),
                   jax.ShapeDtypeStruct(w2_slabs.shape, BF16)],
        compiler_params=_cparams(1),
        name="moe_up",
    )(*tables, xs, w1, w3, w2_slabs)
    return h, w2b.reshape(w2.shape)


def _weight_map(nj):
    def w_map(t, j, te, tv):
        return (te[t], 0, jnp.where(tv[t] > 0, j, nj - 1))

    return w_map


def _moe_down_kernel(te_ref, tv_ref, h_ref, w2_ref, o_ref, *, tm, tn):
    j = pl.program_id(1)
    groups = tn // LANES

    def put(val):
        for c in range(groups):
            o_ref[pl.ds(j * groups + c, tm, stride=ROW_SUB), :] = val[:, c * LANES:(c + 1) * LANES]

    @pl.when(tv_ref[pl.program_id(0)] > 0)
    def _():
        put(_dot(h_ref[...], w2_ref[0]))

    @pl.when(tv_ref[pl.program_id(0)] == 0)
    def _():
        put(jnp.zeros((tm, tn), F32))


def moe_down(h, w2, tiles, *, tm, tn):
    n, f = h.shape
    d = w2.shape[2]
    nj = d // tn
    return pl.pallas_call(
        functools.partial(_moe_down_kernel, tm=tm, tn=tn),
        grid_spec=pltpu.PrefetchScalarGridSpec(
            num_scalar_prefetch=2,
            grid=(n // tm, nj),
            in_specs=[pl.BlockSpec((tm, f), lambda t, j, te, tv: (t, 0)),
                      pl.BlockSpec((1, f, tn), _weight_map(nj))],
            out_specs=pl.BlockSpec((tm * ROW_SUB, LANES), lambda t, j, te, tv: (t, 0))),
        out_shape=jax.ShapeDtypeStruct((n * ROW_SUB, LANES), F32),
        compiler_params=_cparams(2),
        name="moe_down",
    )(*tiles, h, w2)


def _combine_kernel(pos_ref, next_pos_ref, xp_ref, xs_ref, w_ref, g_ref, ys_hbm, yp_ref, ysmp_ref, bufs, sems,
                    *, rows, n_prompt_chunks, n_steps):
    i = pl.program_id(0)
    w0 = w_ref[:, 0:1]
    w1 = w_ref[:, 1:2]

    def finish(buf, x_ref, o_ref):
        for c in range(ROW_SUB):
            a = _from_row_major(buf, 0, rows, c, BUF_PITCH)
            b = _from_row_major(buf, rows * BUF_PITCH, rows, c, BUF_PITCH)
            o_ref[:, c * LANES:(c + 1) * LANES] = x_ref[:, c * LANES:(c + 1) * LANES] + (w0 * a + w1 * b)
        o_ref[...] = _rms(o_ref[...], g_ref[...])

    def consume(buf):
        @pl.when(i < n_prompt_chunks)
        def _():
            finish(buf, xp_ref, yp_ref)

        @pl.when(i >= n_prompt_chunks)
        def _():
            finish(buf, xs_ref, ysmp_ref)

    every_step = i >= 0
    _gather_step(i, n_steps, ys_hbm, pos_ref, next_pos_ref, bufs, sems, TOP_K * rows,
                 every_step, every_step, consume)


def moe_combine(xp, xs, wgt, g_final, ys, dest, *, rows):
    m_prompt, d = xp.shape
    m = m_prompt + xs.shape[0]
    nchunk = m // rows
    npc = m_prompt // rows
    pos = dest.reshape(nchunk, rows, TOP_K).transpose(0, 2, 1).reshape(nchunk, 1, TOP_K * rows)

    def p_map(i):
        return (jnp.minimum(i, npc - 1), 0)

    def s_map(i):
        return (jnp.maximum(i - npc, 0), 0)

    return pl.pallas_call(
        functools.partial(_combine_kernel, rows=rows, n_prompt_chunks=npc, n_steps=nchunk),
        grid=(nchunk,),
        in_specs=_gather_specs(nchunk, TOP_K * rows) + [
                  pl.BlockSpec((rows, d), p_map),
                  pl.BlockSpec((rows, d), s_map),
                  pl.BlockSpec((rows, ROUTER_PAD), lambda i: (i, 0)),
                  pl.BlockSpec((1, d), lambda i: (0, 0)),
                  pl.BlockSpec(memory_space=pl.ANY)],
        out_specs=[pl.BlockSpec((rows, d), p_map), pl.BlockSpec((rows, d), s_map)],
        out_shape=[jax.ShapeDtypeStruct((m_prompt, d), F32),
                   jax.ShapeDtypeStruct((m - m_prompt, d), F32)],
        scratch_shapes=_gather_scratch(TOP_K * rows),
        compiler_params=_cparams(1),
        name="moe_combine",
    )(pos, pos, xp, xs, wgt, g_final, ys)


def _routing_tables(idx2, *, tm, n_rows):
    e_flat = idx2.reshape(-1)
    onehot = (e_flat[:, None] == jnp.arange(N_EXPERTS, dtype=jnp.int32)[None, :]).astype(jnp.int32)
    csum = jnp.cumsum(onehot, axis=0)
    counts = csum[-1]
    rank = jnp.sum(csum * onehot, axis=1) - 1
    padded = ((counts + tm - 1) // tm) * tm
    ends = jnp.cumsum(padded)
    starts = ends - padded
    dest = (jnp.sum(onehot * starts[None, :], axis=1) + rank).astype(jnp.int32)
    token = jnp.arange(e_flat.shape[0], dtype=jnp.int32) // TOP_K
    src_token = jnp.zeros((n_rows,), jnp.int32).at[dest].set(token)
    tile_ids = jnp.arange(n_rows // tm, dtype=jnp.int32)
    used_tiles = ends[-1] // tm
    tile_valid = (tile_ids < used_tiles).astype(jnp.int32)
    tile_start = jnp.minimum(tile_ids, used_tiles - 1) * tm
    tile_expert = jnp.sum((tile_start[:, None] >= ends[None, :]).astype(jnp.int32), axis=1)
    per_expert = ((padded // tm).astype(jnp.int32), (starts // tm).astype(jnp.int32))
    return dest.reshape(-1, TOP_K), src_token, ends[-1:].astype(jnp.int32), (tile_expert, tile_valid), per_expert


def _rel_bias_by_distance(rel_bias_table):
    d = jnp.arange(WINDOW, dtype=jnp.int32)
    max_exact = REL_BUCKETS // 2
    d_f = jnp.maximum(d, 1).astype(F32)
    large = max_exact + (jnp.log(d_f / max_exact) / math.log(REL_MAX_DIST / max_exact)
                         * (REL_BUCKETS - max_exact)).astype(jnp.int32)
    large = jnp.minimum(large, REL_BUCKETS - 1)
    bucket = jnp.where(d < max_exact, d, large)
    return rel_bias_table[bucket].T


def _mixer_prompt(x, mem2d, wts, lw, layer, bias2, *, batch, seq):
    kv = norm_matmul(mem2d, lw["g_mem"], wts["w_mem_kv"], layer, tm=1024, tn=512)
    z = norm_matmul(x, lw["g_attn"], wts["w_in"], layer, tm=1024, tn=512, n_cols=G_OFF)
    ya = swa_prompt(z, bias2, lw["sinks"], batch=batch, seq=seq)
    yp = pool_prompt(z, lw["pool_maps"], lw["pool_scale"], batch=batch, seq=seq, tt=512)
    ym = mem_prompt(z, kv, batch=batch, seq=seq, tq=512)
    merged = merge_branches(x, lw["g_attn"], ya, yp, ym, wts["w_up_attn"], wts["w_up_pool"], wts["w_up_mem"],
                            wts["w_in"], layer, tm=1024, tn=512)
    x = matmul_residual(merged, wts["w_o"], layer, x, tm=1024, tn=512)
    z3 = z.reshape(batch, seq, G_OFF)
    new_k = z3[:, seq - WINDOW:, K_OFF:V_OFF].reshape(batch, WINDOW, N_KV_HEADS, HEAD_DIM)
    new_v = z3[:, seq - WINDOW:, V_OFF:P_OFF].reshape(batch, WINDOW, N_KV_HEADS, HEAD_DIM)
    new_pool = z3[:, seq - POOL_BUF:, P_OFF:M_OFF]
    kv3 = kv.reshape(batch, N_MEM, 2 * MEM_WIDTH)
    mk = kv3[..., :MEM_WIDTH].reshape(batch, N_MEM, MEM_HEADS, MEM_HEAD_DIM)
    mv = kv3[..., MEM_WIDTH:].reshape(batch, N_MEM, MEM_HEADS, MEM_HEAD_DIM)
    return x, new_k, new_v, new_pool, mk, mv


def _mixer_sample(x, cache_k, cache_v, state, mem_k, mem_v, wts, lw, layer, bias_s):
    nb = x.shape[0]
    z = norm_matmul(x, lw["g_attn"], wts["w_in"], layer, tm=nb, tn=512, n_cols=G_OFF)
    q3 = z[:, Q_OFF:K_OFF].reshape(nb, N_HEADS, HEAD_DIM)
    ya3, new_k, new_v = swa_sample(q3, z, cache_k.reshape(nb, WINDOW, KV_WIDTH),
                                   cache_v.reshape(nb, WINDOW, KV_WIDTH), bias_s, lw["sinks"][:, None], bb=8)
    yp, new_pool = pool_sample(z, state, lw["pool_maps"], lw["pool_scale"])
    ym = mem_sample(z, mem_k, mem_v, layer, bb=8)
    ya = ya3.reshape(nb, ATTN_WIDTH)
    merged = merge_branches(x, lw["g_attn"], ya, yp, ym, wts["w_up_attn"], wts["w_up_pool"], wts["w_up_mem"],
                            wts["w_in"], layer, tm=nb, tn=512)
    x = matmul_residual(merged, wts["w_o"], layer, x, tm=nb, tn=512)
    return (x, new_k.reshape(nb, WINDOW, N_KV_HEADS, HEAD_DIM),
            new_v.reshape(nb, WINDOW, N_KV_HEADS, HEAD_DIM), new_pool)


def _dense_ffn(x, g, w1, w3, w2, layer, *, tm):
    h = norm_swiglu(x, g, w1, w3, layer, tm=tm, tn=512)
    return matmul_residual(h, w2, layer, x, tm=tm, tn=512)


def _moe_ffn_final(xp, xs, g, router, w1, w3, w2, g_final):
    m_prompt = xp.shape[0]
    m = m_prompt + xs.shape[0]
    router_pad = jnp.pad(router, ((0, 0), (0, ROUTER_PAD - N_EXPERTS)))
    xn, idx_pad, wgt = norm_router(xp, xs, g, router_pad, tm=COMBINE_ROWS)
    n_rows = pl.cdiv(TOP_K * m, MOE_TM) * MOE_TM + N_EXPERTS * MOE_TM
    dest, src_token, used_rows, tiles, per_expert = _routing_tables(idx_pad[:, :TOP_K], tm=MOE_TM, n_rows=n_rows)
    xsort = moe_dispatch(xn, src_token, used_rows, rows=DISPATCH_ROWS)
    h, w2b = moe_up(xsort, w1, w3, w2, *per_expert, tm=MOE_TM, tn=512)
    ys = moe_down(h, w2b, tiles, tm=MOE_TM, tn=512)
    return moe_combine(xp, xs, wgt, g_final, ys, dest, rows=COMBINE_ROWS)


def kernel(x_prompt, x_sample, mem_prompt, cache_swa_k, cache_swa_v, state_pool, cache_mem_k, cache_mem_v, rel_bias_table, g_attn, w_in, attn_sinks, pool_maps, pool_scale, g_mem, w_mem_kv, w_up_attn, w_up_pool, w_up_mem, w_o, g_ffn, ffn_w1, ffn_w3, ffn_w2, moe_router, moe_w1, moe_w3, moe_w2, g_final):
    batch, seq, d = x_prompt.shape
    nb_s = x_sample.shape[0]
    depth = w_in.shape[0]
    assert depth == 2 and x_sample.shape[1] == 1

    bias_hw = _rel_bias_by_distance(rel_bias_table).astype(F32)
    col = jnp.arange(2 * WINDOW)
    bias2 = bias_hw[:, jnp.clip(WINDOW - col, 0, WINDOW - 1)]
    bias_s = bias_hw[:, ::-1]

    wts = dict(w_in=w_in.astype(BF16), w_mem_kv=w_mem_kv.astype(BF16), w_up_attn=w_up_attn.astype(BF16),
               w_up_pool=w_up_pool.astype(BF16), w_up_mem=w_up_mem.astype(BF16), w_o=w_o.astype(BF16))
    ffn = (ffn_w1.astype(BF16), ffn_w3.astype(BF16), ffn_w2.astype(BF16))

    xp = x_prompt.reshape(batch * seq, d)
    xs = x_sample.reshape(nb_s, d)
    mem2d = mem_prompt.reshape(batch * N_MEM, d)
    p_k, p_v, p_pool, p_mk, p_mv, s_k, s_v, s_pool = [], [], [], [], [], [], [], []
    for l in range(depth):
        lw = dict(g_attn=g_attn[l][None, :], sinks=attn_sinks[l], pool_maps=pool_maps[l].astype(BF16),
                  pool_scale=pool_scale[l][None, :], g_mem=g_mem[l][None, :])
        xp, nk, nv, npool, mk, mv = _mixer_prompt(xp, mem2d, wts, lw, l, bias2, batch=batch, seq=seq)
        p_k.append(nk); p_v.append(nv); p_pool.append(npool); p_mk.append(mk); p_mv.append(mv)
        xs, nk, nv, npool = _mixer_sample(xs, cache_swa_k[l], cache_swa_v[l], state_pool[l],
                                          cache_mem_k, cache_mem_v, wts, lw, l, bias_s)
        s_k.append(nk); s_v.append(nv); s_pool.append(npool)
        gl = g_ffn[l][None, :]
        i = l // 2
        if l % 2 == 0:
            xp = _dense_ffn(xp, gl, *ffn, i, tm=1024)
            xs = _dense_ffn(xs, gl, *ffn, i, tm=nb_s)
        else:
            yp, ys = _moe_ffn_final(xp, xs, gl, moe_router[i], moe_w1[i], moe_w3[i], moe_w2[i], g_final[None, :])
    y_prompt = yp.reshape(batch, seq, d)
    y_sample = ys.reshape(nb_s, 1, d)
    return (y_prompt, y_sample,
            jnp.stack(p_k), jnp.stack(p_v), jnp.stack(p_pool), jnp.stack(p_mk), jnp.stack(p_mv),
            jnp.stack(s_k), jnp.stack(s_v), jnp.stack(s_pool))
```

```python
import functools
import math

import jax
import jax.numpy as jnp
from jax import lax
from jax.experimental import pallas as pl
from jax.experimental.pallas import tpu as pltpu

F32 = jnp.float32
BF16 = jnp.bfloat16

D_MODEL = 2048
N_HEADS = 16
N_KV_HEADS = 4
HEAD_DIM = 64
Q_GROUP = N_HEADS // N_KV_HEADS
WINDOW = 128
ATTN_WIDTH = N_HEADS * HEAD_DIM
KV_WIDTH = N_KV_HEADS * HEAD_DIM
REL_BUCKETS = 32
REL_MAX_DIST = 128
POOL_WINDOWS = (2, 4, 8, 16)
N_POOL_GROUPS = 4
POOL_CH = D_MODEL // 4
POOL_GROUP_CH = POOL_CH // N_POOL_GROUPS
POOL_BUF = max(POOL_WINDOWS) - 1
N_MEM = 256
MEM_HEADS = 4
MEM_HEAD_DIM = 128
MEM_WIDTH = MEM_HEADS * MEM_HEAD_DIM
N_BRANCHES = 3
Q_OFF = 0
K_OFF = Q_OFF + ATTN_WIDTH
V_OFF = K_OFF + KV_WIDTH
P_OFF = V_OFF + KV_WIDTH
M_OFF = P_OFF + POOL_CH
G_OFF = M_OFF + MEM_WIDTH
IN_COLS = G_OFF + N_BRANCHES * D_MODEL
N_EXPERTS = 8
TOP_K = 2
RMS_EPS = 1e-5
NEG_INF = -1e30
PAST_LEN = 8192

VMEM_LIMIT_BYTES = 56 * 1024 * 1024
LANES = 128
POOL_HALO = 16
ROUTER_PAD = LANES
ROW_SUB = D_MODEL // LANES
BUF_PITCH = ROW_SUB + 8
GATHER_UNROLL = 8
EPILOGUE_ROWS = 512
W2_SLAB_ROWS = 64
MOE_TM = 512
DISPATCH_ROWS = 256
COMBINE_ROWS = 128


def _cparams(n_axes):
    return pltpu.CompilerParams(dimension_semantics=("arbitrary",) * n_axes,
                                vmem_limit_bytes=VMEM_LIMIT_BYTES)


def _rms(x, g):
    return (x * lax.rsqrt(jnp.mean(x * x, axis=-1, keepdims=True) + RMS_EPS)) * g


_NN = (((1,), (0,)), ((), ()))
_NT = (((1,), (1,)), ((), ()))


def _dot(a, b, dims=_NN):
    return lax.dot_general(a.astype(BF16), b.astype(BF16), dims, preferred_element_type=F32)


def _row_chunks(rows):
    n = max(1, rows // EPILOGUE_ROWS)
    return [slice(k * (rows // n), (k + 1) * (rows // n)) for k in range(n)]


def _layer_cols(layer, rows, tn):
    return pl.BlockSpec((None, rows, tn), lambda i, j: (layer, 0, j))


def _norm_mm_kernel(x_ref, g_ref, w_ref, o_ref, xn_ref):
    @pl.when(pl.program_id(1) == 0)
    def _():
        xn_ref[...] = _rms(x_ref[...], g_ref[...]).astype(BF16)

    for rs in _row_chunks(o_ref.shape[0]):
        o_ref[rs, :] = _dot(xn_ref[rs, :], w_ref[...])


def norm_matmul(x, g, w, layer, *, tm, tn, n_cols=None):
    m, d = x.shape
    n = w.shape[2] if n_cols is None else n_cols
    return pl.pallas_call(
        _norm_mm_kernel,
        grid=(m // tm, n // tn),
        in_specs=[pl.BlockSpec((tm, d), lambda i, j: (i, 0)),
                  pl.BlockSpec((1, d), lambda i, j: (0, 0)),
                  _layer_cols(layer, d, tn)],
        out_specs=pl.BlockSpec((tm, tn), lambda i, j: (i, j)),
        out_shape=jax.ShapeDtypeStruct((m, n), F32),
        scratch_shapes=[pltpu.VMEM((tm, d), BF16)],
        compiler_params=_cparams(2),
        name="norm_matmul",
    )(x, g, w)


def _norm_swiglu_kernel(x_ref, g_ref, w1_ref, w3_ref, o_ref, xn_ref):
    @pl.when(pl.program_id(1) == 0)
    def _():
        xn_ref[...] = _rms(x_ref[...], g_ref[...]).astype(BF16)

    for rs in _row_chunks(o_ref.shape[0]):
        xn = xn_ref[rs, :]
        a = _dot(xn, w1_ref[...])
        b = _dot(xn, w3_ref[...])
        o_ref[rs, :] = (jax.nn.silu(a) * b).astype(BF16)


def norm_swiglu(x, g, w1, w3, layer, *, tm, tn):
    m, d = x.shape
    n = w1.shape[2]
    return pl.pallas_call(
        _norm_swiglu_kernel,
        grid=(m // tm, n // tn),
        in_specs=[pl.BlockSpec((tm, d), lambda i, j: (i, 0)),
                  pl.BlockSpec((1, d), lambda i, j: (0, 0)),
                  _layer_cols(layer, d, tn), _layer_cols(layer, d, tn)],
        out_specs=pl.BlockSpec((tm, tn), lambda i, j: (i, j)),
        out_shape=jax.ShapeDtypeStruct((m, n), BF16),
        scratch_shapes=[pltpu.VMEM((tm, d), BF16)],
        compiler_params=_cparams(2),
        name="norm_swiglu",
    )(x, g, w1, w3)


def _mm_res_kernel(a_ref, w_ref, r_ref, o_ref):
    for rs in _row_chunks(o_ref.shape[0]):
        o_ref[rs, :] = r_ref[rs, :] + _dot(a_ref[rs, :], w_ref[...])


def matmul_residual(a, w, layer, res, *, tm, tn):
    m, k = a.shape
    n = w.shape[2]
    return pl.pallas_call(
        _mm_res_kernel,
        grid=(m // tm, n // tn),
        in_specs=[pl.BlockSpec((tm, k), lambda i, j: (i, 0)),
                  _layer_cols(layer, k, tn),
                  pl.BlockSpec((tm, tn), lambda i, j: (i, j))],
        out_specs=pl.BlockSpec((tm, tn), lambda i, j: (i, j)),
        out_shape=jax.ShapeDtypeStruct((m, n), F32),
        compiler_params=_cparams(2),
        name="matmul_residual",
    )(a, w, res)


def _merge_kernel(x_ref, g_ref, ya_ref, yp_ref, ym_ref, wa_ref, wp_ref, wm_ref, wg0_ref, wg1_ref, wg2_ref,
                  o_ref, xn_ref):
    @pl.when(pl.program_id(1) == 0)
    def _():
        xn_ref[...] = _rms(x_ref[...], g_ref[...]).astype(BF16)

    for rs in _row_chunks(o_ref.shape[0]):
        xn = xn_ref[rs, :]
        acc = jax.nn.sigmoid(_dot(xn, wg0_ref[...])) * _dot(ya_ref[rs, :], wa_ref[...])
        acc = acc + jax.nn.sigmoid(_dot(xn, wg1_ref[...])) * _dot(yp_ref[rs, :], wp_ref[...])
        acc = acc + jax.nn.sigmoid(_dot(xn, wg2_ref[...])) * _dot(ym_ref[rs, :], wm_ref[...])
        o_ref[rs, :] = acc.astype(BF16)


def merge_branches(x, g, ya, yp, ym, wa, wp, wm, w_in, layer, *, tm, tn):
    m, d = x.shape
    nj = D_MODEL // tn
    g_blk = G_OFF // tn

    def gate_cols(b):
        return pl.BlockSpec((None, d, tn), lambda i, j: (layer, 0, g_blk + b * nj + j))

    return pl.pallas_call(
        _merge_kernel,
        grid=(m // tm, nj),
        in_specs=[pl.BlockSpec((tm, d), lambda i, j: (i, 0)),
                  pl.BlockSpec((1, d), lambda i, j: (0, 0)),
                  pl.BlockSpec((tm, ATTN_WIDTH), lambda i, j: (i, 0)),
                  pl.BlockSpec((tm, POOL_CH), lambda i, j: (i, 0)),
                  pl.BlockSpec((tm, MEM_WIDTH), lambda i, j: (i, 0)),
                  _layer_cols(layer, ATTN_WIDTH, tn), _layer_cols(layer, POOL_CH, tn),
                  _layer_cols(layer, MEM_WIDTH, tn),
                  gate_cols(0), gate_cols(1), gate_cols(2)],
        out_specs=pl.BlockSpec((tm, tn), lambda i, j: (i, j)),
        out_shape=jax.ShapeDtypeStruct((m, D_MODEL), BF16),
        scratch_shapes=[pltpu.VMEM((tm, d), BF16)],
        compiler_params=_cparams(2),
        name="merge_branches",
    )(x, g, ya, yp, ym, wa, wp, wm, w_in, w_in, w_in)


def _swa_prompt_kernel(q_ref, kp_ref, kc_ref, vp_ref, vc_ref, base_ref, sink_ref, o_ref, bias_ref):
    i = pl.program_id(1)

    @pl.when((pl.program_id(0) == 0) & (i == 0))
    def _():
        for h in range(N_HEADS):
            row0 = jnp.broadcast_to(base_ref[h:h + 1, :], (WINDOW, 2 * WINDOW))
            bias_ref[h] = pltpu.roll(row0, 0, 1, stride=1, stride_axis=0)

    q = q_ref[...].astype(BF16)
    k2 = jnp.concatenate([kp_ref[...], kc_ref[...]], axis=0).astype(BF16)
    v2 = jnp.concatenate([vp_ref[...], vc_ref[...]], axis=0).astype(BF16)
    row = lax.broadcasted_iota(jnp.int32, (WINDOW, 2 * WINDOW), 0)
    col = lax.broadcasted_iota(jnp.int32, (WINDOW, 2 * WINDOW), 1)
    dist = row + WINDOW - col
    valid = (dist >= 0) & (dist < WINDOW) & ((i > 0) | (col >= WINDOW))
    for h in range(N_HEADS):
        kv = h // Q_GROUP
        qh = q[:, h * HEAD_DIM:(h + 1) * HEAD_DIM]
        kh = k2[:, kv * HEAD_DIM:(kv + 1) * HEAD_DIM]
        vh = v2[:, kv * HEAD_DIM:(kv + 1) * HEAD_DIM]
        logits = _dot(qh, kh, _NT) * (HEAD_DIM ** -0.5)
        logits = jnp.where(valid, logits + bias_ref[h], NEG_INF)
        s = sink_ref[h]
        m = jnp.maximum(jnp.max(logits, axis=-1, keepdims=True), s)
        p = jnp.exp(logits - m)
        denom = jnp.sum(p, axis=-1, keepdims=True) + jnp.exp(s - m)
        o = _dot(p * (1.0 / denom), vh)
        o_ref[:, h * HEAD_DIM:(h + 1) * HEAD_DIM] = o.astype(BF16)


def swa_prompt(z, bias_base, sinks, *, batch, seq):
    nb = seq // WINDOW
    kblk = K_OFF // KV_WIDTH
    vblk = V_OFF // KV_WIDTH

    def cur(c):
        return pl.BlockSpec((WINDOW, KV_WIDTH), lambda b, i: (b * nb + i, c))

    def prev(c):
        return pl.BlockSpec((WINDOW, KV_WIDTH), lambda b, i: (b * nb + jnp.maximum(i - 1, 0), c))

    return pl.pallas_call(
        _swa_prompt_kernel,
        grid=(batch, nb),
        in_specs=[pl.BlockSpec((WINDOW, ATTN_WIDTH), lambda b, i: (b * nb + i, 0)),
                  prev(kblk), cur(kblk), prev(vblk), cur(vblk),
                  pl.BlockSpec((N_HEADS, 2 * WINDOW), lambda b, i: (0, 0)),
                  pl.BlockSpec(memory_space=pltpu.SMEM)],
        out_specs=pl.BlockSpec((WINDOW, ATTN_WIDTH), lambda b, i: (b * nb + i, 0)),
        out_shape=jax.ShapeDtypeStruct((batch * seq, ATTN_WIDTH), BF16),
        scratch_shapes=[pltpu.VMEM((N_HEADS, WINDOW, 2 * WINDOW), F32)],
        compiler_params=_cparams(2),
        name="swa_prompt",
    )(z, z, z, z, z, bias_base, sinks)


def _swa_sample_kernel(q_ref, kn_ref, vn_ref, ck_ref, cv_ref, bias_ref, sink_ref,
                       o_ref, ok_ref, ov_ref, *, bb):
    row = lax.broadcasted_iota(jnp.int32, (WINDOW, KV_WIDTH), 0)
    hrow = lax.broadcasted_iota(jnp.int32, (N_HEADS, KV_WIDTH), 0)
    hcol = lax.broadcasted_iota(jnp.int32, (N_HEADS, KV_WIDTH), 1)
    own = (hcol // HEAD_DIM) == (hrow // Q_GROUP)
    s = sink_ref[...]
    for b in range(bb):
        newk = jnp.where(row == WINDOW - 1, kn_ref[b:b + 1, :], pltpu.roll(ck_ref[b], WINDOW - 1, 0))
        newv = jnp.where(row == WINDOW - 1, vn_ref[b:b + 1, :], pltpu.roll(cv_ref[b], WINDOW - 1, 0))
        ok_ref[b] = newk
        ov_ref[b] = newv
        qb = q_ref[b]
        qbd = jnp.where(own, jnp.concatenate([qb] * N_KV_HEADS, axis=1), 0.0)
        logits = _dot(qbd, newk, _NT) * (HEAD_DIM ** -0.5)
        logits = logits + bias_ref[...]
        m = jnp.maximum(jnp.max(logits, axis=-1, keepdims=True), s)
        p = jnp.exp(logits - m)
        denom = jnp.sum(p, axis=-1, keepdims=True) + jnp.exp(s - m)
        o = jnp.where(own, _dot(p * (1.0 / denom), newv), 0.0)
        acc = o[:, 0:HEAD_DIM]
        for c in range(1, N_KV_HEADS):
            acc = acc + o[:, c * HEAD_DIM:(c + 1) * HEAD_DIM]
        o_ref[b] = acc.astype(BF16)


def swa_sample(q3, zs, cache_k, cache_v, layer, bias_s, sinks_col, *, bb):
    nbatch = q3.shape[0]
    kblk = K_OFF // KV_WIDTH
    vblk = V_OFF // KV_WIDTH
    cache_spec = pl.BlockSpec((bb, WINDOW, KV_WIDTH), lambda i: (i, 0, 0))
    cache_in = pl.BlockSpec((None, bb, WINDOW, KV_WIDTH), lambda i: (layer, i, 0, 0))
    return pl.pallas_call(
        functools.partial(_swa_sample_kernel, bb=bb),
        grid=(nbatch // bb,),
        in_specs=[pl.BlockSpec((bb, N_HEADS, HEAD_DIM), lambda i: (i, 0, 0)),
                  pl.BlockSpec((bb, KV_WIDTH), lambda i: (i, kblk)),
                  pl.BlockSpec((bb, KV_WIDTH), lambda i: (i, vblk)),
                  cache_in, cache_in,
                  pl.BlockSpec((N_HEADS, WINDOW), lambda i: (0, 0)),
                  pl.BlockSpec((N_HEADS, 1), lambda i: (0, 0))],
        out_specs=[pl.BlockSpec((bb, N_HEADS, HEAD_DIM), lambda i: (i, 0, 0)), cache_spec, cache_spec],
        out_shape=[jax.ShapeDtypeStruct((nbatch, N_HEADS, HEAD_DIM), BF16),
                   jax.ShapeDtypeStruct((nbatch, WINDOW, KV_WIDTH), F32),
                   jax.ShapeDtypeStruct((nbatch, WINDOW, KV_WIDTH), F32)],
        compiler_params=_cparams(1),
        name="swa_sample",
    )(q3, zs, zs, cache_k, cache_v, bias_s, sinks_col)


def _pool_mix(sums, p, cnts, maps_ref, scale_ref, o_ref):
    for g in range(N_POOL_GROUPS):
        c0 = g * POOL_GROUP_CH
        c1 = c0 + POOL_GROUP_CH
        diff = sums[g] / cnts[g] - p[:, c0:c1]
        mixed = _dot(diff, maps_ref[g])
        o_ref[:, c0:c1] = (mixed * scale_ref[:, c0:c1]).astype(BF16)


def _pool_prompt_kernel(p_ref, halo_ref, maps_ref, scale_ref, o_ref, *, tt):
    i = pl.program_id(1)
    p = p_ref[...]
    halo = jnp.where(i == 0, 0.0, halo_ref[...])
    ext = jnp.concatenate([halo, p], axis=0)
    s = ext
    sums = []
    shift = 1
    for g in range(N_POOL_GROUPS):
        s = s[:, (POOL_GROUP_CH if g > 0 else 0):]
        s = s + pltpu.roll(s, shift, 0)
        shift *= 2
        sums.append(s[POOL_HALO:, 0:POOL_GROUP_CH])
    pos = i * tt + lax.broadcasted_iota(jnp.int32, (tt, 1), 0)
    cnts = [jnp.minimum(pos + 1, w).astype(F32) for w in POOL_WINDOWS]
    _pool_mix(sums, p, cnts, maps_ref, scale_ref, o_ref)


def pool_prompt(z, maps, scale, *, batch, seq, tt):
    nt = seq // tt
    pblk = P_OFF // POOL_CH
    hb = tt // POOL_HALO
    return pl.pallas_call(
        functools.partial(_pool_prompt_kernel, tt=tt),
        grid=(batch, nt),
        in_specs=[pl.BlockSpec((tt, POOL_CH), lambda b, i: (b * nt + i, pblk)),
                  pl.BlockSpec((POOL_HALO, POOL_CH),
                               lambda b, i: (jnp.maximum((b * nt + i) * hb - 1, 0), pblk)),
                  pl.BlockSpec((N_POOL_GROUPS, POOL_GROUP_CH, POOL_GROUP_CH), lambda b, i: (0, 0, 0)),
                  pl.BlockSpec((1, POOL_CH), lambda b, i: (0, 0))],
        out_specs=pl.BlockSpec((tt, POOL_CH), lambda b, i: (b * nt + i, 0)),
        out_shape=jax.ShapeDtypeStruct((batch * seq, POOL_CH), BF16),
        compiler_params=_cparams(2),
        name="pool_prompt",
    )(z, z, maps, scale)


def _pool_sample_kernel(p_ref, st_ref, maps_ref, scale_ref, o_ref, ns_ref):
    p = p_ref[...]
    rows = [st_ref[:, r, :] for r in range(POOL_BUF)]
    sums = []
    for g, w in enumerate(POOL_WINDOWS):
        c0 = g * POOL_GROUP_CH
        c1 = c0 + POOL_GROUP_CH
        s = p[:, c0:c1]
        for r in range(POOL_BUF - (w - 1), POOL_BUF):
            s = s + rows[r][:, c0:c1]
        sums.append(s)
    cnts = [float(min(PAST_LEN + 1, w)) for w in POOL_WINDOWS]
    _pool_mix(sums, p, cnts, maps_ref, scale_ref, o_ref)
    for r in range(POOL_BUF - 1):
        ns_ref[:, r, :] = rows[r + 1]
    ns_ref[:, POOL_BUF - 1, :] = p


def pool_sample(zs, state, maps, scale):
    nbatch = zs.shape[0]
    pblk = P_OFF // POOL_CH
    st_spec = pl.BlockSpec((nbatch, POOL_BUF, POOL_CH), lambda i: (0, 0, 0))
    return pl.pallas_call(
        _pool_sample_kernel,
        grid=(1,),
        in_specs=[pl.BlockSpec((nbatch, POOL_CH), lambda i: (0, pblk)),
                  st_spec,
                  pl.BlockSpec((N_POOL_GROUPS, POOL_GROUP_CH, POOL_GROUP_CH), lambda i: (0, 0, 0)),
                  pl.BlockSpec((1, POOL_CH), lambda i: (0, 0))],
        out_specs=[pl.BlockSpec((nbatch, POOL_CH), lambda i: (0, 0)), st_spec],
        out_shape=[jax.ShapeDtypeStruct((nbatch, POOL_CH), BF16),
                   jax.ShapeDtypeStruct((nbatch, POOL_BUF, POOL_CH), F32)],
        compiler_params=_cparams(1),
        name="pool_sample",
    )(zs, state, maps, scale)


def _mem_prompt_kernel(q_ref, k_ref, v_ref, o_ref):
    q = q_ref[...].astype(BF16)
    k = k_ref[...].astype(BF16)
    v = v_ref[...].astype(BF16)
    for h in range(MEM_HEADS):
        c0 = h * MEM_HEAD_DIM
        c1 = c0 + MEM_HEAD_DIM
        logits = _dot(q[:, c0:c1], k[:, c0:c1], _NT) * (MEM_HEAD_DIM ** -0.5)
        p = jnp.exp(logits - jnp.max(logits, axis=-1, keepdims=True))
        probs = p * (1.0 / jnp.sum(p, axis=-1, keepdims=True))
        o_ref[:, c0:c1] = _dot(probs, v[:, c0:c1]).astype(BF16)


def mem_prompt(z, kv, *, batch, seq, tq):
    nt = seq // tq
    mblk = M_OFF // MEM_WIDTH
    return pl.pallas_call(
        _mem_prompt_kernel,
        grid=(batch, nt),
        in_specs=[pl.BlockSpec((tq, MEM_WIDTH), lambda b, i: (b * nt + i, mblk)),
                  pl.BlockSpec((N_MEM, MEM_WIDTH), lambda b, i: (b, 0)),
                  pl.BlockSpec((N_MEM, MEM_WIDTH), lambda b, i: (b, 1))],
        out_specs=pl.BlockSpec((tq, MEM_WIDTH), lambda b, i: (b * nt + i, 0)),
        out_shape=jax.ShapeDtypeStruct((batch * seq, MEM_WIDTH), BF16),
        compiler_params=_cparams(2),
        name="mem_prompt",
    )(z, kv, kv)


MEM_Q_ROWS = 16


def _mem_sample_kernel(q_ref, k_ref, v_ref, o_ref, *, bb):
    hrow = lax.broadcasted_iota(jnp.int32, (MEM_Q_ROWS, MEM_WIDTH), 0)
    hcol = lax.broadcasted_iota(jnp.int32, (MEM_Q_ROWS, MEM_WIDTH), 1)
    own = (hcol // MEM_HEAD_DIM) == hrow
    rows = []
    for b in range(bb):
        k = jnp.concatenate([k_ref[b, :, h, :] for h in range(MEM_HEADS)], axis=1)
        v = jnp.concatenate([v_ref[b, :, h, :] for h in range(MEM_HEADS)], axis=1)
        qbd = jnp.where(own, jnp.broadcast_to(q_ref[b:b + 1, :], (MEM_Q_ROWS, MEM_WIDTH)), 0.0)
        logits = _dot(qbd, k, _NT) * (MEM_HEAD_DIM ** -0.5)
        p = jnp.exp(logits - jnp.max(logits, axis=-1, keepdims=True))
        probs = p * (1.0 / jnp.sum(p, axis=-1, keepdims=True))
        o = jnp.where(own, _dot(probs, v), 0.0)
        rows.append(jnp.sum(o, axis=0, keepdims=True))
    o_ref[...] = jnp.concatenate(rows, axis=0)


def mem_sample(zs, mem_k, mem_v, layer, *, bb):
    nbatch = zs.shape[0]
    mblk = M_OFF // MEM_WIDTH
    kv_spec = pl.BlockSpec((None, bb, N_MEM, MEM_HEADS, MEM_HEAD_DIM), lambda i: (layer, i, 0, 0, 0))
    return pl.pallas_call(
        functools.partial(_mem_sample_kernel, bb=bb),
        grid=(nbatch // bb,),
        in_specs=[pl.BlockSpec((bb, MEM_WIDTH), lambda i: (i, mblk)), kv_spec, kv_spec],
        out_specs=pl.BlockSpec((bb, MEM_WIDTH), lambda i: (i, 0)),
        out_shape=jax.ShapeDtypeStruct((nbatch, MEM_WIDTH), F32),
        compiler_params=_cparams(1),
        name="mem_sample",
    )(zs, mem_k, mem_v)


def _to_row_major(x, o_ref, rows):
    for c in range(ROW_SUB):
        o_ref[pl.ds(c, rows, stride=ROW_SUB), :] = x[:, c * LANES:(c + 1) * LANES]


def _from_row_major(ref, first_row, rows, c, pitch=ROW_SUB):
    return ref[pl.ds(first_row + c, rows, stride=pitch), :]


def _norm_router_kernel(xp_ref, xs_ref, g_ref, r_ref, xn_ref, idx_ref, wgt_ref, *, tm, n_prompt_steps):
    x = jnp.where(pl.program_id(0) < n_prompt_steps, xp_ref[...], xs_ref[...])
    xn = _rms(x, g_ref[...])
    _to_row_major(xn, xn_ref, tm)
    logits = _dot(xn, r_ref[...])
    lane = lax.broadcasted_iota(jnp.int32, logits.shape, 1)
    logits = jnp.where(lane < N_EXPERTS, logits, -jnp.inf)
    v1 = jnp.max(logits, axis=-1, keepdims=True)
    i1 = jnp.min(jnp.where(logits == v1, lane, ROUTER_PAD), axis=-1, keepdims=True)
    rest = jnp.where(lane == i1, -jnp.inf, logits)
    v2 = jnp.max(rest, axis=-1, keepdims=True)
    i2 = jnp.min(jnp.where(rest == v2, lane, ROUTER_PAD), axis=-1, keepdims=True)
    e2 = jnp.exp(v2 - v1)
    den = 1.0 + e2
    idx_ref[...] = jnp.where(lane == 0, i1, jnp.where(lane == 1, i2, 0))
    wgt_ref[...] = jnp.where(lane == 0, 1.0 / den, jnp.where(lane == 1, e2 / den, 0.0))


def norm_router(xp, xs, g, router_pad, *, tm):
    m_prompt, d = xp.shape
    m = m_prompt + xs.shape[0]
    nps = m_prompt // tm
    return pl.pallas_call(
        functools.partial(_norm_router_kernel, tm=tm, n_prompt_steps=nps),
        grid=(m // tm,),
        in_specs=[pl.BlockSpec((tm, d), lambda i: (jnp.minimum(i, nps - 1), 0)),
                  pl.BlockSpec((tm, d), lambda i: (jnp.maximum(i - nps, 0), 0)),
                  pl.BlockSpec((1, d), lambda i: (0, 0)),
                  pl.BlockSpec((d, ROUTER_PAD), lambda i: (0, 0))],
        out_specs=[pl.BlockSpec((tm * ROW_SUB, LANES), lambda i: (i, 0)),
                   pl.BlockSpec((tm, ROUTER_PAD), lambda i: (i, 0)),
                   pl.BlockSpec((tm, ROUTER_PAD), lambda i: (i, 0))],
        out_shape=[jax.ShapeDtypeStruct((m * ROW_SUB, LANES), F32),
                   jax.ShapeDtypeStruct((m, ROUTER_PAD), jnp.int32),
                   jax.ShapeDtypeStruct((m, ROUTER_PAD), F32)],
        compiler_params=_cparams(1),
        name="norm_router",
    )(xp, xs, g, router_pad)


def _token_copy(src_hbm, token, buf, slot, sem):
    return pltpu.make_async_copy(src_hbm.at[pl.ds(token * ROW_SUB, ROW_SUB)],
                                 buf.at[pl.ds(slot * BUF_PITCH, ROW_SUB)], sem)


def _gather_loop(src_hbm, tok_ref, buf, sem, count, *, wait):
    def body(s, c):
        copy = _token_copy(src_hbm, tok_ref[0, 0, s], buf, s, sem)
        if wait:
            copy.wait()
        else:
            copy.start()
        return c

    lax.fori_loop(0, count, body, 0, unroll=GATHER_UNROLL)


def _gather_step(i, n_steps, src_hbm, tok_ref, next_tok_ref, bufs, sems, count, valid, next_valid, consume):
    for slot in range(2):
        @pl.when(i % 2 == slot)
        def _():
            @pl.when((i == 0) & valid)
            def _():
                _gather_loop(src_hbm, tok_ref, bufs.at[slot], sems.at[slot], count, wait=False)

            @pl.when((i + 1 < n_steps) & next_valid)
            def _():
                _gather_loop(src_hbm, next_tok_ref, bufs.at[1 - slot], sems.at[1 - slot], count, wait=False)

            @pl.when(valid)
            def _():
                _gather_loop(src_hbm, tok_ref, bufs.at[slot], sems.at[slot], count, wait=True)
                consume(bufs.at[slot])


def _gather_specs(nchunk, count):
    def cur(i, *_):
        return (i, 0, 0)

    def nxt(i, *_):
        return (jnp.minimum(i + 1, nchunk - 1), 0, 0)

    return [pl.BlockSpec((1, 1, count), cur, memory_space=pltpu.SMEM),
            pl.BlockSpec((1, 1, count), nxt, memory_space=pltpu.SMEM)]


def _gather_scratch(count):
    return [pltpu.VMEM((2, count * BUF_PITCH, LANES), F32), pltpu.SemaphoreType.DMA((2,))]


def _dispatch_kernel(used_ref, tok_ref, next_tok_ref, src_hbm, o_ref, bufs, sems, *, rows, n_steps):
    i = pl.program_id(0)
    valid = i * rows < used_ref[0]

    def consume(buf):
        for c in range(ROW_SUB):
            o_ref[:, c * LANES:(c + 1) * LANES] = _from_row_major(buf, 0, rows, c, BUF_PITCH).astype(BF16)

    _gather_step(i, n_steps, src_hbm, tok_ref, next_tok_ref, bufs, sems, rows,
                 valid, (i + 1) * rows < used_ref[0], consume)

    @pl.when(jnp.logical_not(valid))
    def _():
        o_ref[...] = jnp.zeros_like(o_ref)


def moe_dispatch(xn, src_token, used_rows, *, rows):
    n = src_token.shape[0]
    nchunk = n // rows
    tokens = src_token.reshape(nchunk, 1, rows)
    return pl.pallas_call(
        functools.partial(_dispatch_kernel, rows=rows, n_steps=nchunk),
        grid_spec=pltpu.PrefetchScalarGridSpec(
            num_scalar_prefetch=1,
            grid=(nchunk,),
            in_specs=_gather_specs(nchunk, rows) + [pl.BlockSpec(memory_space=pl.ANY)],
            out_specs=pl.BlockSpec((rows, D_MODEL), lambda i, used: (i, 0)),
            scratch_shapes=_gather_scratch(rows)),
        out_shape=jax.ShapeDtypeStruct((n, D_MODEL), BF16),
        compiler_params=_cparams(1),
        name="moe_dispatch",
    )(used_rows, tokens, tokens, xn)


STEP_IDLE, STEP_COMPUTE, STEP_ZERO = 0, 1, 2


def _moe_up_kernel(xt_ref, ot_ref, oj_ref, we_ref, wj_ref, cast_ref, mode_ref, ne_ref, nj_ref, more_ref,
                   x_ref, w1_hbm, w3_hbm, w2_ref, o_ref, w2b_ref, w1b_ref, w3b_ref, w1f_ref, w3f_ref, sems,
                   *, n_w2_blocks, tn):
    s = pl.program_id(0)

    def weight_copies(e, j):
        cols = pl.ds(pl.multiple_of(j * tn, tn), tn)
        return (pltpu.make_async_copy(w1_hbm.at[e, :, cols], w1f_ref, sems.at[0]),
                pltpu.make_async_copy(w3_hbm.at[e, :, cols], w3f_ref, sems.at[1]))

    @pl.when(cast_ref[s] > 0)
    def _():
        @pl.when(s == 0)
        def _():
            for copy in weight_copies(we_ref[0], wj_ref[0]):
                copy.start()

        for copy in weight_copies(we_ref[s], wj_ref[s]):
            copy.wait()
        w1b_ref[...] = w1f_ref[...].astype(BF16)
        w3b_ref[...] = w3f_ref[...].astype(BF16)

        @pl.when(more_ref[s] > 0)
        def _():
            for copy in weight_copies(ne_ref[s], nj_ref[s]):
                copy.start()

    @pl.when(s < n_w2_blocks)
    def _():
        w2b_ref[...] = w2_ref[...].astype(BF16)

    @pl.when(mode_ref[s] == STEP_COMPUTE)
    def _():
        x = x_ref[...]
        a = _dot(x, w1b_ref[...])
        b = _dot(x, w3b_ref[...])
        o_ref[...] = (jax.nn.silu(a) * b).astype(BF16)

    @pl.when(mode_ref[s] == STEP_ZERO)
    def _():
        o_ref[...] = jnp.zeros_like(o_ref)


def _moe_up_steps(tiles_per_expert, first_tile, *, n_tiles, nj):
    n_steps = (n_tiles + N_EXPERTS) * nj
    used = jnp.sum(tiles_per_expert)
    n_compute = used * nj
    n_zero = (n_tiles - used) * nj
    empty = tiles_per_expert == 0
    n_cast_only = jnp.sum(empty.astype(jnp.int32)) * nj
    n_live = n_compute + n_zero + n_cast_only
    a = jnp.minimum(jnp.arange(n_steps, dtype=jnp.int32), n_live - 1)
    blk_end = jnp.cumsum(tiles_per_expert * nj)
    e_c = jnp.minimum(jnp.sum((a[:, None] >= blk_end[None, :]).astype(jnp.int32), axis=1), N_EXPERTS - 1)
    onehot_c = (e_c[:, None] == jnp.arange(N_EXPERTS, dtype=jnp.int32)[None, :]).astype(jnp.int32)
    pick = lambda v: jnp.sum(onehot_c * v[None, :], axis=1)
    within = a - (pick(blk_end) - pick(tiles_per_expert) * nj)
    t_e = jnp.maximum(pick(tiles_per_expert), 1)
    j_c = within // t_e
    r_c = within % t_e
    tile_c = pick(first_tile) + r_c
    z = a - n_compute
    tile_z = used + z // nj
    j_z = z % nj
    b = a - n_compute - n_zero
    empty_rank = jnp.cumsum(empty.astype(jnp.int32)) - 1
    k = b // nj
    e_b = jnp.sum(jnp.where(empty[None, :] & (empty_rank[None, :] == k[:, None]),
                            jnp.arange(N_EXPERTS, dtype=jnp.int32)[None, :], 0), axis=1)
    j_b = b % nj
    is_c = a < n_compute
    is_z = (~is_c) & (a < n_compute + n_zero)
    is_b = (~is_c) & (~is_z)
    live = jnp.arange(n_steps, dtype=jnp.int32) < n_live
    last_tile = used - 1
    last_e = jnp.sum((last_tile >= jnp.cumsum(tiles_per_expert)).astype(jnp.int32))
    x_tile = jnp.where(is_c, tile_c, last_tile)
    o_tile = jnp.where(is_c, tile_c, jnp.where(is_z, tile_z, n_tiles - 1))
    o_col = jnp.where(is_c, j_c, jnp.where(is_z, j_z, nj - 1))
    w_e = jnp.where(is_c, e_c, jnp.where(is_b, e_b, last_e))
    w_col = jnp.where(is_c, j_c, jnp.where(is_b, j_b, nj - 1))
    cast = (live & ((is_c & (r_c == 0)) | is_b)).astype(jnp.int32)
    mode = jnp.where(live & is_c, STEP_COMPUTE, jnp.where(live & is_z, STEP_ZERO, STEP_IDLE))
    steps = jnp.arange(n_steps, dtype=jnp.int32)
    first_at_or_after = jnp.flip(lax.cummin(jnp.flip(jnp.where(cast > 0, steps, n_steps))))
    next_first = jnp.concatenate([first_at_or_after[1:], jnp.full((1,), n_steps, jnp.int32)])
    more = (next_first < n_steps).astype(jnp.int32)
    next_first = jnp.minimum(next_first, n_steps - 1)
    tables = (x_tile, o_tile, o_col, w_e, w_col, cast, mode, w_e[next_first], w_col[next_first], more)
    return tuple(t.astype(jnp.int32) for t in tables), n_steps


def moe_up(xs, w1, w3, w2, tiles_per_expert, first_tile, *, tm, tn):
    n, d = xs.shape
    f = w1.shape[2]
    nj = f // tn
    tables, n_steps = _moe_up_steps(tiles_per_expert, first_tile, n_tiles=n // tm, nj=nj)

    def x_map(s, xt, *_):
        return (xt[s], 0)

    def o_map(s, xt, ot, oj, *_):
        return (ot[s], oj[s])

    w2_slabs = w2.reshape(-1, W2_SLAB_ROWS, d)
    n_slabs = w2_slabs.shape[0]
    assert n_slabs <= n_steps

    def w2_map(s, *_):
        return (jnp.minimum(s, n_slabs - 1), 0, 0)

    h, w2b = pl.pallas_call(
        functools.partial(_moe_up_kernel, n_w2_blocks=n_slabs, tn=tn),
        grid_spec=pltpu.PrefetchScalarGridSpec(
            num_scalar_prefetch=len(tables),
            grid=(n_steps,),
            in_specs=[pl.BlockSpec((tm, d), x_map),
                      pl.BlockSpec(memory_space=pl.ANY),
                      pl.BlockSpec(memory_space=pl.ANY),
                      pl.BlockSpec((1, W2_SLAB_ROWS, d), w2_map)],
            out_specs=[pl.BlockSpec((tm, tn), o_map),
                       pl.BlockSpec((1, W2_SLAB_ROWS, d), w2_map)],
            scratch_shapes=[pltpu.VMEM((d, tn), BF16), pltpu.VMEM((d, tn), BF16),
                            pltpu.VMEM((d, tn), F32), pltpu.VMEM((d, tn), F32),
                            pltpu.SemaphoreType.DMA((2,))]),
        out_shape=[jax.ShapeDtypeStruct((n, f), BF16),
                   jax.ShapeDtypeStruct(w2_slabs.shape, BF16)],
        compiler_params=_cparams(1),
        name="moe_up",
    )(*tables, xs, w1, w3, w2_slabs)
    return h, w2b.reshape(w2.shape)


def _weight_map(nj):
    def w_map(t, j, te, tv):
        return (te[t], 0, jnp.where(tv[t] > 0, j, nj - 1))

    return w_map


def _moe_down_kernel(te_ref, tv_ref, h_ref, w2_ref, o_ref, *, tm, tn):
    j = pl.program_id(1)
    groups = tn // LANES

    def put(val):
        for c in range(groups):
            o_ref[pl.ds(j * groups + c, tm, stride=ROW_SUB), :] = val[:, c * LANES:(c + 1) * LANES]

    @pl.when(tv_ref[pl.program_id(0)] > 0)
    def _():
        put(_dot(h_ref[...], w2_ref[0]))

    @pl.when(tv_ref[pl.program_id(0)] == 0)
    def _():
        put(jnp.zeros((tm, tn), F32))


def moe_down(h, w2, tiles, *, tm, tn):
    n, f = h.shape
    d = w2.shape[2]
    nj = d // tn
    return pl.pallas_call(
        functools.partial(_moe_down_kernel, tm=tm, tn=tn),
        grid_spec=pltpu.PrefetchScalarGridSpec(
            num_scalar_prefetch=2,
            grid=(n // tm, nj),
            in_specs=[pl.BlockSpec((tm, f), lambda t, j, te, tv: (t, 0)),
                      pl.BlockSpec((1, f, tn), _weight_map(nj))],
            out_specs=pl.BlockSpec((tm * ROW_SUB, LANES), lambda t, j, te, tv: (t, 0))),
        out_shape=jax.ShapeDtypeStruct((n * ROW_SUB, LANES), F32),
        compiler_params=_cparams(2),
        name="moe_down",
    )(*tiles, h, w2)


def _combine_kernel(pos_ref, next_pos_ref, xp_ref, xs_ref, w_ref, g_ref, ys_hbm, yp_ref, ysmp_ref, bufs, sems,
                    *, rows, n_prompt_chunks, n_steps):
    i = pl.program_id(0)
    w0 = w_ref[:, 0:1]
    w1 = w_ref[:, 1:2]

    def finish(buf, x_ref, o_ref):
        for c in range(ROW_SUB):
            a = _from_row_major(buf, 0, rows, c, BUF_PITCH)
            b = _from_row_major(buf, rows * BUF_PITCH, rows, c, BUF_PITCH)
            o_ref[:, c * LANES:(c + 1) * LANES] = x_ref[:, c * LANES:(c + 1) * LANES] + (w0 * a + w1 * b)
        o_ref[...] = _rms(o_ref[...], g_ref[...])

    def consume(buf):
        @pl.when(i < n_prompt_chunks)
        def _():
            finish(buf, xp_ref, yp_ref)

        @pl.when(i >= n_prompt_chunks)
        def _():
            finish(buf, xs_ref, ysmp_ref)

    every_step = i >= 0
    _gather_step(i, n_steps, ys_hbm, pos_ref, next_pos_ref, bufs, sems, TOP_K * rows,
                 every_step, every_step, consume)


def moe_combine(xp, xs, wgt, g_final, ys, dest, *, rows):
    m_prompt, d = xp.shape
    m = m_prompt + xs.shape[0]
    nchunk = m // rows
    npc = m_prompt // rows
    pos = dest.reshape(nchunk, rows, TOP_K).transpose(0, 2, 1).reshape(nchunk, 1, TOP_K * rows)

    def p_map(i):
        return (jnp.minimum(i, npc - 1), 0)

    def s_map(i):
        return (jnp.maximum(i - npc, 0), 0)

    return pl.pallas_call(
        functools.partial(_combine_kernel, rows=rows, n_prompt_chunks=npc, n_steps=nchunk),
        grid=(nchunk,),
        in_specs=_gather_specs(nchunk, TOP_K * rows) + [
                  pl.BlockSpec((rows, d), p_map),
                  pl.BlockSpec((rows, d), s_map),
                  pl.BlockSpec((rows, ROUTER_PAD), lambda i: (i, 0)),
                  pl.BlockSpec((1, d), lambda i: (0, 0)),
                  pl.BlockSpec(memory_space=pl.ANY)],
        out_specs=[pl.BlockSpec((rows, d), p_map), pl.BlockSpec((rows, d), s_map)],
        out_shape=[jax.ShapeDtypeStruct((m_prompt, d), F32),
                   jax.ShapeDtypeStruct((m - m_prompt, d), F32)],
        scratch_shapes=_gather_scratch(TOP_K * rows),
        compiler_params=_cparams(1),
        name="moe_combine",
    )(pos, pos, xp, xs, wgt, g_final, ys)


def _routing_tables(idx2, *, tm, n_rows):
    e_flat = idx2.reshape(-1)
    onehot = (e_flat[:, None] == jnp.arange(N_EXPERTS, dtype=jnp.int32)[None, :]).astype(jnp.int32)
    csum = jnp.cumsum(onehot, axis=0)
    counts = csum[-1]
    rank = jnp.sum(csum * onehot, axis=1) - 1
    padded = ((counts + tm - 1) // tm) * tm
    ends = jnp.cumsum(padded)
    starts = ends - padded
    dest = (jnp.sum(onehot * starts[None, :], axis=1) + rank).astype(jnp.int32)
    token = jnp.arange(e_flat.shape[0], dtype=jnp.int32) // TOP_K
    src_token = jnp.zeros((n_rows,), jnp.int32).at[dest].set(token, unique_indices=True)
    tile_ids = jnp.arange(n_rows // tm, dtype=jnp.int32)
    used_tiles = ends[-1] // tm
    tile_valid = (tile_ids < used_tiles).astype(jnp.int32)
    tile_start = jnp.minimum(tile_ids, used_tiles - 1) * tm
    tile_expert = jnp.sum((tile_start[:, None] >= ends[None, :]).astype(jnp.int32), axis=1)
    per_expert = ((padded // tm).astype(jnp.int32), (starts // tm).astype(jnp.int32))
    return dest.reshape(-1, TOP_K), src_token, ends[-1:].astype(jnp.int32), (tile_expert, tile_valid), per_expert


def _rel_bias_by_distance(rel_bias_table):
    d = jnp.arange(WINDOW, dtype=jnp.int32)
    max_exact = REL_BUCKETS // 2
    d_f = jnp.maximum(d, 1).astype(F32)
    large = max_exact + (jnp.log(d_f / max_exact) / math.log(REL_MAX_DIST / max_exact)
                         * (REL_BUCKETS - max_exact)).astype(jnp.int32)
    large = jnp.minimum(large, REL_BUCKETS - 1)
    bucket = jnp.where(d < max_exact, d, large)
    return rel_bias_table[bucket].T


def _mixer_prompt(x, mem2d, wts, lw, layer, bias2, *, batch, seq):
    kv = norm_matmul(mem2d, lw["g_mem"], wts["w_mem_kv"], layer, tm=1024, tn=512)
    z = norm_matmul(x, lw["g_attn"], wts["w_in"], layer, tm=1024, tn=512, n_cols=G_OFF)
    ya = swa_prompt(z, bias2, lw["sinks"], batch=batch, seq=seq)
    yp = pool_prompt(z, lw["pool_maps"], lw["pool_scale"], batch=batch, seq=seq, tt=512)
    ym = mem_prompt(z, kv, batch=batch, seq=seq, tq=512)
    merged = merge_branches(x, lw["g_attn"], ya, yp, ym, wts["w_up_attn"], wts["w_up_pool"], wts["w_up_mem"],
                            wts["w_in"], layer, tm=1024, tn=512)
    x = matmul_residual(merged, wts["w_o"], layer, x, tm=1024, tn=512)
    z3 = z.reshape(batch, seq, G_OFF)
    new_k = z3[:, seq - WINDOW:, K_OFF:V_OFF].reshape(batch, WINDOW, N_KV_HEADS, HEAD_DIM)
    new_v = z3[:, seq - WINDOW:, V_OFF:P_OFF].reshape(batch, WINDOW, N_KV_HEADS, HEAD_DIM)
    new_pool = z3[:, seq - POOL_BUF:, P_OFF:M_OFF]
    kv3 = kv.reshape(batch, N_MEM, 2 * MEM_WIDTH)
    mk = kv3[..., :MEM_WIDTH].reshape(batch, N_MEM, MEM_HEADS, MEM_HEAD_DIM)
    mv = kv3[..., MEM_WIDTH:].reshape(batch, N_MEM, MEM_HEADS, MEM_HEAD_DIM)
    return x, new_k, new_v, new_pool, mk, mv


def _mixer_sample(x, cache_k, cache_v, state, mem_k, mem_v, wts, lw, layer, bias_s):
    nb = x.shape[0]
    z = norm_matmul(x, lw["g_attn"], wts["w_in"], layer, tm=nb, tn=512, n_cols=G_OFF)
    q3 = z[:, Q_OFF:K_OFF].reshape(nb, N_HEADS, HEAD_DIM)
    ya3, new_k, new_v = swa_sample(q3, z, cache_k, cache_v, layer, bias_s, lw["sinks"][:, None], bb=8)
    yp, new_pool = pool_sample(z, state, lw["pool_maps"], lw["pool_scale"])
    ym = mem_sample(z, mem_k, mem_v, layer, bb=8)
    ya = ya3.reshape(nb, ATTN_WIDTH)
    merged = merge_branches(x, lw["g_attn"], ya, yp, ym, wts["w_up_attn"], wts["w_up_pool"], wts["w_up_mem"],
                            wts["w_in"], layer, tm=nb, tn=512)
    x = matmul_residual(merged, wts["w_o"], layer, x, tm=nb, tn=512)
    return (x, new_k.reshape(nb, WINDOW, N_KV_HEADS, HEAD_DIM),
            new_v.reshape(nb, WINDOW, N_KV_HEADS, HEAD_DIM), new_pool)


def _dense_ffn(x, g, w1, w3, w2, layer, *, tm):
    h = norm_swiglu(x, g, w1, w3, layer, tm=tm, tn=512)
    return matmul_residual(h, w2, layer, x, tm=tm, tn=512)


def _moe_ffn_final(xp, xs, g, router, w1, w3, w2, g_final):
    m_prompt = xp.shape[0]
    m = m_prompt + xs.shape[0]
    router_pad = jnp.pad(router, ((0, 0), (0, ROUTER_PAD - N_EXPERTS)))
    xn, idx_pad, wgt = norm_router(xp, xs, g, router_pad, tm=COMBINE_ROWS)
    n_rows = pl.cdiv(TOP_K * m, MOE_TM) * MOE_TM + N_EXPERTS * MOE_TM
    dest, src_token, used_rows, tiles, per_expert = _routing_tables(idx_pad[:, :TOP_K], tm=MOE_TM, n_rows=n_rows)
    xsort = moe_dispatch(xn, src_token, used_rows, rows=DISPATCH_ROWS)
    h, w2b = moe_up(xsort, w1, w3, w2, *per_expert, tm=MOE_TM, tn=512)
    ys = moe_down(h, w2b, tiles, tm=MOE_TM, tn=512)
    return moe_combine(xp, xs, wgt, g_final, ys, dest, rows=COMBINE_ROWS)


def kernel(x_prompt, x_sample, mem_prompt, cache_swa_k, cache_swa_v, state_pool, cache_mem_k, cache_mem_v, rel_bias_table, g_attn, w_in, attn_sinks, pool_maps, pool_scale, g_mem, w_mem_kv, w_up_attn, w_up_pool, w_up_mem, w_o, g_ffn, ffn_w1, ffn_w3, ffn_w2, moe_router, moe_w1, moe_w3, moe_w2, g_final):
    batch, seq, d = x_prompt.shape
    nb_s = x_sample.shape[0]
    depth = w_in.shape[0]
    assert depth == 2 and x_sample.shape[1] == 1

    bias_hw = _rel_bias_by_distance(rel_bias_table).astype(F32)
    col = jnp.arange(2 * WINDOW)
    bias2 = bias_hw[:, jnp.clip(WINDOW - col, 0, WINDOW - 1)]
    bias_s = bias_hw[:, ::-1]

    wts = dict(w_in=w_in.astype(BF16), w_mem_kv=w_mem_kv.astype(BF16), w_up_attn=w_up_attn.astype(BF16),
               w_up_pool=w_up_pool.astype(BF16), w_up_mem=w_up_mem.astype(BF16), w_o=w_o.astype(BF16))
    ffn = (ffn_w1.astype(BF16), ffn_w3.astype(BF16), ffn_w2.astype(BF16))

    xp = x_prompt.reshape(batch * seq, d)
    xs = x_sample.reshape(nb_s, d)
    mem2d = mem_prompt.reshape(batch * N_MEM, d)
    swa_k = cache_swa_k.reshape(depth, nb_s, WINDOW, KV_WIDTH)
    swa_v = cache_swa_v.reshape(depth, nb_s, WINDOW, KV_WIDTH)
    p_k, p_v, p_pool, p_mk, p_mv, s_k, s_v, s_pool = [], [], [], [], [], [], [], []
    for l in range(depth):
        lw = dict(g_attn=g_attn[l][None, :], sinks=attn_sinks[l], pool_maps=pool_maps[l].astype(BF16),
                  pool_scale=pool_scale[l][None, :], g_mem=g_mem[l][None, :])
        xp, nk, nv, npool, mk, mv = _mixer_prompt(xp, mem2d, wts, lw, l, bias2, batch=batch, seq=seq)
        p_k.append(nk); p_v.append(nv); p_pool.append(npool); p_mk.append(mk); p_mv.append(mv)
        xs, nk, nv, npool = _mixer_sample(xs, swa_k, swa_v, state_pool[l],
                                          cache_mem_k, cache_mem_v, wts, lw, l, bias_s)
        s_k.append(nk); s_v.append(nv); s_pool.append(npool)
        gl = g_ffn[l][None, :]
        i = l // 2
        if l % 2 == 0:
            xp = _dense_ffn(xp, gl, *ffn, i, tm=1024)
            xs = _dense_ffn(xs, gl, *ffn, i, tm=nb_s)
        else:
            yp, ys = _moe_ffn_final(xp, xs, gl, moe_router[i], moe_w1[i], moe_w3[i], moe_w2[i], g_final[None, :])
    y_prompt = yp.reshape(batch, seq, d)
    y_sample = ys.reshape(nb_s, 1, d)
    return (y_prompt, y_sample,
            jnp.stack(p_k), jnp.stack(p_v), jnp.stack(p_pool), jnp.stack(p_mk), jnp.stack(p_mv),
            jnp.stack(s_k), jnp.stack(s_v), jnp.stack(s_pool))
```

```python
import functools
import math

import jax
import jax.numpy as jnp
from jax import lax
from jax.experimental import pallas as pl
from jax.experimental.pallas import tpu as pltpu

F32 = jnp.float32
BF16 = jnp.bfloat16

D_MODEL = 2048
N_HEADS = 16
N_KV_HEADS = 4
HEAD_DIM = 64
Q_GROUP = N_HEADS // N_KV_HEADS
WINDOW = 128
ATTN_WIDTH = N_HEADS * HEAD_DIM
KV_WIDTH = N_KV_HEADS * HEAD_DIM
REL_BUCKETS = 32
REL_MAX_DIST = 128
POOL_WINDOWS = (2, 4, 8, 16)
N_POOL_GROUPS = 4
POOL_CH = D_MODEL // 4
POOL_GROUP_CH = POOL_CH // N_POOL_GROUPS
POOL_BUF = max(POOL_WINDOWS) - 1
N_MEM = 256
MEM_HEADS = 4
MEM_HEAD_DIM = 128
MEM_WIDTH = MEM_HEADS * MEM_HEAD_DIM
N_BRANCHES = 3
Q_OFF = 0
K_OFF = Q_OFF + ATTN_WIDTH
V_OFF = K_OFF + KV_WIDTH
P_OFF = V_OFF + KV_WIDTH
M_OFF = P_OFF + POOL_CH
G_OFF = M_OFF + MEM_WIDTH
N_EXPERTS = 8
TOP_K = 2
RMS_EPS = 1e-5
NEG_INF = -1e30
PAST_LEN = 8192

VMEM_LIMIT_BYTES = 56 * 1024 * 1024
LANES = 128
POOL_HALO = 16
ROUTER_PAD = LANES
ROW_SUB = D_MODEL // LANES
BUF_PITCH = ROW_SUB + 8
GATHER_UNROLL = 16
EPILOGUE_ROWS = 512
W2_SLAB_ROWS = 64
ROW_TILE = 1024
COL_TILE = 512
SEQ_TILE = 512
SAMPLE_SEQS = 8
MOE_TM = 512
DISPATCH_ROWS = 512
COMBINE_ROWS = 128


def _cparams(n_axes):
    return pltpu.CompilerParams(dimension_semantics=("arbitrary",) * n_axes,
                                vmem_limit_bytes=VMEM_LIMIT_BYTES)


def _rms(x, g):
    return (x * lax.rsqrt(jnp.mean(x * x, axis=-1, keepdims=True) + RMS_EPS)) * g


_NN = (((1,), (0,)), ((), ()))
_NT = (((1,), (1,)), ((), ()))


def _dot(a, b, dims=_NN):
    return lax.dot_general(a.astype(BF16), b.astype(BF16), dims, preferred_element_type=F32)


def _row_chunks(rows):
    n = max(1, rows // EPILOGUE_ROWS)
    return [slice(k * (rows // n), (k + 1) * (rows // n)) for k in range(n)]


def _layer_cols(layer, rows, tn):
    return pl.BlockSpec((None, rows, tn), lambda i, j: (layer, 0, j))


def _norm_mm_kernel(x_ref, g_ref, w_ref, o_ref, xn_ref):
    @pl.when(pl.program_id(1) == 0)
    def _():
        xn_ref[...] = _rms(x_ref[...], g_ref[...]).astype(BF16)

    for rs in _row_chunks(o_ref.shape[0]):
        o_ref[rs, :] = _dot(xn_ref[rs, :], w_ref[...])


def norm_matmul(x, g, w, layer, *, tm, tn, n_cols=None):
    m, d = x.shape
    n = w.shape[2] if n_cols is None else n_cols
    return pl.pallas_call(
        _norm_mm_kernel,
        grid=(m // tm, n // tn),
        in_specs=[pl.BlockSpec((tm, d), lambda i, j: (i, 0)),
                  pl.BlockSpec((1, d), lambda i, j: (0, 0)),
                  _layer_cols(layer, d, tn)],
        out_specs=pl.BlockSpec((tm, tn), lambda i, j: (i, j)),
        out_shape=jax.ShapeDtypeStruct((m, n), F32),
        scratch_shapes=[pltpu.VMEM((tm, d), BF16)],
        compiler_params=_cparams(2),
        name="norm_matmul",
    )(x, g, w)


def _norm_swiglu_kernel(x_ref, g_ref, w1_ref, w3_ref, o_ref, xn_ref):
    @pl.when(pl.program_id(1) == 0)
    def _():
        xn_ref[...] = _rms(x_ref[...], g_ref[...]).astype(BF16)

    for rs in _row_chunks(o_ref.shape[0]):
        xn = xn_ref[rs, :]
        a = _dot(xn, w1_ref[...])
        b = _dot(xn, w3_ref[...])
        o_ref[rs, :] = (jax.nn.silu(a) * b).astype(BF16)


def norm_swiglu(x, g, w1, w3, layer, *, tm, tn):
    m, d = x.shape
    n = w1.shape[2]
    return pl.pallas_call(
        _norm_swiglu_kernel,
        grid=(m // tm, n // tn),
        in_specs=[pl.BlockSpec((tm, d), lambda i, j: (i, 0)),
                  pl.BlockSpec((1, d), lambda i, j: (0, 0)),
                  _layer_cols(layer, d, tn), _layer_cols(layer, d, tn)],
        out_specs=pl.BlockSpec((tm, tn), lambda i, j: (i, j)),
        out_shape=jax.ShapeDtypeStruct((m, n), BF16),
        scratch_shapes=[pltpu.VMEM((tm, d), BF16)],
        compiler_params=_cparams(2),
        name="norm_swiglu",
    )(x, g, w1, w3)


def _mm_res_kernel(a_ref, w_ref, r_ref, o_ref):
    for rs in _row_chunks(o_ref.shape[0]):
        o_ref[rs, :] = r_ref[rs, :] + _dot(a_ref[rs, :], w_ref[...])


def matmul_residual(a, w, layer, res, *, tm, tn):
    m, k = a.shape
    n = w.shape[2]
    return pl.pallas_call(
        _mm_res_kernel,
        grid=(m // tm, n // tn),
        in_specs=[pl.BlockSpec((tm, k), lambda i, j: (i, 0)),
                  _layer_cols(layer, k, tn),
                  pl.BlockSpec((tm, tn), lambda i, j: (i, j))],
        out_specs=pl.BlockSpec((tm, tn), lambda i, j: (i, j)),
        out_shape=jax.ShapeDtypeStruct((m, n), F32),
        compiler_params=_cparams(2),
        name="matmul_residual",
    )(a, w, res)


def _merge_kernel(x_ref, g_ref, ya_ref, yp_ref, ym_ref, wa_ref, wp_ref, wm_ref, wg0_ref, wg1_ref, wg2_ref,
                  o_ref, xn_ref):
    @pl.when(pl.program_id(1) == 0)
    def _():
        xn_ref[...] = _rms(x_ref[...], g_ref[...]).astype(BF16)

    for rs in _row_chunks(o_ref.shape[0]):
        xn = xn_ref[rs, :]
        acc = jax.nn.sigmoid(_dot(xn, wg0_ref[...])) * _dot(ya_ref[rs, :], wa_ref[...])
        acc = acc + jax.nn.sigmoid(_dot(xn, wg1_ref[...])) * _dot(yp_ref[rs, :], wp_ref[...])
        acc = acc + jax.nn.sigmoid(_dot(xn, wg2_ref[...])) * _dot(ym_ref[rs, :], wm_ref[...])
        o_ref[rs, :] = acc.astype(BF16)


def merge_branches(x, g, ya, yp, ym, wa, wp, wm, w_in, layer, *, tm, tn):
    m, d = x.shape
    nj = D_MODEL // tn
    g_blk = G_OFF // tn

    def gate_cols(b):
        return pl.BlockSpec((None, d, tn), lambda i, j: (layer, 0, g_blk + b * nj + j))

    return pl.pallas_call(
        _merge_kernel,
        grid=(m // tm, nj),
        in_specs=[pl.BlockSpec((tm, d), lambda i, j: (i, 0)),
                  pl.BlockSpec((1, d), lambda i, j: (0, 0)),
                  pl.BlockSpec((tm, ATTN_WIDTH), lambda i, j: (i, 0)),
                  pl.BlockSpec((tm, POOL_CH), lambda i, j: (i, 0)),
                  pl.BlockSpec((tm, MEM_WIDTH), lambda i, j: (i, 0)),
                  _layer_cols(layer, ATTN_WIDTH, tn), _layer_cols(layer, POOL_CH, tn),
                  _layer_cols(layer, MEM_WIDTH, tn),
                  gate_cols(0), gate_cols(1), gate_cols(2)],
        out_specs=pl.BlockSpec((tm, tn), lambda i, j: (i, j)),
        out_shape=jax.ShapeDtypeStruct((m, D_MODEL), BF16),
        scratch_shapes=[pltpu.VMEM((tm, d), BF16)],
        compiler_params=_cparams(2),
        name="merge_branches",
    )(x, g, ya, yp, ym, wa, wp, wm, w_in, w_in, w_in)


def _swa_prompt_kernel(q_ref, kp_ref, kc_ref, vp_ref, vc_ref, base_ref, sink_ref, o_ref, bias_ref):
    i = pl.program_id(1)

    @pl.when((pl.program_id(0) == 0) & (i == 0))
    def _():
        for h in range(N_HEADS):
            row0 = jnp.broadcast_to(base_ref[h:h + 1, :], (WINDOW, 2 * WINDOW))
            bias_ref[h] = pltpu.roll(row0, 0, 1, stride=1, stride_axis=0)

    q = q_ref[...].astype(BF16)
    k2 = jnp.concatenate([kp_ref[...], kc_ref[...]], axis=0).astype(BF16)
    v2 = jnp.concatenate([vp_ref[...], vc_ref[...]], axis=0).astype(BF16)
    row = lax.broadcasted_iota(jnp.int32, (WINDOW, 2 * WINDOW), 0)
    col = lax.broadcasted_iota(jnp.int32, (WINDOW, 2 * WINDOW), 1)
    dist = row + WINDOW - col
    valid = (dist >= 0) & (dist < WINDOW) & ((i > 0) | (col >= WINDOW))
    for h in range(N_HEADS):
        kv = h // Q_GROUP
        qh = q[:, h * HEAD_DIM:(h + 1) * HEAD_DIM]
        kh = k2[:, kv * HEAD_DIM:(kv + 1) * HEAD_DIM]
        vh = v2[:, kv * HEAD_DIM:(kv + 1) * HEAD_DIM]
        logits = _dot(qh, kh, _NT) * (HEAD_DIM ** -0.5)
        logits = jnp.where(valid, logits + bias_ref[h], NEG_INF)
        s = sink_ref[h]
        m = jnp.maximum(jnp.max(logits, axis=-1, keepdims=True), s)
        p = jnp.exp(logits - m)
        denom = jnp.sum(p, axis=-1, keepdims=True) + jnp.exp(s - m)
        o = _dot(p * (1.0 / denom), vh)
        o_ref[:, h * HEAD_DIM:(h + 1) * HEAD_DIM] = o.astype(BF16)


def swa_prompt(z, bias_base, sinks, *, batch, seq):
    nb = seq // WINDOW
    kblk = K_OFF // KV_WIDTH
    vblk = V_OFF // KV_WIDTH

    def cur(c):
        return pl.BlockSpec((WINDOW, KV_WIDTH), lambda b, i: (b * nb + i, c))

    def prev(c):
        return pl.BlockSpec((WINDOW, KV_WIDTH), lambda b, i: (b * nb + jnp.maximum(i - 1, 0), c))

    return pl.pallas_call(
        _swa_prompt_kernel,
        grid=(batch, nb),
        in_specs=[pl.BlockSpec((WINDOW, ATTN_WIDTH), lambda b, i: (b * nb + i, 0)),
                  prev(kblk), cur(kblk), prev(vblk), cur(vblk),
                  pl.BlockSpec((N_HEADS, 2 * WINDOW), lambda b, i: (0, 0)),
                  pl.BlockSpec(memory_space=pltpu.SMEM)],
        out_specs=pl.BlockSpec((WINDOW, ATTN_WIDTH), lambda b, i: (b * nb + i, 0)),
        out_shape=jax.ShapeDtypeStruct((batch * seq, ATTN_WIDTH), BF16),
        scratch_shapes=[pltpu.VMEM((N_HEADS, WINDOW, 2 * WINDOW), F32)],
        compiler_params=_cparams(2),
        name="swa_prompt",
    )(z, z, z, z, z, bias_base, sinks)


def _swa_sample_kernel(q_ref, kn_ref, vn_ref, ck_ref, cv_ref, bias_ref, sink_ref,
                       o_ref, ok_ref, ov_ref, *, bb):
    row = lax.broadcasted_iota(jnp.int32, (WINDOW, KV_WIDTH), 0)
    hrow = lax.broadcasted_iota(jnp.int32, (N_HEADS, KV_WIDTH), 0)
    hcol = lax.broadcasted_iota(jnp.int32, (N_HEADS, KV_WIDTH), 1)
    own = (hcol // HEAD_DIM) == (hrow // Q_GROUP)
    s = sink_ref[...]
    for b in range(bb):
        newk = jnp.where(row == WINDOW - 1, kn_ref[b:b + 1, :], pltpu.roll(ck_ref[b], WINDOW - 1, 0))
        newv = jnp.where(row == WINDOW - 1, vn_ref[b:b + 1, :], pltpu.roll(cv_ref[b], WINDOW - 1, 0))
        ok_ref[b] = newk
        ov_ref[b] = newv
        qb = q_ref[b]
        qbd = jnp.where(own, jnp.concatenate([qb] * N_KV_HEADS, axis=1), 0.0)
        logits = _dot(qbd, newk, _NT) * (HEAD_DIM ** -0.5)
        logits = logits + bias_ref[...]
        m = jnp.maximum(jnp.max(logits, axis=-1, keepdims=True), s)
        p = jnp.exp(logits - m)
        denom = jnp.sum(p, axis=-1, keepdims=True) + jnp.exp(s - m)
        o = jnp.where(own, _dot(p * (1.0 / denom), newv), 0.0)
        acc = o[:, 0:HEAD_DIM]
        for c in range(1, N_KV_HEADS):
            acc = acc + o[:, c * HEAD_DIM:(c + 1) * HEAD_DIM]
        o_ref[b] = acc.astype(BF16)


def swa_sample(q3, zs, cache_k, cache_v, layer, bias_s, sinks_col, *, bb):
    nbatch = q3.shape[0]
    kblk = K_OFF // KV_WIDTH
    vblk = V_OFF // KV_WIDTH
    cache_spec = pl.BlockSpec((bb, WINDOW, KV_WIDTH), lambda i: (i, 0, 0))
    cache_in = pl.BlockSpec((None, bb, WINDOW, KV_WIDTH), lambda i: (layer, i, 0, 0))
    return pl.pallas_call(
        functools.partial(_swa_sample_kernel, bb=bb),
        grid=(nbatch // bb,),
        in_specs=[pl.BlockSpec((bb, N_HEADS, HEAD_DIM), lambda i: (i, 0, 0)),
                  pl.BlockSpec((bb, KV_WIDTH), lambda i: (i, kblk)),
                  pl.BlockSpec((bb, KV_WIDTH), lambda i: (i, vblk)),
                  cache_in, cache_in,
                  pl.BlockSpec((N_HEADS, WINDOW), lambda i: (0, 0)),
                  pl.BlockSpec((N_HEADS, 1), lambda i: (0, 0))],
        out_specs=[pl.BlockSpec((bb, N_HEADS, HEAD_DIM), lambda i: (i, 0, 0)), cache_spec, cache_spec],
        out_shape=[jax.ShapeDtypeStruct((nbatch, N_HEADS, HEAD_DIM), BF16),
                   jax.ShapeDtypeStruct((nbatch, WINDOW, KV_WIDTH), F32),
                   jax.ShapeDtypeStruct((nbatch, WINDOW, KV_WIDTH), F32)],
        compiler_params=_cparams(1),
        name="swa_sample",
    )(q3, zs, zs, cache_k, cache_v, bias_s, sinks_col)


def _pool_mix(sums, p, cnts, maps_ref, scale_ref, o_ref):
    for g in range(N_POOL_GROUPS):
        c0 = g * POOL_GROUP_CH
        c1 = c0 + POOL_GROUP_CH
        diff = sums[g] / cnts[g] - p[:, c0:c1]
        mixed = _dot(diff, maps_ref[g])
        o_ref[:, c0:c1] = (mixed * scale_ref[:, c0:c1]).astype(BF16)


def _pool_prompt_kernel(p_ref, halo_ref, maps_ref, scale_ref, o_ref, *, tt):
    i = pl.program_id(1)
    p = p_ref[...]
    halo = jnp.where(i == 0, 0.0, halo_ref[...])
    ext = jnp.concatenate([halo, p], axis=0)
    s = ext
    sums = []
    shift = 1
    for g in range(N_POOL_GROUPS):
        s = s[:, (POOL_GROUP_CH if g > 0 else 0):]
        s = s + pltpu.roll(s, shift, 0)
        shift *= 2
        sums.append(s[POOL_HALO:, 0:POOL_GROUP_CH])
    pos = i * tt + lax.broadcasted_iota(jnp.int32, (tt, 1), 0)
    cnts = [jnp.minimum(pos + 1, w).astype(F32) for w in POOL_WINDOWS]
    _pool_mix(sums, p, cnts, maps_ref, scale_ref, o_ref)


def pool_prompt(z, maps, scale, *, batch, seq, tt):
    nt = seq // tt
    pblk = P_OFF // POOL_CH
    hb = tt // POOL_HALO
    return pl.pallas_call(
        functools.partial(_pool_prompt_kernel, tt=tt),
        grid=(batch, nt),
        in_specs=[pl.BlockSpec((tt, POOL_CH), lambda b, i: (b * nt + i, pblk)),
                  pl.BlockSpec((POOL_HALO, POOL_CH),
                               lambda b, i: (jnp.maximum((b * nt + i) * hb - 1, 0), pblk)),
                  pl.BlockSpec((N_POOL_GROUPS, POOL_GROUP_CH, POOL_GROUP_CH), lambda b, i: (0, 0, 0)),
                  pl.BlockSpec((1, POOL_CH), lambda b, i: (0, 0))],
        out_specs=pl.BlockSpec((tt, POOL_CH), lambda b, i: (b * nt + i, 0)),
        out_shape=jax.ShapeDtypeStruct((batch * seq, POOL_CH), BF16),
        compiler_params=_cparams(2),
        name="pool_prompt",
    )(z, z, maps, scale)


def _pool_sample_kernel(p_ref, st_ref, maps_ref, scale_ref, o_ref, ns_ref):
    p = p_ref[...]
    rows = [st_ref[:, r, :] for r in range(POOL_BUF)]
    sums = []
    for g, w in enumerate(POOL_WINDOWS):
        c0 = g * POOL_GROUP_CH
        c1 = c0 + POOL_GROUP_CH
        s = p[:, c0:c1]
        for r in range(POOL_BUF - (w - 1), POOL_BUF):
            s = s + rows[r][:, c0:c1]
        sums.append(s)
    cnts = [float(min(PAST_LEN + 1, w)) for w in POOL_WINDOWS]
    _pool_mix(sums, p, cnts, maps_ref, scale_ref, o_ref)
    for r in range(POOL_BUF - 1):
        ns_ref[:, r, :] = rows[r + 1]
    ns_ref[:, POOL_BUF - 1, :] = p


def pool_sample(zs, state, maps, scale):
    nbatch = zs.shape[0]
    pblk = P_OFF // POOL_CH
    st_spec = pl.BlockSpec((nbatch, POOL_BUF, POOL_CH), lambda i: (0, 0, 0))
    return pl.pallas_call(
        _pool_sample_kernel,
        grid=(1,),
        in_specs=[pl.BlockSpec((nbatch, POOL_CH), lambda i: (0, pblk)),
                  st_spec,
                  pl.BlockSpec((N_POOL_GROUPS, POOL_GROUP_CH, POOL_GROUP_CH), lambda i: (0, 0, 0)),
                  pl.BlockSpec((1, POOL_CH), lambda i: (0, 0))],
        out_specs=[pl.BlockSpec((nbatch, POOL_CH), lambda i: (0, 0)), st_spec],
        out_shape=[jax.ShapeDtypeStruct((nbatch, POOL_CH), BF16),
                   jax.ShapeDtypeStruct((nbatch, POOL_BUF, POOL_CH), F32)],
        compiler_params=_cparams(1),
        name="pool_sample",
    )(zs, state, maps, scale)


def _mem_prompt_kernel(q_ref, k_ref, v_ref, o_ref):
    q = q_ref[...].astype(BF16)
    k = k_ref[...].astype(BF16)
    v = v_ref[...].astype(BF16)
    for h in range(MEM_HEADS):
        c0 = h * MEM_HEAD_DIM
        c1 = c0 + MEM_HEAD_DIM
        logits = _dot(q[:, c0:c1], k[:, c0:c1], _NT) * (MEM_HEAD_DIM ** -0.5)
        p = jnp.exp(logits - jnp.max(logits, axis=-1, keepdims=True))
        probs = p * (1.0 / jnp.sum(p, axis=-1, keepdims=True))
        o_ref[:, c0:c1] = _dot(probs, v[:, c0:c1]).astype(BF16)


def mem_prompt(z, kv, *, batch, seq, tq):
    nt = seq // tq
    mblk = M_OFF // MEM_WIDTH
    return pl.pallas_call(
        _mem_prompt_kernel,
        grid=(batch, nt),
        in_specs=[pl.BlockSpec((tq, MEM_WIDTH), lambda b, i: (b * nt + i, mblk)),
                  pl.BlockSpec((N_MEM, MEM_WIDTH), lambda b, i: (b, 0)),
                  pl.BlockSpec((N_MEM, MEM_WIDTH), lambda b, i: (b, 1))],
        out_specs=pl.BlockSpec((tq, MEM_WIDTH), lambda b, i: (b * nt + i, 0)),
        out_shape=jax.ShapeDtypeStruct((batch * seq, MEM_WIDTH), BF16),
        compiler_params=_cparams(2),
        name="mem_prompt",
    )(z, kv, kv)


MEM_Q_ROWS = 16


def _mem_sample_kernel(q_ref, k_ref, v_ref, o_ref, *, bb):
    hrow = lax.broadcasted_iota(jnp.int32, (MEM_Q_ROWS, MEM_WIDTH), 0)
    hcol = lax.broadcasted_iota(jnp.int32, (MEM_Q_ROWS, MEM_WIDTH), 1)
    own = (hcol // MEM_HEAD_DIM) == hrow
    rows = []
    for b in range(bb):
        k = jnp.concatenate([k_ref[b, :, h, :] for h in range(MEM_HEADS)], axis=1)
        v = jnp.concatenate([v_ref[b, :, h, :] for h in range(MEM_HEADS)], axis=1)
        qbd = jnp.where(own, jnp.broadcast_to(q_ref[b:b + 1, :], (MEM_Q_ROWS, MEM_WIDTH)), 0.0)
        logits = _dot(qbd, k, _NT) * (MEM_HEAD_DIM ** -0.5)
        p = jnp.exp(logits - jnp.max(logits, axis=-1, keepdims=True))
        probs = p * (1.0 / jnp.sum(p, axis=-1, keepdims=True))
        o = jnp.where(own, _dot(probs, v), 0.0)
        rows.append(jnp.sum(o, axis=0, keepdims=True))
    o_ref[...] = jnp.concatenate(rows, axis=0)


def mem_sample(zs, mem_k, mem_v, layer, *, bb):
    nbatch = zs.shape[0]
    mblk = M_OFF // MEM_WIDTH
    kv_spec = pl.BlockSpec((None, bb, N_MEM, MEM_HEADS, MEM_HEAD_DIM), lambda i: (layer, i, 0, 0, 0))
    return pl.pallas_call(
        functools.partial(_mem_sample_kernel, bb=bb),
        grid=(nbatch // bb,),
        in_specs=[pl.BlockSpec((bb, MEM_WIDTH), lambda i: (i, mblk)), kv_spec, kv_spec],
        out_specs=pl.BlockSpec((bb, MEM_WIDTH), lambda i: (i, 0)),
        out_shape=jax.ShapeDtypeStruct((nbatch, MEM_WIDTH), F32),
        compiler_params=_cparams(1),
        name="mem_sample",
    )(zs, mem_k, mem_v)


def _to_row_major(x, o_ref, rows):
    for c in range(ROW_SUB):
        o_ref[pl.ds(c, rows, stride=ROW_SUB), :] = x[:, c * LANES:(c + 1) * LANES]


def _from_row_major(ref, first_row, rows, c, pitch=ROW_SUB):
    return ref[pl.ds(first_row + c, rows, stride=pitch), :]


def _norm_router_kernel(xp_ref, xs_ref, g_ref, r_ref, xn_ref, idx_ref, wgt_ref, *, tm, n_prompt_steps):
    x = jnp.where(pl.program_id(0) < n_prompt_steps, xp_ref[...], xs_ref[...])
    xn = _rms(x, g_ref[...])
    _to_row_major(xn, xn_ref, tm)
    logits = _dot(xn, r_ref[...])
    lane = lax.broadcasted_iota(jnp.int32, logits.shape, 1)
    logits = jnp.where(lane < N_EXPERTS, logits, -jnp.inf)
    v1 = jnp.max(logits, axis=-1, keepdims=True)
    i1 = jnp.min(jnp.where(logits == v1, lane, ROUTER_PAD), axis=-1, keepdims=True)
    rest = jnp.where(lane == i1, -jnp.inf, logits)
    v2 = jnp.max(rest, axis=-1, keepdims=True)
    i2 = jnp.min(jnp.where(rest == v2, lane, ROUTER_PAD), axis=-1, keepdims=True)
    e2 = jnp.exp(v2 - v1)
    den = 1.0 + e2
    idx_ref[...] = jnp.where(lane == 0, i1, jnp.where(lane == 1, i2, 0))
    wgt_ref[...] = jnp.where(lane == 0, 1.0 / den, jnp.where(lane == 1, e2 / den, 0.0))


def norm_router(xp, xs, g, router_pad, *, tm):
    m_prompt, d = xp.shape
    m = m_prompt + xs.shape[0]
    nps = m_prompt // tm
    return pl.pallas_call(
        functools.partial(_norm_router_kernel, tm=tm, n_prompt_steps=nps),
        grid=(m // tm,),
        in_specs=[pl.BlockSpec((tm, d), lambda i: (jnp.minimum(i, nps - 1), 0)),
                  pl.BlockSpec((tm, d), lambda i: (jnp.maximum(i - nps, 0), 0)),
                  pl.BlockSpec((1, d), lambda i: (0, 0)),
                  pl.BlockSpec((d, ROUTER_PAD), lambda i: (0, 0))],
        out_specs=[pl.BlockSpec((tm * ROW_SUB, LANES), lambda i: (i, 0)),
                   pl.BlockSpec((tm, ROUTER_PAD), lambda i: (i, 0)),
                   pl.BlockSpec((tm, ROUTER_PAD), lambda i: (i, 0))],
        out_shape=[jax.ShapeDtypeStruct((m * ROW_SUB, LANES), F32),
                   jax.ShapeDtypeStruct((m, ROUTER_PAD), jnp.int32),
                   jax.ShapeDtypeStruct((m, ROUTER_PAD), F32)],
        compiler_params=_cparams(1),
        name="norm_router",
    )(xp, xs, g, router_pad)


def _token_copy(src_hbm, token, buf, slot, sem):
    return pltpu.make_async_copy(src_hbm.at[pl.ds(token * ROW_SUB, ROW_SUB)],
                                 buf.at[pl.ds(slot * BUF_PITCH, ROW_SUB)], sem)


def _gather_loop(src_hbm, tok_ref, buf, sem, count, *, wait):
    def body(s, c):
        copy = _token_copy(src_hbm, tok_ref[0, 0, s], buf, s, sem)
        if wait:
            copy.wait()
        else:
            copy.start()
        return c

    lax.fori_loop(0, count, body, 0, unroll=GATHER_UNROLL)


def _gather_step(i, n_steps, src_hbm, tok_ref, next_tok_ref, bufs, sems, count, valid, next_valid, consume):
    for slot in range(2):
        @pl.when(i % 2 == slot)
        def _():
            @pl.when((i == 0) & valid)
            def _():
                _gather_loop(src_hbm, tok_ref, bufs.at[slot], sems.at[slot], count, wait=False)

            @pl.when((i + 1 < n_steps) & next_valid)
            def _():
                _gather_loop(src_hbm, next_tok_ref, bufs.at[1 - slot], sems.at[1 - slot], count, wait=False)

            @pl.when(valid)
            def _():
                _gather_loop(src_hbm, tok_ref, bufs.at[slot], sems.at[slot], count, wait=True)
                consume(bufs.at[slot])


def _gather_specs(nchunk, count):
    def cur(i, *_):
        return (i, 0, 0)

    def nxt(i, *_):
        return (jnp.minimum(i + 1, nchunk - 1), 0, 0)

    return [pl.BlockSpec((1, 1, count), cur, memory_space=pltpu.SMEM),
            pl.BlockSpec((1, 1, count), nxt, memory_space=pltpu.SMEM)]


def _gather_scratch(count):
    return [pltpu.VMEM((2, count * BUF_PITCH, LANES), F32), pltpu.SemaphoreType.DMA((2,))]


def _dispatch_kernel(used_ref, tok_ref, next_tok_ref, src_hbm, o_ref, bufs, sems, *, rows, n_steps):
    i = pl.program_id(0)
    valid = i * rows < used_ref[0]

    def consume(buf):
        for c in range(ROW_SUB):
            o_ref[:, c * LANES:(c + 1) * LANES] = _from_row_major(buf, 0, rows, c, BUF_PITCH).astype(BF16)

    _gather_step(i, n_steps, src_hbm, tok_ref, next_tok_ref, bufs, sems, rows,
                 valid, (i + 1) * rows < used_ref[0], consume)

    @pl.when(jnp.logical_not(valid))
    def _():
        o_ref[...] = jnp.zeros_like(o_ref)


def moe_dispatch(xn, src_token, used_rows, *, rows):
    n = src_token.shape[0]
    nchunk = n // rows
    tokens = src_token.reshape(nchunk, 1, rows)
    return pl.pallas_call(
        functools.partial(_dispatch_kernel, rows=rows, n_steps=nchunk),
        grid_spec=pltpu.PrefetchScalarGridSpec(
            num_scalar_prefetch=1,
            grid=(nchunk,),
            in_specs=_gather_specs(nchunk, rows) + [pl.BlockSpec(memory_space=pl.ANY)],
            out_specs=pl.BlockSpec((rows, D_MODEL), lambda i, used: (i, 0)),
            scratch_shapes=_gather_scratch(rows)),
        out_shape=jax.ShapeDtypeStruct((n, D_MODEL), BF16),
        compiler_params=_cparams(1),
        name="moe_dispatch",
    )(used_rows, tokens, tokens, xn)


STEP_IDLE, STEP_COMPUTE, STEP_ZERO = 0, 1, 2


def _moe_up_kernel(xt_ref, ot_ref, oj_ref, we_ref, wj_ref, cast_ref, mode_ref, ne_ref, nj_ref, more_ref,
                   x_ref, w1_hbm, w3_hbm, w2_ref, o_ref, w2b_ref, w1b_ref, w3b_ref, w1f_ref, w3f_ref, sems,
                   *, n_w2_blocks, tn):
    s = pl.program_id(0)

    def weight_copies(e, j):
        cols = pl.ds(pl.multiple_of(j * tn, tn), tn)
        return (pltpu.make_async_copy(w1_hbm.at[e, :, cols], w1f_ref, sems.at[0]),
                pltpu.make_async_copy(w3_hbm.at[e, :, cols], w3f_ref, sems.at[1]))

    @pl.when(cast_ref[s] > 0)
    def _():
        @pl.when(s == 0)
        def _():
            for copy in weight_copies(we_ref[0], wj_ref[0]):
                copy.start()

        for copy in weight_copies(we_ref[s], wj_ref[s]):
            copy.wait()
        w1b_ref[...] = w1f_ref[...].astype(BF16)
        w3b_ref[...] = w3f_ref[...].astype(BF16)

        @pl.when(more_ref[s] > 0)
        def _():
            for copy in weight_copies(ne_ref[s], nj_ref[s]):
                copy.start()

    @pl.when(s < n_w2_blocks)
    def _():
        w2b_ref[...] = w2_ref[...].astype(BF16)

    @pl.when(mode_ref[s] == STEP_COMPUTE)
    def _():
        x = x_ref[...]
        a = _dot(x, w1b_ref[...])
        b = _dot(x, w3b_ref[...])
        o_ref[...] = (jax.nn.silu(a) * b).astype(BF16)

    @pl.when(mode_ref[s] == STEP_ZERO)
    def _():
        o_ref[...] = jnp.zeros_like(o_ref)


def _moe_up_steps(tiles_per_expert, first_tile, *, n_tiles, nj):
    n_steps = (n_tiles + N_EXPERTS) * nj
    used = jnp.sum(tiles_per_expert)
    n_compute = used * nj
    n_zero = (n_tiles - used) * nj
    empty = tiles_per_expert == 0
    n_cast_only = jnp.sum(empty.astype(jnp.int32)) * nj
    n_live = n_compute + n_zero + n_cast_only
    a = jnp.minimum(jnp.arange(n_steps, dtype=jnp.int32), n_live - 1)
    blk_end = jnp.cumsum(tiles_per_expert * nj)
    e_c = jnp.minimum(jnp.sum((a[:, None] >= blk_end[None, :]).astype(jnp.int32), axis=1), N_EXPERTS - 1)
    onehot_c = (e_c[:, None] == jnp.arange(N_EXPERTS, dtype=jnp.int32)[None, :]).astype(jnp.int32)
    pick = lambda v: jnp.sum(onehot_c * v[None, :], axis=1)
    within = a - (pick(blk_end) - pick(tiles_per_expert) * nj)
    t_e = jnp.maximum(pick(tiles_per_expert), 1)
    j_c = within // t_e
    r_c = within % t_e
    tile_c = pick(first_tile) + r_c
    z = a - n_compute
    tile_z = used + z // nj
    j_z = z % nj
    b = a - n_compute - n_zero
    empty_rank = jnp.cumsum(empty.astype(jnp.int32)) - 1
    k = b // nj
    e_b = jnp.sum(jnp.where(empty[None, :] & (empty_rank[None, :] == k[:, None]),
                            jnp.arange(N_EXPERTS, dtype=jnp.int32)[None, :], 0), axis=1)
    j_b = b % nj
    is_c = a < n_compute
    is_z = (~is_c) & (a < n_compute + n_zero)
    is_b = (~is_c) & (~is_z)
    live = jnp.arange(n_steps, dtype=jnp.int32) < n_live
    last_tile = used - 1
    last_e = jnp.sum((last_tile >= jnp.cumsum(tiles_per_expert)).astype(jnp.int32))
    x_tile = jnp.where(is_c, tile_c, last_tile)
    o_tile = jnp.where(is_c, tile_c, jnp.where(is_z, tile_z, n_tiles - 1))
    o_col = jnp.where(is_c, j_c, jnp.where(is_z, j_z, nj - 1))
    w_e = jnp.where(is_c, e_c, jnp.where(is_b, e_b, last_e))
    w_col = jnp.where(is_c, j_c, jnp.where(is_b, j_b, nj - 1))
    cast = (live & ((is_c & (r_c == 0)) | is_b)).astype(jnp.int32)
    mode = jnp.where(live & is_c, STEP_COMPUTE, jnp.where(live & is_z, STEP_ZERO, STEP_IDLE))
    steps = jnp.arange(n_steps, dtype=jnp.int32)
    first_at_or_after = jnp.flip(lax.cummin(jnp.flip(jnp.where(cast > 0, steps, n_steps))))
    next_first = jnp.concatenate([first_at_or_after[1:], jnp.full((1,), n_steps, jnp.int32)])
    more = (next_first < n_steps).astype(jnp.int32)
    next_first = jnp.minimum(next_first, n_steps - 1)
    tables = (x_tile, o_tile, o_col, w_e, w_col, cast, mode, w_e[next_first], w_col[next_first], more)
    return tuple(t.astype(jnp.int32) for t in tables), n_steps


def moe_up(xs, w1, w3, w2, tiles_per_expert, first_tile, *, tm, tn):
    n, d = xs.shape
    f = w1.shape[2]
    nj = f // tn
    tables, n_steps = _moe_up_steps(tiles_per_expert, first_tile, n_tiles=n // tm, nj=nj)

    def x_map(s, xt, *_):
        return (xt[s], 0)

    def o_map(s, xt, ot, oj, *_):
        return (ot[s], oj[s])

    w2_slabs = w2.reshape(-1, W2_SLAB_ROWS, d)
    n_slabs = w2_slabs.shape[0]
    assert n_slabs <= n_steps

    def w2_map(s, *_):
        return (jnp.minimum(s, n_slabs - 1), 0, 0)

    h, w2b = pl.pallas_call(
        functools.partial(_moe_up_kernel, n_w2_blocks=n_slabs, tn=tn),
        grid_spec=pltpu.PrefetchScalarGridSpec(
            num_scalar_prefetch=len(tables),
            grid=(n_steps,),
            in_specs=[pl.BlockSpec((tm, d), x_map),
                      pl.BlockSpec(memory_space=pl.ANY),
                      pl.BlockSpec(memory_space=pl.ANY),
                      pl.BlockSpec((1, W2_SLAB_ROWS, d), w2_map)],
            out_specs=[pl.BlockSpec((tm, tn), o_map),
                       pl.BlockSpec((1, W2_SLAB_ROWS, d), w2_map)],
            scratch_shapes=[pltpu.VMEM((d, tn), BF16), pltpu.VMEM((d, tn), BF16),
                            pltpu.VMEM((d, tn), F32), pltpu.VMEM((d, tn), F32),
                            pltpu.SemaphoreType.DMA((2,))]),
        out_shape=[jax.ShapeDtypeStruct((n, f), BF16),
                   jax.ShapeDtypeStruct(w2_slabs.shape, BF16)],
        compiler_params=_cparams(1),
        name="moe_up",
    )(*tables, xs, w1, w3, w2_slabs)
    return h, w2b.reshape(w2.shape)


def _weight_map(nj):
    def w_map(t, j, te, tv):
        return (te[t], 0, jnp.where(tv[t] > 0, j, nj - 1))

    return w_map


def _moe_down_kernel(te_ref, tv_ref, h_ref, w2_ref, o_ref, *, tm, tn):
    j = pl.program_id(1)
    groups = tn // LANES

    def put(val):
        for c in range(groups):
            o_ref[pl.ds(j * groups + c, tm, stride=ROW_SUB), :] = val[:, c * LANES:(c + 1) * LANES]

    @pl.when(tv_ref[pl.program_id(0)] > 0)
    def _():
        put(_dot(h_ref[...], w2_ref[0]))

    @pl.when(tv_ref[pl.program_id(0)] == 0)
    def _():
        put(jnp.zeros((tm, tn), F32))


def moe_down(h, w2, tiles, *, tm, tn):
    n, f = h.shape
    d = w2.shape[2]
    nj = d // tn
    return pl.pallas_call(
        functools.partial(_moe_down_kernel, tm=tm, tn=tn),
        grid_spec=pltpu.PrefetchScalarGridSpec(
            num_scalar_prefetch=2,
            grid=(n // tm, nj),
            in_specs=[pl.BlockSpec((tm, f), lambda t, j, te, tv: (t, 0)),
                      pl.BlockSpec((1, f, tn), _weight_map(nj))],
            out_specs=pl.BlockSpec((tm * ROW_SUB, LANES), lambda t, j, te, tv: (t, 0))),
        out_shape=jax.ShapeDtypeStruct((n * ROW_SUB, LANES), F32),
        compiler_params=_cparams(2),
        name="moe_down",
    )(*tiles, h, w2)


def _combine_kernel(pos_ref, next_pos_ref, xp_ref, xs_ref, w_ref, g_ref, ys_hbm, yp_ref, ysmp_ref, bufs, sems,
                    *, rows, n_prompt_chunks, n_steps):
    i = pl.program_id(0)
    w0 = w_ref[:, 0:1]
    w1 = w_ref[:, 1:2]

    def finish(buf, x_ref, o_ref):
        for c in range(ROW_SUB):
            a = _from_row_major(buf, 0, rows, c, BUF_PITCH)
            b = _from_row_major(buf, rows * BUF_PITCH, rows, c, BUF_PITCH)
            o_ref[:, c * LANES:(c + 1) * LANES] = x_ref[:, c * LANES:(c + 1) * LANES] + (w0 * a + w1 * b)
        o_ref[...] = _rms(o_ref[...], g_ref[...])

    def consume(buf):
        @pl.when(i < n_prompt_chunks)
        def _():
            finish(buf, xp_ref, yp_ref)

        @pl.when(i >= n_prompt_chunks)
        def _():
            finish(buf, xs_ref, ysmp_ref)

    every_step = i >= 0
    _gather_step(i, n_steps, ys_hbm, pos_ref, next_pos_ref, bufs, sems, TOP_K * rows,
                 every_step, every_step, consume)


def moe_combine(xp, xs, wgt, g_final, ys, dest, *, rows):
    m_prompt, d = xp.shape
    m = m_prompt + xs.shape[0]
    nchunk = m // rows
    npc = m_prompt // rows
    pos = dest.reshape(nchunk, rows, TOP_K).transpose(0, 2, 1).reshape(nchunk, 1, TOP_K * rows)

    def p_map(i):
        return (jnp.minimum(i, npc - 1), 0)

    def s_map(i):
        return (jnp.maximum(i - npc, 0), 0)

    return pl.pallas_call(
        functools.partial(_combine_kernel, rows=rows, n_prompt_chunks=npc, n_steps=nchunk),
        grid=(nchunk,),
        in_specs=_gather_specs(nchunk, TOP_K * rows) + [
                  pl.BlockSpec((rows, d), p_map),
                  pl.BlockSpec((rows, d), s_map),
                  pl.BlockSpec((rows, ROUTER_PAD), lambda i: (i, 0)),
                  pl.BlockSpec((1, d), lambda i: (0, 0)),
                  pl.BlockSpec(memory_space=pl.ANY)],
        out_specs=[pl.BlockSpec((rows, d), p_map), pl.BlockSpec((rows, d), s_map)],
        out_shape=[jax.ShapeDtypeStruct((m_prompt, d), F32),
                   jax.ShapeDtypeStruct((m - m_prompt, d), F32)],
        scratch_shapes=_gather_scratch(TOP_K * rows),
        compiler_params=_cparams(1),
        name="moe_combine",
    )(pos, pos, xp, xs, wgt, g_final, ys)


def _routing_tables(idx2, *, tm, n_rows):
    e_flat = idx2.reshape(-1)
    onehot = (e_flat[:, None] == jnp.arange(N_EXPERTS, dtype=jnp.int32)[None, :]).astype(jnp.int32)
    csum = jnp.cumsum(onehot, axis=0)
    counts = csum[-1]
    rank = jnp.sum(csum * onehot, axis=1) - 1
    padded = ((counts + tm - 1) // tm) * tm
    ends = jnp.cumsum(padded)
    starts = ends - padded
    dest = (jnp.sum(onehot * starts[None, :], axis=1) + rank).astype(jnp.int32)
    token = jnp.arange(e_flat.shape[0], dtype=jnp.int32) // TOP_K
    src_token = jnp.zeros((n_rows,), jnp.int32).at[dest].set(token, unique_indices=True)
    tile_ids = jnp.arange(n_rows // tm, dtype=jnp.int32)
    used_tiles = ends[-1] // tm
    tile_valid = (tile_ids < used_tiles).astype(jnp.int32)
    tile_start = jnp.minimum(tile_ids, used_tiles - 1) * tm
    tile_expert = jnp.sum((tile_start[:, None] >= ends[None, :]).astype(jnp.int32), axis=1)
    per_expert = ((padded // tm).astype(jnp.int32), (starts // tm).astype(jnp.int32))
    return dest.reshape(-1, TOP_K), src_token, ends[-1:].astype(jnp.int32), (tile_expert, tile_valid), per_expert


def _rel_bias_by_distance(rel_bias_table):
    d = jnp.arange(WINDOW, dtype=jnp.int32)
    max_exact = REL_BUCKETS // 2
    d_f = jnp.maximum(d, 1).astype(F32)
    large = max_exact + (jnp.log(d_f / max_exact) / math.log(REL_MAX_DIST / max_exact)
                         * (REL_BUCKETS - max_exact)).astype(jnp.int32)
    large = jnp.minimum(large, REL_BUCKETS - 1)
    bucket = jnp.where(d < max_exact, d, large)
    return rel_bias_table[bucket].T


def _mixer_prompt(x, mem2d, wts, lw, layer, bias2, *, batch, seq):
    kv = norm_matmul(mem2d, lw["g_mem"], wts["w_mem_kv"], layer, tm=ROW_TILE, tn=COL_TILE)
    z = norm_matmul(x, lw["g_attn"], wts["w_in"], layer, tm=ROW_TILE, tn=COL_TILE, n_cols=G_OFF)
    ya = swa_prompt(z, bias2, lw["sinks"], batch=batch, seq=seq)
    yp = pool_prompt(z, lw["pool_maps"], lw["pool_scale"], batch=batch, seq=seq, tt=SEQ_TILE)
    ym = mem_prompt(z, kv, batch=batch, seq=seq, tq=SEQ_TILE)
    merged = merge_branches(x, lw["g_attn"], ya, yp, ym, wts["w_up_attn"], wts["w_up_pool"], wts["w_up_mem"],
                            wts["w_in"], layer, tm=ROW_TILE, tn=COL_TILE)
    x = matmul_residual(merged, wts["w_o"], layer, x, tm=ROW_TILE, tn=COL_TILE)
    z3 = z.reshape(batch, seq, G_OFF)
    new_k = z3[:, seq - WINDOW:, K_OFF:V_OFF].reshape(batch, WINDOW, N_KV_HEADS, HEAD_DIM)
    new_v = z3[:, seq - WINDOW:, V_OFF:P_OFF].reshape(batch, WINDOW, N_KV_HEADS, HEAD_DIM)
    new_pool = z3[:, seq - POOL_BUF:, P_OFF:M_OFF]
    kv3 = kv.reshape(batch, N_MEM, 2 * MEM_WIDTH)
    mk = kv3[..., :MEM_WIDTH].reshape(batch, N_MEM, MEM_HEADS, MEM_HEAD_DIM)
    mv = kv3[..., MEM_WIDTH:].reshape(batch, N_MEM, MEM_HEADS, MEM_HEAD_DIM)
    return x, new_k, new_v, new_pool, mk, mv


def _mixer_sample(x, cache_k, cache_v, state, mem_k, mem_v, wts, lw, layer, bias_s):
    nb = x.shape[0]
    z = norm_matmul(x, lw["g_attn"], wts["w_in"], layer, tm=nb, tn=COL_TILE, n_cols=G_OFF)
    q3 = z[:, Q_OFF:K_OFF].reshape(nb, N_HEADS, HEAD_DIM)
    ya3, new_k, new_v = swa_sample(q3, z, cache_k, cache_v, layer, bias_s, lw["sinks"][:, None], bb=SAMPLE_SEQS)
    yp, new_pool = pool_sample(z, state, lw["pool_maps"], lw["pool_scale"])
    ym = mem_sample(z, mem_k, mem_v, layer, bb=SAMPLE_SEQS)
    ya = ya3.reshape(nb, ATTN_WIDTH)
    merged = merge_branches(x, lw["g_attn"], ya, yp, ym, wts["w_up_attn"], wts["w_up_pool"], wts["w_up_mem"],
                            wts["w_in"], layer, tm=nb, tn=COL_TILE)
    x = matmul_residual(merged, wts["w_o"], layer, x, tm=nb, tn=COL_TILE)
    return (x, new_k.reshape(nb, WINDOW, N_KV_HEADS, HEAD_DIM),
            new_v.reshape(nb, WINDOW, N_KV_HEADS, HEAD_DIM), new_pool)


def _dense_ffn(x, g, w1, w3, w2, layer, *, tm):
    h = norm_swiglu(x, g, w1, w3, layer, tm=tm, tn=COL_TILE)
    return matmul_residual(h, w2, layer, x, tm=tm, tn=COL_TILE)


def _moe_ffn_final(xp, xs, g, router, w1, w3, w2, g_final):
    m_prompt = xp.shape[0]
    m = m_prompt + xs.shape[0]
    router_pad = jnp.pad(router, ((0, 0), (0, ROUTER_PAD - N_EXPERTS)))
    xn, idx_pad, wgt = norm_router(xp, xs, g, router_pad, tm=COMBINE_ROWS)
    n_rows = pl.cdiv(TOP_K * m, MOE_TM) * MOE_TM + N_EXPERTS * MOE_TM
    dest, src_token, used_rows, tiles, per_expert = _routing_tables(idx_pad[:, :TOP_K], tm=MOE_TM, n_rows=n_rows)
    xsort = moe_dispatch(xn, src_token, used_rows, rows=DISPATCH_ROWS)
    h, w2b = moe_up(xsort, w1, w3, w2, *per_expert, tm=MOE_TM, tn=COL_TILE)
    ys = moe_down(h, w2b, tiles, tm=MOE_TM, tn=COL_TILE)
    return moe_combine(xp, xs, wgt, g_final, ys, dest, rows=COMBINE_ROWS)


def kernel(x_prompt, x_sample, mem_prompt, cache_swa_k, cache_swa_v, state_pool, cache_mem_k, cache_mem_v, rel_bias_table, g_attn, w_in, attn_sinks, pool_maps, pool_scale, g_mem, w_mem_kv, w_up_attn, w_up_pool, w_up_mem, w_o, g_ffn, ffn_w1, ffn_w3, ffn_w2, moe_router, moe_w1, moe_w3, moe_w2, g_final):
    batch, seq, d = x_prompt.shape
    nb_s = x_sample.shape[0]
    depth = w_in.shape[0]
    assert depth == 2 and x_sample.shape[1] == 1

    bias_hw = _rel_bias_by_distance(rel_bias_table).astype(F32)
    col = jnp.arange(2 * WINDOW)
    bias2 = bias_hw[:, jnp.clip(WINDOW - col, 0, WINDOW - 1)]
    bias_s = bias_hw[:, ::-1]

    wts = dict(w_in=w_in.astype(BF16), w_mem_kv=w_mem_kv.astype(BF16), w_up_attn=w_up_attn.astype(BF16),
               w_up_pool=w_up_pool.astype(BF16), w_up_mem=w_up_mem.astype(BF16), w_o=w_o.astype(BF16))
    ffn = (ffn_w1.astype(BF16), ffn_w3.astype(BF16), ffn_w2.astype(BF16))

    xp = x_prompt.reshape(batch * seq, d)
    xs = x_sample.reshape(nb_s, d)
    mem2d = mem_prompt.reshape(batch * N_MEM, d)
    swa_k = cache_swa_k.reshape(depth, nb_s, WINDOW, KV_WIDTH)
    swa_v = cache_swa_v.reshape(depth, nb_s, WINDOW, KV_WIDTH)
    p_k, p_v, p_pool, p_mk, p_mv, s_k, s_v, s_pool = [], [], [], [], [], [], [], []
    for l in range(depth):
        lw = dict(g_attn=g_attn[l][None, :], sinks=attn_sinks[l], pool_maps=pool_maps[l].astype(BF16),
                  pool_scale=pool_scale[l][None, :], g_mem=g_mem[l][None, :])
        xp, nk, nv, npool, mk, mv = _mixer_prompt(xp, mem2d, wts, lw, l, bias2, batch=batch, seq=seq)
        p_k.append(nk); p_v.append(nv); p_pool.append(npool); p_mk.append(mk); p_mv.append(mv)
        xs, nk, nv, npool = _mixer_sample(xs, swa_k, swa_v, state_pool[l],
                                          cache_mem_k, cache_mem_v, wts, lw, l, bias_s)
        s_k.append(nk); s_v.append(nv); s_pool.append(npool)
        gl = g_ffn[l][None, :]
        i = l // 2
        if l % 2 == 0:
            xp = _dense_ffn(xp, gl, *ffn, i, tm=ROW_TILE)
            xs = _dense_ffn(xs, gl, *ffn, i, tm=nb_s)
        else:
            yp, ys = _moe_ffn_final(xp, xs, gl, moe_router[i], moe_w1[i], moe_w3[i], moe_w2[i], g_final[None, :])
    y_prompt = yp.reshape(batch, seq, d)
    y_sample = ys.reshape(nb_s, 1, d)
    return (y_prompt, y_sample,
            jnp.stack(p_k), jnp.stack(p_v), jnp.stack(p_pool), jnp.stack(p_mk), jnp.stack(p_mv),
            jnp.stack(s_k), jnp.stack(s_v), jnp.stack(s_pool))
```
